```python
import math
import jax, jax.numpy as jnp
from jax import lax
import numpy as np

D_MODEL = 1024
BATCH = 4
SEQ = 4096
DEPTH = 1
DEC_BATCH = 32
DEC_SEQ = 4
PAST_LEN = 16384
PAGE_SIZE = 128

MIX_ATTN = D_MODEL // 2
N_HEADS = 8
HEAD_DIM = MIX_ATTN // N_HEADS
N_KV_HEADS = 2
GROUP = N_HEADS // N_KV_HEADS
CONV_CH = D_MODEL - MIX_ATTN
CONV_W = 31
CMP_BLOCK = 32
CMP_STRIDE = 16
CMP_HID = 2 * HEAD_DIM
SLC_BLOCK = 64
TOP_K = 16
WINDOW = 512
Q_BLOCK = 128
D_FF = 4 * D_MODEL
PLE_DIM = 256
Q_COLS = N_HEADS * HEAD_DIM
KV_COLS = 2 * N_KV_HEADS * HEAD_DIM
GATE_COLS = 3 * N_HEADS
CONV_COLS = 2 * CONV_CH
N_IN = Q_COLS + 3 * KV_COLS + GATE_COLS + CONV_COLS
EPS = 1e-6
NEG = -1e30
FORCE_BONUS = 1e4

kernel_name = "nsa_conformer_parallel_heads_step"

F32 = jnp.float32


def rmsnorm(x, g):
    xf = x.astype(F32)
    y = xf * lax.rsqrt(jnp.mean(xf * xf, -1, keepdims=True) + EPS)
    return (y * g.astype(F32)).astype(x.dtype)


def layernorm(x, g, b):
    xf = x.astype(F32)
    mu = jnp.mean(xf, -1, keepdims=True)
    xc = xf - mu
    y = xc * lax.rsqrt(jnp.mean(xc * xc, -1, keepdims=True) + EPS)
    return (y * g.astype(F32) + b.astype(F32)).astype(x.dtype)


def alibi_slopes():
    s = 2.0 ** (-8.0 * np.arange(1, N_HEADS + 1) / N_HEADS)
    return jnp.asarray(s, F32).reshape(N_KV_HEADS, GROUP)


def masked_softmax(s, mask):
    s = jnp.where(mask, s, NEG)
    e = jnp.where(mask, jnp.exp(s - jnp.max(s, -1, keepdims=True)), 0.0)
    return e / jnp.maximum(jnp.sum(e, -1, keepdims=True), 1e-30)


def compress(kv, w1, w2, pe):
    B, T = kv.shape[:2]
    n_chunk = T // CMP_STRIDE
    c = kv[:, :n_chunk * CMP_STRIDE].reshape(B, n_chunk, CMP_STRIDE, 2, N_KV_HEADS, HEAD_DIM).astype(F32)
    lo = jnp.einsum('bnjchd,cjde->bnche', c, w1[:, :CMP_STRIDE])
    hi = jnp.einsum('bnjchd,cjde->bnche', c, w1[:, CMP_STRIDE:])
    pe_term = jnp.einsum('cjd,cjde->ce', pe, w1)
    pre = lo[:, :-1] + hi[:, 1:] + pe_term[:, None, :]
    return jnp.einsum('bnche,ced->bnchd', jax.nn.gelu(pre), w2).astype(F32)


def nsa_core(q, qpos, gates, cmp_kv, fetch, n_sel, win_kv, win_pos):
    B, Tq = q.shape[:2]
    slopes = alibi_slopes()[:, :, None, None]
    qg = q.reshape(B, Tq, N_KV_HEADS, GROUP, HEAD_DIM).astype(F32) * (HEAD_DIM ** -0.5)

    n_cmp = cmp_kv.shape[1]
    cmp_end = jnp.arange(n_cmp) * CMP_STRIDE + (CMP_BLOCK - 1)
    d_c = qpos[:, None] - cmp_end[None, :]
    s = jnp.einsum('btkgd,bnkd->bkgtn', qg, cmp_kv[:, :, 0]) - slopes * d_c.astype(F32)
    p_cmp = masked_softmax(s, d_c >= 0)
    o_cmp = jnp.einsum('bkgtn,bnkd->btkgd', p_cmp, cmp_kv[:, :, 1])

    ci = jnp.arange(n_cmp)[:, None] * CMP_STRIDE
    sj = jnp.arange(n_sel)[None, :] * SLC_BLOCK
    cover = ((ci + CMP_BLOCK > sj) & (ci < sj + SLC_BLOCK)).astype(F32)
    imp = jnp.einsum('bkgtn,nj->bktj', p_cmp, cover)
    jj = jnp.arange(n_sel)[None, :]
    cur = (qpos // SLC_BLOCK)[:, None]
    forced = (jj == 0) | (jj == cur) | (jj == cur - 1)
    score = jnp.where(jj * SLC_BLOCK <= qpos[:, None], imp, -1.0) + jnp.where(forced, FORCE_BONUS, 0.0)
    _, idx = lax.top_k(score, min(TOP_K, n_sel))
    sel = fetch(idx)
    kpos = idx[..., None] * SLC_BLOCK + jnp.arange(SLC_BLOCK)
    d_s = (qpos[:, None, None] - kpos).reshape(B, N_KV_HEADS, 1, Tq, -1)
    ksel = sel[..., 0, :].reshape(B, N_KV_HEADS, Tq, -1, HEAD_DIM).astype(F32)
    vsel = sel[..., 1, :].reshape(B, N_KV_HEADS, Tq, -1, HEAD_DIM).astype(F32)
    s = jnp.einsum('btkgd,bktmd->bkgtm', qg, ksel) - slopes * d_s.astype(F32)
    p = masked_softmax(s, d_s >= 0)
    o_slc = jnp.einsum('bkgtm,bktmd->btkgd', p, vsel)

    d_w = qpos[:, None] - win_pos[None, :]
    mask_w = (d_w >= 0) & (d_w < WINDOW) & (win_pos[None, :] >= 0)
    s = jnp.einsum('btkgd,bskd->bkgts', qg, win_kv[:, :, 0].astype(F32)) - slopes * d_w.astype(F32)
    p = masked_softmax(s, mask_w)
    o_win = jnp.einsum('bkgts,bskd->btkgd', p, win_kv[:, :, 1].astype(F32))

    g = gates.reshape(B, Tq, 3, N_KV_HEADS, GROUP, 1)
    o = g[:, :, 0] * o_cmp + g[:, :, 1] * o_slc + g[:, :, 2] * o_win
    return o.reshape(B, Tq, MIX_ATTN).astype(q.dtype)


def mixer_inputs(x, g, w_in):
    B, T = x.shape[:2]
    proj = rmsnorm(x, g) @ w_in
    cuts = [int(c) for c in np.cumsum([Q_COLS, KV_COLS, KV_COLS, KV_COLS, GATE_COLS])]
    q, kc, ks, kw, gt, cv = jnp.split(proj, cuts, axis=-1)
    kv = lambda a: a.reshape(B, T, 2, N_KV_HEADS, HEAD_DIM)
    gates = jax.nn.sigmoid(gt.astype(F32)).reshape(B, T, 3, N_HEADS)
    a, b = jnp.split(cv, 2, axis=-1)
    u = a * jax.nn.sigmoid(b)
    return q.reshape(B, T, N_HEADS, HEAD_DIM), kv(kc), kv(ks), kv(kw), gates, u


def conv_tail(hist, conv_w, conv_b, ln_g, ln_b):
    y = lax.conv_general_dilated(hist, conv_w[:, None, :].astype(hist.dtype), (1,), 'VALID',
                                 dimension_numbers=('NWC', 'WIO', 'NWC'),
                                 feature_group_count=CONV_CH) + conv_b
    y = layernorm(y, ln_g, ln_b)
    return y * jax.nn.sigmoid(y)


def layer_tail(x, mix, w_out, g_mlp, w_up, w_down, g_ple, w_ple, w_ple_gate, p):
    x = x + mix @ w_out
    x = x + jnp.square(jax.nn.relu(rmsnorm(x, g_mlp) @ w_up)) @ w_down
    x = x + (p @ w_ple) * jax.nn.sigmoid(rmsnorm(x, g_ple) @ w_ple_gate)
    return x


def nsa_prompt(q, kv_cmp, kv_slc, kv_win, gates, w1, w2, pe):
    B, T = q.shape[:2]
    cmp = compress(kv_cmp, w1, w2, pe)
    n_sel = T // SLC_BLOCK
    blocks = kv_slc.reshape(B, n_sel, SLC_BLOCK, 2, N_KV_HEADS, HEAD_DIM)
    bi = jnp.arange(B)[:, None, None, None]
    hi = jnp.arange(N_KV_HEADS)[None, :, None, None]
    fetch = lambda idx: blocks[bi, idx, :, :, hi, :]
    win_pad = jnp.pad(kv_win, ((0, 0), (WINDOW, 0), (0, 0), (0, 0), (0, 0)))

    def q_block(qs):
        qpos = qs + jnp.arange(Q_BLOCK)
        return nsa_core(lax.dynamic_slice_in_dim(q, qs, Q_BLOCK, 1), qpos,
                        lax.dynamic_slice_in_dim(gates, qs, Q_BLOCK, 1), cmp, fetch, n_sel,
                        lax.dynamic_slice_in_dim(win_pad, qs, WINDOW + Q_BLOCK, 1),
                        qs - WINDOW + jnp.arange(WINDOW + Q_BLOCK))

    out = lax.map(q_block, jnp.arange(T // Q_BLOCK, dtype=jnp.int32) * Q_BLOCK)
    return out.transpose(1, 0, 2, 3).reshape(B, T, MIX_ATTN)


def nsa_sample(q, kv_cmp, kv_slc, kv_win, gates, cache_cmp, cache_slc, cache_win, page_table, w1, w2, pe):
    DB, Tq = q.shape[:2]
    n_pages = page_table.shape[1]
    past_len = n_pages * PAGE_SIZE
    past_cmp = cache_cmp[page_table].reshape(DB, past_len, 2, N_KV_HEADS, HEAD_DIM)
    cmp = compress(jnp.concatenate([past_cmp, kv_cmp.astype(past_cmp.dtype)], 1), w1, w2, pe)

    nb_past = past_len // SLC_BLOCK
    nb_new = -(-Tq // SLC_BLOCK)
    n_sel = nb_past + nb_new
    bpp = PAGE_SIZE // SLC_BLOCK
    pool = cache_slc.reshape(-1, bpp, SLC_BLOCK, 2, N_KV_HEADS, HEAD_DIM)
    new_blocks = jnp.pad(kv_slc, ((0, 0), (0, nb_new * SLC_BLOCK - Tq), (0, 0), (0, 0), (0, 0)))
    new_blocks = new_blocks.reshape(DB, nb_new, SLC_BLOCK, 2, N_KV_HEADS, HEAD_DIM)
    bi = jnp.arange(DB)[:, None, None, None]
    hi = jnp.arange(N_KV_HEADS)[None, :, None, None]

    def fetch(idx):
        pj = jnp.minimum(idx, nb_past - 1)
        page = page_table[bi, pj // bpp]
        past = pool[page, pj % bpp, :, :, hi, :]
        new = new_blocks[bi, jnp.clip(idx - nb_past, 0, nb_new - 1), :, :, hi, :]
        return jnp.where((idx < nb_past)[..., None, None, None], past, new.astype(past.dtype))

    win_buf = cache_win.shape[1]
    win_all = jnp.concatenate([cache_win, kv_win.astype(cache_win.dtype)], 1)
    win_pos = past_len - win_buf + jnp.arange(win_buf + Tq)
    qpos = past_len + jnp.arange(Tq)
    out = nsa_core(q, qpos, gates, cmp, fetch, n_sel, win_all, win_pos)
    return out, win_all[:, -win_buf:]


def setup_inputs(seed: int = 0) -> dict:
    key = jax.random.key(seed)
    ks = jax.random.split(key, 32)
    nrm = lambda k, shape, scale: jax.random.normal(k, shape, F32) * scale
    n_pages = PAST_LEN // PAGE_SIZE
    used = DEC_BATCH * n_pages
    n_pool = used + max(1, used // 4)
    win_buf = min(WINDOW, PAST_LEN)
    page_table = jax.random.permutation(ks[0], n_pool)[:used].reshape(DEC_BATCH, n_pages).astype(jnp.int32)
    return {
        "x_prompt": nrm(ks[1], (BATCH, SEQ, D_MODEL), 1.0),
        "x_sample": nrm(ks[2], (DEC_BATCH, DEC_SEQ, D_MODEL), 1.0),
        "p_prompt": nrm(ks[3], (DEPTH, BATCH, SEQ, PLE_DIM), 1.0),
        "p_sample": nrm(ks[4], (DEPTH, DEC_BATCH, DEC_SEQ, PLE_DIM), 1.0),
        "cache_cmp_kv": nrm(ks[5], (DEPTH, n_pool, PAGE_SIZE, 2, N_KV_HEADS, HEAD_DIM), 1.0),
        "cache_slc_kv": nrm(ks[6], (DEPTH, n_pool, PAGE_SIZE, 2, N_KV_HEADS, HEAD_DIM), 1.0),
        "cache_win_kv": nrm(ks[7], (DEPTH, DEC_BATCH, win_buf, 2, N_KV_HEADS, HEAD_DIM), 1.0),
        "state_conv": nrm(ks[8], (DEPTH, DEC_BATCH, CONV_W - 1, CONV_CH), 0.5),
        "page_table": page_table,
        "g_attn": 1.0 + nrm(ks[9], (DEPTH, D_MODEL), 0.05),
        "w_in": nrm(ks[10], (DEPTH, D_MODEL, N_IN), D_MODEL ** -0.5),
        "w_cmp1": nrm(ks[11], (DEPTH, 2, CMP_BLOCK, HEAD_DIM, CMP_HID), (CMP_BLOCK * HEAD_DIM) ** -0.5),
        "w_cmp2": nrm(ks[12], (DEPTH, 2, CMP_HID, HEAD_DIM), CMP_HID ** -0.5),
        "pe_cmp": nrm(ks[13], (DEPTH, 2, CMP_BLOCK, HEAD_DIM), 0.1),
        "conv_w": nrm(ks[14], (DEPTH, CONV_W, CONV_CH), CONV_W ** -0.5),
        "conv_b": nrm(ks[15], (DEPTH, CONV_CH), 0.01),
        "ln_conv_g": 1.0 + nrm(ks[16], (DEPTH, CONV_CH), 0.05),
        "ln_conv_b": nrm(ks[17], (DEPTH, CONV_CH), 0.01),
        "w_out": nrm(ks[18], (DEPTH, D_MODEL, D_MODEL), D_MODEL ** -0.5),
        "g_mlp": 1.0 + nrm(ks[19], (DEPTH, D_MODEL), 0.05),
        "w_up": nrm(ks[20], (DEPTH, D_MODEL, D_FF), D_MODEL ** -0.5),
        "w_down": nrm(ks[21], (DEPTH, D_FF, D_MODEL), D_FF ** -0.5),
        "g_ple": 1.0 + nrm(ks[22], (DEPTH, D_MODEL), 0.05),
        "w_ple": nrm(ks[23], (DEPTH, PLE_DIM, D_MODEL), PLE_DIM ** -0.5),
        "w_ple_gate": nrm(ks[24], (DEPTH, D_MODEL, D_MODEL), D_MODEL ** -0.5),
        "g_final": 1.0 + nrm(ks[25], (D_MODEL,), 0.05),
    }


def reference(x_prompt, x_sample, p_prompt, p_sample, cache_cmp_kv, cache_slc_kv, cache_win_kv,
              state_conv, page_table, g_attn, w_in, w_cmp1, w_cmp2, pe_cmp, conv_w, conv_b,
              ln_conv_g, ln_conv_b, w_out, g_mlp, w_up, w_down, g_ple, w_ple, w_ple_gate, g_final):
    win_buf = cache_win_kv.shape[2]
    xp, xs = x_prompt, x_sample
    pc, psl, pw, pcv, sc, ssl, sw, scv = [], [], [], [], [], [], [], []
    for i in range(DEPTH):
        q, kc, ksl, kw, gt, u = mixer_inputs(xp, g_attn[i], w_in[i])
        attn = nsa_prompt(q, kc, ksl, kw, gt, w_cmp1[i], w_cmp2[i], pe_cmp[i])
        hist = jnp.pad(u, ((0, 0), (CONV_W - 1, 0), (0, 0)))
        conv = conv_tail(hist, conv_w[i], conv_b[i], ln_conv_g[i], ln_conv_b[i])
        xp = layer_tail(xp, jnp.concatenate([attn, conv], -1), w_out[i], g_mlp[i], w_up[i], w_down[i],
                        g_ple[i], w_ple[i], w_ple_gate[i], p_prompt[i])
        pc.append(kc)
        psl.append(ksl)
        pw.append(jnp.pad(kw, ((0, 0), (win_buf, 0), (0, 0), (0, 0), (0, 0)))[:, -win_buf:])
        pcv.append(hist[:, -(CONV_W - 1):])
        q, kc, ksl, kw, gt, u = mixer_inputs(xs, g_attn[i], w_in[i])
        attn, new_win = nsa_sample(q, kc, ksl, kw, gt, cache_cmp_kv[i], cache_slc_kv[i], cache_win_kv[i],
                                   page_table, w_cmp1[i], w_cmp2[i], pe_cmp[i])
        hist = jnp.concatenate([state_conv[i].astype(u.dtype), u], 1)
        conv = conv_tail(hist, conv_w[i], conv_b[i], ln_conv_g[i], ln_conv_b[i])
        xs = layer_tail(xs, jnp.concatenate([attn, conv], -1), w_out[i], g_mlp[i], w_up[i], w_down[i],
                        g_ple[i], w_ple[i], w_ple_gate[i], p_sample[i])
        sc.append(kc)
        ssl.append(ksl)
        sw.append(new_win)
        scv.append(hist[:, -(CONV_W - 1):])
    y_prompt = rmsnorm(xp, g_final)
    y_sample = rmsnorm(xs, g_final)
    return (y_prompt, y_sample, jnp.stack(pc), jnp.stack(psl), jnp.stack(pw), jnp.stack(pcv),
            jnp.stack(sc), jnp.stack(ssl), jnp.stack(sw), jnp.stack(scv))
```

```python
import functools

import numpy as np
import jax
import jax.numpy as jnp
from jax import lax
from jax.experimental import pallas as pl
from jax.experimental.pallas import tpu as pltpu

D_MODEL = 1024
N_HEADS = 8
HEAD_DIM = 64
N_KV_HEADS = 2
GROUP = N_HEADS // N_KV_HEADS
MIX_ATTN = N_HEADS * HEAD_DIM
CONV_CH = D_MODEL - MIX_ATTN
CONV_W = 31
CMP_BLOCK = 32
CMP_STRIDE = 16
CMP_HID = 2 * HEAD_DIM
SLC_BLOCK = 64
TOP_K = 16
WINDOW = 512
PAGE_SIZE = 128
D_FF = 4 * D_MODEL
PLE_DIM = 256
Q_COLS = N_HEADS * HEAD_DIM
KV_COLS = 2 * N_KV_HEADS * HEAD_DIM
GATE_COLS = 3 * N_HEADS
EPS = 1e-6
NEG = -1e30
FORCE_BONUS = 1e4

LANES = 128
Q_TILE = 128
KEY_TILE = 256
VMEM_LIMIT = 56 * 1024 * 1024

F32 = jnp.float32
BF16 = jnp.bfloat16


def _dot(a, b):
    return jnp.dot(a, b, preferred_element_type=F32)


def _dot_nt(a, b):
    return lax.dot_general(a, b, (((1,), (1,)), ((), ())), preferred_element_type=F32)


def _sigmoid(x):
    return 1.0 / (1.0 + jnp.exp(-x))


def _rms(x, g):
    return x * lax.rsqrt(jnp.mean(x * x, -1, keepdims=True) + EPS) * g


def _slopes():
    return 2.0 ** (-8.0 * np.arange(1, N_HEADS + 1) / N_HEADS)


def _proj_kernel(x_ref, g_ref, w_ref, cadd_ref, *out_refs, segs, seq_len):
    x = x_ref[...]
    xn = _rms(x, g_ref[...]).astype(BF16)
    tm = x.shape[0]
    row0 = pl.program_id(0) * tm
    for seg, o_ref in zip(segs, out_refs):
        kind, c0, width = seg[0], seg[1], seg[2]
        y = _dot(xn, w_ref[:, c0:c0 + width])
        if kind == "plain":
            pass
        elif kind == "const":
            y = y + cadd_ref[:, c0:c0 + width]
        elif kind == "sigmoid":
            y = _sigmoid(y)
        elif kind == "glu":
            c1 = seg[3]
            y = y * _sigmoid(_dot(xn, w_ref[:, c1:c1 + width]))
        elif kind == "kvaug":
            pos = (row0 + lax.broadcasted_iota(jnp.int32, (tm, width), 0)) % seq_len
            lane = lax.broadcasted_iota(jnp.int32, (tm, width), 1)
            is_key = lane < (width // 2)
            l = lane % LANES
            y = y + cadd_ref[:, c0:c0 + width]
            y = y + jnp.where(is_key & (l == HEAD_DIM), (pos // SLC_BLOCK).astype(F32), 0.0)
            y = y + jnp.where(is_key & (l == HEAD_DIM + 1), (pos % SLC_BLOCK).astype(F32), 0.0)
        o_ref[...] = y.astype(o_ref.dtype)


def _project(x2d, g, w_packed, cadd, segs, out_dtypes, tm, seq_len):
    rows = x2d.shape[0]
    nw = w_packed.shape[1]
    out_shape = [jax.ShapeDtypeStruct((rows, s[2]), dt) for s, dt in zip(segs, out_dtypes)]
    out_specs = [pl.BlockSpec((tm, s[2]), lambda i: (i, 0)) for s in segs]
    return pl.pallas_call(
        functools.partial(_proj_kernel, segs=tuple(segs), seq_len=seq_len),
        grid=(rows // tm,),
        in_specs=[
            pl.BlockSpec((tm, D_MODEL), lambda i: (i, 0)),
            pl.BlockSpec((1, D_MODEL), lambda i: (0, 0)),
            pl.BlockSpec((D_MODEL, nw), lambda i: (0, 0), pipeline_mode=pl.Buffered(1)),
            pl.BlockSpec((1, nw), lambda i: (0, 0)),
        ],
        out_specs=out_specs,
        out_shape=out_shape,
        compiler_params=pltpu.CompilerParams(
            dimension_semantics=("arbitrary",), vmem_limit_bytes=VMEM_LIMIT),
        name="proj",
    )(x2d, g, w_packed, cadd)


HALO = 32


def _conv_kernel(st_ref, u_ref, w_ref, cb_ref, lg_ref, lb_ref, o_ref, ext_ref, *, tt):
    @pl.when(pl.program_id(1) == 0)
    def _():
        ext_ref[0:HALO, :] = st_ref[0]

    ext_ref[HALO:HALO + tt, :] = u_ref[0]
    w = w_ref[...]
    y = jnp.zeros((tt, CONV_CH), F32) + cb_ref[...]
    for r in range(8):
        n_q = 5 if r == 0 else 4
        s = ext_ref[pl.ds(r, tt + 8 * (n_q - 1)), :]
        for q in range(n_q):
            k = 8 * q + r - 2
            if 0 <= k < CONV_W:
                y = y + w[k:k + 1, :] * s[8 * q:8 * q + tt, :]
    mu = jnp.mean(y, -1, keepdims=True)
    yc = y - mu
    yn = yc * lax.rsqrt(jnp.mean(yc * yc, -1, keepdims=True) + EPS) * lg_ref[...] + lb_ref[...]
    o_ref[0] = (yn * _sigmoid(yn)).astype(o_ref.dtype)
    ext_ref[0:HALO, :] = ext_ref[tt:tt + HALO, :]


def _conv_tail(state32, u, conv_w, conv_b, ln_g, ln_b, tt):
    b, t, _ = u.shape
    w32 = jnp.pad(conv_w, ((0, 1), (0, 0)))
    row = lambda a: a.reshape(1, CONV_CH)
    return pl.pallas_call(
        functools.partial(_conv_kernel, tt=tt),
        grid=(b, t // tt),
        in_specs=[
            pl.BlockSpec((1, HALO, CONV_CH), lambda i, j: (i, 0, 0)),
            pl.BlockSpec((1, tt, CONV_CH), lambda i, j: (i, j, 0)),
            pl.BlockSpec((HALO, CONV_CH), lambda i, j: (0, 0)),
            pl.BlockSpec((1, CONV_CH), lambda i, j: (0, 0)),
            pl.BlockSpec((1, CONV_CH), lambda i, j: (0, 0)),
            pl.BlockSpec((1, CONV_CH), lambda i, j: (0, 0)),
        ],
        out_specs=pl.BlockSpec((1, tt, CONV_CH), lambda i, j: (i, j, 0)),
        out_shape=jax.ShapeDtypeStruct((b, t, CONV_CH), BF16),
        scratch_shapes=[pltpu.VMEM((HALO + tt, CONV_CH), F32)],
        compiler_params=pltpu.CompilerParams(
            dimension_semantics=("arbitrary", "arbitrary"), vmem_limit_bytes=VMEM_LIMIT),
        name="conv_tail",
    )(state32, u, w32, row(conv_b), row(ln_g), row(ln_b))


def _gelu_tanh(x):
    return 0.5 * x * (1.0 + jnp.tanh(np.sqrt(2.0 / np.pi) * (x + 0.044715 * (x * x * x))))


def _compress_kernel(pt_ref, cache_ref, w1_ref, pe_ref, w1f_ref, w2k_ref, w2v_ref, o_ref,
                     xbuf, pet, sem, tsem, *, pages, n_groups, ppb):
    b = pl.program_id(0)
    gi = pl.program_id(1)
    rows = pages * PAGE_SIZE
    n = rows // CMP_STRIDE

    @pl.when((b == 0) & (gi == 0))
    def _():
        for c in range(2):
            t = lax.dot_general(pe_ref[c], w1f_ref[c], (((1,), (0,)), ((), ())),
                                precision=lax.Precision.HIGHEST, preferred_element_type=F32)
            pet[c] = jnp.concatenate([t, t], axis=1)

    def page_copy(p, c):
        page = pt_ref[b * ppb + gi * pages + p]
        return pltpu.make_async_copy(cache_ref.at[page, :, pl.ds(c * LANES, LANES)],
                                     xbuf.at[c, pl.ds(p * PAGE_SIZE, PAGE_SIZE)], sem)

    def tail_copy(c):
        page = pt_ref[b * ppb + (gi + 1) * pages]
        return pltpu.make_async_copy(cache_ref.at[page, pl.ds(0, CMP_STRIDE), pl.ds(c * LANES, LANES)],
                                     xbuf.at[c, pl.ds(rows, CMP_STRIDE)], tsem)

    for p in range(pages):
        for c in range(2):
            page_copy(p, c).start()

    @pl.when(gi + 1 < n_groups)
    def _():
        for c in range(2):
            tail_copy(c).start()

    @pl.when(gi + 1 >= n_groups)
    def _():
        for c in range(2):
            xbuf[c, rows:rows + CMP_STRIDE, :] = jnp.zeros((CMP_STRIDE, LANES), F32)

    for p in range(pages):
        for c in range(2):
            page_copy(p, c).wait()

    @pl.when(gi + 1 < n_groups)
    def _():
        for c in range(2):
            tail_copy(c).wait()

    nidx = gi * n + lax.broadcasted_iota(jnp.int32, (n, 2 * LANES), 0)
    cend = nidx * CMP_STRIDE + (CMP_BLOCK - 1)
    lane = lax.broadcasted_iota(jnp.int32, (n, 2 * LANES), 1) % LANES
    aug = (jnp.where(lane == HEAD_DIM, (cend // SLC_BLOCK).astype(F32), 0.0)
           + jnp.where(lane == HEAD_DIM + 1, (cend % SLC_BLOCK).astype(F32), 0.0))
    for c, w2_ref in ((0, w2k_ref), (1, w2v_ref)):
        lhs = jnp.concatenate(
            [xbuf[c, pl.ds(j, n, stride=CMP_STRIDE), :].astype(BF16)
             for j in range(CMP_BLOCK)], axis=1)
        pre = _dot(lhs, w1_ref[c]) + pet[c][0:1, :]
        h = _gelu_tanh(pre).astype(BF16)
        out = _dot(h, w2_ref[...])
        if c == 0:
            o_ref[0, :, 0:2 * LANES] = (out + aug).astype(o_ref.dtype)
        else:
            o_ref[0, :, 2 * LANES:] = out.astype(o_ref.dtype)


def _compress(page_table, cache, w1big, pe8, w1f, w2k, w2v, pages):
    nb, ppb = page_table.shape
    n_groups = ppb // pages
    n = pages * PAGE_SIZE // CMP_STRIDE
    ncols = 2 * LANES + w2v.shape[1]
    grid_spec = pltpu.PrefetchScalarGridSpec(
        num_scalar_prefetch=1,
        grid=(nb, n_groups),
        in_specs=[
            pl.BlockSpec(memory_space=pl.ANY),
            pl.BlockSpec(w1big.shape, lambda i, j, pt: (0, 0, 0)),
            pl.BlockSpec(pe8.shape, lambda i, j, pt: (0, 0, 0)),
            pl.BlockSpec(w1f.shape, lambda i, j, pt: (0, 0, 0)),
            pl.BlockSpec(w2k.shape, lambda i, j, pt: (0, 0)),
            pl.BlockSpec(w2v.shape, lambda i, j, pt: (0, 0)),
        ],
        out_specs=pl.BlockSpec((1, n, ncols), lambda i, j, pt: (i, j, 0)),
        scratch_shapes=[
            pltpu.VMEM((2, pages * PAGE_SIZE + CMP_STRIDE, LANES), F32),
            pltpu.VMEM((2, 8, 2 * CMP_HID), F32),
            pltpu.SemaphoreType.DMA(()),
            pltpu.SemaphoreType.DMA(()),
        ],
    )
    return pl.pallas_call(
        functools.partial(_compress_kernel, pages=pages, n_groups=n_groups, ppb=ppb),
        grid_spec=grid_spec,
        out_shape=jax.ShapeDtypeStruct((nb, n_groups * n, ncols), BF16),
        compiler_params=pltpu.CompilerParams(
            dimension_semantics=("arbitrary", "arbitrary"), vmem_limit_bytes=VMEM_LIMIT),
        name="compress",
    )(page_table.reshape(-1), cache, w1big, pe8, w1f, w2k, w2v)


def _compress_weights(w_cmp1, w_cmp2, pe_cmp, v_ext):
    w1 = w_cmp1
    z = jnp.zeros_like(w1)
    w1big = jnp.stack([jnp.concatenate([w1, z], -1), jnp.concatenate([z, w1], -1)], axis=2)
    w1big = w1big.reshape(2, CMP_BLOCK * 2 * HEAD_DIM, 2 * CMP_HID).astype(BF16)
    pe8 = jnp.broadcast_to(pe_cmp.reshape(2, 1, CMP_BLOCK * HEAD_DIM), (2, 8, CMP_BLOCK * HEAD_DIM))
    w1f = w1.reshape(2, CMP_BLOCK * HEAD_DIM, CMP_HID)

    def w2_layout(w2, group, offset):
        out = jnp.zeros((2, CMP_HID, 2, group), F32)
        for h in range(2):
            out = out.at[h, :, h, offset(h):offset(h) + HEAD_DIM].set(w2)
        return out.reshape(2 * CMP_HID, 2 * group).astype(BF16)

    w2k = w2_layout(w_cmp2[0], LANES, lambda h: 0)
    if v_ext:
        w2v = w2_layout(w_cmp2[1], KV_COLS, lambda h: 2 * HEAD_DIM + h * HEAD_DIM)
    else:
        w2v = w2_layout(w_cmp2[1], LANES, lambda h: 0)
    return w1big, pe8, w1f, w2k, w2v


def _flash(q, kv_ref, k, lo, hi, bias_fn):
    rows = q.shape[0]

    def body(step, carry):
        m, acc = carry
        kt = hi - 1 - step
        off = pl.multiple_of(kt * KEY_TILE, KEY_TILE)
        k_t = kv_ref[0, pl.ds(off, KEY_TILE), k * LANES:(k + 1) * LANES]
        v_t = kv_ref[0, pl.ds(off, KEY_TILE), (2 + k) * LANES:(3 + k) * LANES]
        s = _dot_nt(q, k_t) + bias_fn(kt)
        m_new = jnp.maximum(m, jnp.max(s, -1, keepdims=True))
        p = jnp.exp(s - m_new)
        acc = jnp.exp(m - m_new) * acc + _dot(p.astype(BF16), v_t)
        return m_new, acc

    m0 = jnp.full((rows, 1), 0.1 * NEG, F32)
    acc0 = jnp.zeros((rows, LANES), F32)
    _, acc = lax.fori_loop(0, hi - lo, body, (m0, acc0))
    return acc / jnp.maximum(acc[:, HEAD_DIM:HEAD_DIM + 1], 1e-30)


def _nsa_prompt_kernel(q_ref, gt_ref, cmp_ref, ks_ref, kw_ref, cov_ref, e_ref, o_ref):
    i = pl.program_id(1)
    qs = i * Q_TILE
    q = q_ref[0]
    gates = gt_ref[0]
    qpos_t = qs + lax.broadcasted_iota(jnp.int32, (Q_TILE, 1), 0)
    qpos_gt = jnp.concatenate([qpos_t] * GROUP, axis=0)
    n_cmp_pad = cmp_ref.shape[1]
    cend = lax.broadcasted_iota(jnp.int32, (1, n_cmp_pad), 1) * CMP_STRIDE + (CMP_BLOCK - 1)
    jl = lax.broadcasted_iota(jnp.int32, (1, LANES), 1)
    cur = qpos_t // SLC_BLOCK
    forced = (jl == 0) | (jl == cur) | (jl == cur - 1)
    n_sel = ks_ref.shape[1] // SLC_BLOCK
    jrow = lax.broadcasted_iota(jnp.int32, (n_sel, Q_TILE), 0)
    kiota = lax.broadcasted_iota(jnp.int32, (1, KEY_TILE), 1)

    for k in range(N_KV_HEADS):
        qk = jnp.concatenate(
            [q[:, (GROUP * k + g) * LANES:(GROUP * k + g + 1) * LANES] for g in range(GROUP)], axis=0)

        kc = cmp_ref[0, :, k * LANES:(k + 1) * LANES]
        vc = cmp_ref[0, :, (2 + k) * LANES:(3 + k) * LANES]
        valid = qpos_gt >= cend
        s = jnp.where(valid, _dot_nt(qk, kc), NEG)
        e = jnp.where(valid, jnp.exp(s - jnp.max(s, -1, keepdims=True)), 0.0)
        p = e / jnp.maximum(jnp.sum(e, -1, keepdims=True), 1e-30)
        o_cmp = _dot(p.astype(BF16), vc)

        psum = p[0:Q_TILE]
        for g in range(1, GROUP):
            psum = psum + p[g * Q_TILE:(g + 1) * Q_TILE]
        p_hi = psum.astype(BF16)
        p_lo = (psum - p_hi.astype(F32)).astype(BF16)
        imp = _dot(p_hi, cov_ref[...]) + _dot(p_lo, cov_ref[...])
        score = (jnp.where(jl * SLC_BLOCK <= qpos_t, imp, -1.0)
                 + jnp.where(forced, FORCE_BONUS, 0.0))
        sc_t = score.T[0:n_sel]
        cnt = jnp.zeros((n_sel, Q_TILE), jnp.int32)
        for jp in range(n_sel):
            row = sc_t[jp:jp + 1, :]
            tie = jnp.where(jrow > jp, 1, 0)
            cnt = cnt + jnp.where(row > sc_t, 1, jnp.where(row == sc_t, tie, 0))
        sel_t = jnp.where(cnt < TOP_K, 1.0, 0.0)
        sel_t = jnp.concatenate([sel_t, jnp.zeros((LANES - n_sel, Q_TILE), F32)], axis=0)
        sel = sel_t.T.astype(BF16)

        def slc_bias(kt, sel=sel):
            in_sel = _dot(sel, e_ref[kt])
            kpos = kt * KEY_TILE + kiota
            b_t = jnp.where((in_sel > 0.5) & (kpos <= qpos_t), 0.0, NEG)
            return jnp.concatenate([b_t] * GROUP, axis=0)

        def win_bias(kt):
            d = qpos_t - (kt * KEY_TILE + kiota)
            b_t = jnp.where((d >= 0) & (d < WINDOW), 0.0, NEG)
            return jnp.concatenate([b_t] * GROUP, axis=0)

        hi = i // 2 + 1
        o_slc = _flash(qk, ks_ref, k, 0, hi, slc_bias)
        o_win = _flash(qk, kw_ref, k, jnp.maximum(i - 4, 0) // 2, hi, win_bias)

        for g in range(GROUP):
            h = GROUP * k + g
            rs = slice(g * Q_TILE, (g + 1) * Q_TILE)
            o = (gates[:, h:h + 1] * o_cmp[rs]
                 + gates[:, N_HEADS + h:N_HEADS + h + 1] * o_slc[rs]
                 + gates[:, 2 * N_HEADS + h:2 * N_HEADS + h + 1] * o_win[rs])
            o_ref[0, :, h * LANES:(h + 1) * LANES] = o.astype(o_ref.dtype)


def _nsa_prompt(q_aug, gates, cmp, ks_aug, kw_aug, cover, expand):
    b, t, _ = q_aug.shape
    return pl.pallas_call(
        _nsa_prompt_kernel,
        grid=(b, t // Q_TILE),
        in_specs=[
            pl.BlockSpec((1, Q_TILE, N_HEADS * LANES), lambda i, j: (i, j, 0)),
            pl.BlockSpec((1, Q_TILE, LANES), lambda i, j: (i, j, 0)),
            pl.BlockSpec((1,) + cmp.shape[1:], lambda i, j: (i, 0, 0)),
            pl.BlockSpec((1,) + ks_aug.shape[1:], lambda i, j: (i, 0, 0)),
            pl.BlockSpec((1,) + kw_aug.shape[1:], lambda i, j: (i, 0, 0)),
            pl.BlockSpec(cover.shape, lambda i, j: (0, 0)),
            pl.BlockSpec(expand.shape, lambda i, j: (0, 0, 0)),
        ],
        out_specs=pl.BlockSpec((1, Q_TILE, N_HEADS * LANES), lambda i, j: (i, j, 0)),
        out_shape=jax.ShapeDtypeStruct((b, t, N_HEADS * LANES), BF16),
        compiler_params=pltpu.CompilerParams(
            dimension_semantics=("arbitrary", "arbitrary"), vmem_limit_bytes=VMEM_LIMIT),
        name="nsa_prompt",
    )(q_aug, gates, cmp, ks_aug, kw_aug, cover, expand)


def _selection_constants(n_cmp_pad, n_cmp, n_sel, seq_len):
    ci = np.arange(n_cmp_pad)[:, None] * CMP_STRIDE
    sj = np.arange(LANES)[None, :] * SLC_BLOCK
    cover = ((ci + CMP_BLOCK > sj) & (ci < sj + SLC_BLOCK)
             & (np.arange(n_cmp_pad)[:, None] < n_cmp) & (np.arange(LANES)[None, :] < n_sel))
    m = np.arange(seq_len).reshape(seq_len // KEY_TILE, 1, KEY_TILE)
    expand = (m // SLC_BLOCK) == np.arange(LANES).reshape(1, LANES, 1)
    return jnp.asarray(cover, BF16), jnp.asarray(expand, BF16)


FF_CHUNK = 1024


def _tail_kernel(*refs, n_mix, has_pre):
    x_ref = refs[0]
    mix_refs = refs[1:1 + n_mix]
    pos = 1 + n_mix
    pre_ref = refs[pos] if has_pre else None
    pos += int(has_pre)
    p_ref = refs[pos]
    wmix_refs = refs[pos + 1:pos + 1 + n_mix]
    (gmlp_ref, wup_ref, wdown_ref, gple_ref, wple_ref, wpg_ref, gfin_ref,
     y_ref) = refs[pos + 1 + n_mix:]
    x = x_ref[...]
    for m_ref, w_ref in zip(mix_refs, wmix_refs):
        x = x + _dot(m_ref[...], w_ref[...])
    if has_pre:
        x = x + pre_ref[...]
    xn = _rms(x, gmlp_ref[...]).astype(BF16)
    acc = jnp.zeros_like(x)
    for c in range(D_FF // FF_CHUNK):
        h = _dot(xn, wup_ref[:, c * FF_CHUNK:(c + 1) * FF_CHUNK])
        h = jnp.square(jnp.maximum(h, 0.0)).astype(BF16)
        acc = acc + _dot(h, wdown_ref[c * FF_CHUNK:(c + 1) * FF_CHUNK, :])
    x = x + acc
    gate = _sigmoid(_dot(_rms(x, gple_ref[...]).astype(BF16), wpg_ref[...]))
    x = x + _dot(p_ref[...].astype(BF16), wple_ref[...]) * gate
    y_ref[...] = _rms(x, gfin_ref[...])


def _layer_tail(x2d, mixes, wmixes, pre, p2d, g_mlp, w_up, w_down, g_ple, w_ple, w_pg, g_final, tm):
    rows = x2d.shape[0]
    row_spec = lambda a: pl.BlockSpec((tm, a.shape[1]), lambda i: (i, 0))
    const = lambda a: pl.BlockSpec(a.shape, lambda i: (0, 0), pipeline_mode=pl.Buffered(1))
    vec = lambda a: a.reshape(1, -1)
    has_pre = pre is not None
    args = [x2d, *mixes] + ([pre] if has_pre else []) + [p2d]
    specs = [row_spec(a) for a in args]
    consts = [*wmixes, vec(g_mlp), w_up, w_down, vec(g_ple), w_ple, w_pg, vec(g_final)]
    return pl.pallas_call(
        functools.partial(_tail_kernel, n_mix=len(mixes), has_pre=has_pre),
        grid=(rows // tm,),
        in_specs=specs + [const(a) for a in consts],
        out_specs=pl.BlockSpec((tm, D_MODEL), lambda i: (i, 0)),
        out_shape=jax.ShapeDtypeStruct((rows, D_MODEL), F32),
        compiler_params=pltpu.CompilerParams(
            dimension_semantics=("arbitrary",), vmem_limit_bytes=VMEM_LIMIT),
        name="layer_tail",
    )(*args, *consts)


def _pad_heads(w, n_groups):
    d = w.shape[0]
    return jnp.pad(w.reshape(d, n_groups, HEAD_DIM), ((0, 0), (0, 0), (0, LANES - HEAD_DIM))).reshape(
        d, n_groups * LANES)


def _proj_weights(w_in, with_kv_aug):
    c_q, c_kc, c_ks, c_kw, c_gt = np.cumsum([Q_COLS, KV_COLS, KV_COLS, KV_COLS, GATE_COLS])
    w_q = w_in[:, :c_q] * (HEAD_DIM ** -0.5)
    w_kc, w_ks, w_kw = w_in[:, c_q:c_kc], w_in[:, c_kc:c_ks], w_in[:, c_ks:c_kw]
    w_gt = jnp.pad(w_in[:, c_kw:c_gt], ((0, 0), (0, LANES - GATE_COLS)))
    w_a, w_b = w_in[:, c_gt:c_gt + CONV_CH], w_in[:, c_gt + CONV_CH:]
    pieces = [("q_aug", "const", _pad_heads(w_q, N_HEADS), BF16),
              ("kc", "plain", w_kc, F32), ("ks", "plain", w_ks, F32), ("kw", "plain", w_kw, F32),
              ("gates", "sigmoid", w_gt, F32),
              ("u", "glu", w_a, F32), (None, None, w_b, None)]
    if with_kv_aug:
        pieces += [("ks_aug", "kvaug", _pad_heads(w_ks, 4), BF16),
                   ("kw_aug", "kvaug", _pad_heads(w_kw, 4), BF16)]
    cols, segs, dtypes, names = [], [], [], []
    off = 0
    offsets = {}
    for name, kind, w, dt in pieces:
        offsets[name] = off
        if name is not None:
            seg = [kind, off, w.shape[1]]
            if kind == "glu":
                seg.append(off + w.shape[1])
            segs.append(tuple(seg))
            dtypes.append(dt)
            names.append(name)
        cols.append(w)
        off += w.shape[1]
    w_packed = jnp.concatenate(cols, axis=1).astype(BF16)
    cadd = np.zeros((1, off), np.float32)
    sl = _slopes()
    for h in range(N_HEADS):
        cadd[0, offsets["q_aug"] + h * LANES + HEAD_DIM] = SLC_BLOCK * sl[h]
        cadd[0, offsets["q_aug"] + h * LANES + HEAD_DIM + 1] = sl[h]
    if with_kv_aug:
        for nm in ("ks_aug", "kw_aug"):
            for h in range(N_KV_HEADS):
                cadd[0, offsets[nm] + (2 + h) * LANES + HEAD_DIM] = 1.0
    return w_packed, jnp.asarray(cadd), segs, dtypes, names


def _tmp_alibi():
    return jnp.asarray(_slopes(), F32).reshape(N_KV_HEADS, GROUP)


def _tmp_masked_softmax(s, mask):
    s = jnp.where(mask, s, NEG)
    e = jnp.where(mask, jnp.exp(s - jnp.max(s, -1, keepdims=True)), 0.0)
    return e / jnp.maximum(jnp.sum(e, -1, keepdims=True), 1e-30)


def _tmp_compress(kv, w1, w2, pe):
    B, T = kv.shape[:2]
    n_chunk = T // CMP_STRIDE
    c = kv[:, :n_chunk * CMP_STRIDE].reshape(B, n_chunk, CMP_STRIDE, 2, N_KV_HEADS, HEAD_DIM).astype(F32)
    lo = jnp.einsum('bnjchd,cjde->bnche', c, w1[:, :CMP_STRIDE])
    hi = jnp.einsum('bnjchd,cjde->bnche', c, w1[:, CMP_STRIDE:])
    pe_term = jnp.einsum('cjd,cjde->ce', pe, w1)
    pre = lo[:, :-1] + hi[:, 1:] + pe_term[:, None, :]
    return jnp.einsum('bnche,ced->bnchd', jax.nn.gelu(pre), w2).astype(F32)


def _tmp_nsa_core(q, qpos, gates, cmp_kv, fetch, n_sel, win_kv, win_pos):
    B, Tq = q.shape[:2]
    slopes = _tmp_alibi()[:, :, None, None]
    qg = q.reshape(B, Tq, N_KV_HEADS, GROUP, HEAD_DIM).astype(F32) * (HEAD_DIM ** -0.5)
    n_cmp = cmp_kv.shape[1]
    cmp_end = jnp.arange(n_cmp) * CMP_STRIDE + (CMP_BLOCK - 1)
    d_c = qpos[:, None] - cmp_end[None, :]
    s = jnp.einsum('btkgd,bnkd->bkgtn', qg, cmp_kv[:, :, 0]) - slopes * d_c.astype(F32)
    p_cmp = _tmp_masked_softmax(s, d_c >= 0)
    o_cmp = jnp.einsum('bkgtn,bnkd->btkgd', p_cmp, cmp_kv[:, :, 1])
    ci = jnp.arange(n_cmp)[:, None] * CMP_STRIDE
    sj = jnp.arange(n_sel)[None, :] * SLC_BLOCK
    cover = ((ci + CMP_BLOCK > sj) & (ci < sj + SLC_BLOCK)).astype(F32)
    imp = jnp.einsum('bkgtn,nj->bktj', p_cmp, cover)
    jj = jnp.arange(n_sel)[None, :]
    cur = (qpos // SLC_BLOCK)[:, None]
    forced = (jj == 0) | (jj == cur) | (jj == cur - 1)
    score = jnp.where(jj * SLC_BLOCK <= qpos[:, None], imp, -1.0) + jnp.where(forced, FORCE_BONUS, 0.0)
    _, idx = lax.top_k(score, min(TOP_K, n_sel))
    sel = fetch(idx)
    kpos = idx[..., None] * SLC_BLOCK + jnp.arange(SLC_BLOCK)
    d_s = (qpos[:, None, None] - kpos).reshape(B, N_KV_HEADS, 1, Tq, -1)
    ksel = sel[..., 0, :].reshape(B, N_KV_HEADS, Tq, -1, HEAD_DIM).astype(F32)
    vsel = sel[..., 1, :].reshape(B, N_KV_HEADS, Tq, -1, HEAD_DIM).astype(F32)
    s = jnp.einsum('btkgd,bktmd->bkgtm', qg, ksel) - slopes * d_s.astype(F32)
    p = _tmp_masked_softmax(s, d_s >= 0)
    o_slc = jnp.einsum('bkgtm,bktmd->btkgd', p, vsel)
    d_w = qpos[:, None] - win_pos[None, :]
    mask_w = (d_w >= 0) & (d_w < WINDOW) & (win_pos[None, :] >= 0)
    s = jnp.einsum('btkgd,bskd->bkgts', qg, win_kv[:, :, 0].astype(F32)) - slopes * d_w.astype(F32)
    p = _tmp_masked_softmax(s, mask_w)
    o_win = jnp.einsum('bkgts,bskd->btkgd', p, win_kv[:, :, 1].astype(F32))
    g = gates.reshape(B, Tq, 3, N_KV_HEADS, GROUP, 1)
    o = g[:, :, 0] * o_cmp + g[:, :, 1] * o_slc + g[:, :, 2] * o_win
    return o.reshape(B, Tq, MIX_ATTN).astype(q.dtype)


def _tmp_nsa_sample(q, kv_cmp, kv_slc, kv_win, gates, cache_cmp, cache_slc, cache_win, page_table, w1, w2, pe):
    DB, Tq = q.shape[:2]
    n_pages = page_table.shape[1]
    past_len = n_pages * PAGE_SIZE
    past_cmp = cache_cmp[page_table].reshape(DB, past_len, 2, N_KV_HEADS, HEAD_DIM)
    cmp = _tmp_compress(jnp.concatenate([past_cmp, kv_cmp.astype(past_cmp.dtype)], 1), w1, w2, pe)
    nb_past = past_len // SLC_BLOCK
    nb_new = -(-Tq // SLC_BLOCK)
    n_sel = nb_past + nb_new
    bpp = PAGE_SIZE // SLC_BLOCK
    pool = cache_slc.reshape(-1, bpp, SLC_BLOCK, 2, N_KV_HEADS, HEAD_DIM)
    new_blocks = jnp.pad(kv_slc, ((0, 0), (0, nb_new * SLC_BLOCK - Tq), (0, 0), (0, 0), (0, 0)))
    new_blocks = new_blocks.reshape(DB, nb_new, SLC_BLOCK, 2, N_KV_HEADS, HEAD_DIM)
    bi = jnp.arange(DB)[:, None, None, None]
    hi = jnp.arange(N_KV_HEADS)[None, :, None, None]

    def fetch(idx):
        pj = jnp.minimum(idx, nb_past - 1)
        page = page_table[bi, pj // bpp]
        flat = pool.reshape(-1, SLC_BLOCK * 2 * N_KV_HEADS * HEAD_DIM)
        past = jnp.take(flat, page * bpp + pj % bpp, axis=0).reshape(
            idx.shape + (SLC_BLOCK, 2, N_KV_HEADS, HEAD_DIM))
        past = jnp.where((hi == 0)[..., None, None, None], past[..., 0, :], past[..., 1, :])
        new = new_blocks[bi, jnp.clip(idx - nb_past, 0, nb_new - 1), :, :, hi, :]
        return jnp.where((idx < nb_past)[..., None, None, None], past, new.astype(past.dtype))

    win_buf = cache_win.shape[1]
    win_all = jnp.concatenate([cache_win, kv_win.astype(cache_win.dtype)], 1)
    win_pos = past_len - win_buf + jnp.arange(win_buf + Tq)
    qpos = past_len + jnp.arange(Tq)
    return _tmp_nsa_core(q, qpos, gates, cmp, fetch, n_sel, win_all, win_pos)


TQ_PAD = 8


def kernel(x_prompt, x_sample, p_prompt, p_sample, cache_cmp_kv, cache_slc_kv, cache_win_kv, state_conv, page_table, g_attn, w_in, w_cmp1, w_cmp2, pe_cmp, conv_w, conv_b, ln_conv_g, ln_conv_b, w_out, g_mlp, w_up, w_down, g_ple, w_ple, w_ple_gate, g_final):
    b, t, _ = x_prompt.shape
    db, tq, _ = x_sample.shape
    win_buf = cache_win_kv.shape[2]
    kv5 = lambda a, nb, nt: a.reshape(1, nb, nt, 2, N_KV_HEADS, HEAD_DIM)

    w_out_attn = jnp.pad(w_out[0][:MIX_ATTN].reshape(N_HEADS, HEAD_DIM, D_MODEL),
                         ((0, 0), (0, LANES - HEAD_DIM), (0, 0))).reshape(N_HEADS * LANES, D_MODEL).astype(BF16)
    w_out_conv = w_out[0][MIX_ATTN:].astype(BF16)
    tail_w = (g_mlp[0], w_up[0].astype(BF16), w_down[0].astype(BF16), g_ple[0],
              w_ple[0].astype(BF16), w_ple_gate[0].astype(BF16), g_final)
    g_row = g_attn[0].reshape(1, D_MODEL)

    w_packed, cadd, segs, dtypes, names = _proj_weights(w_in[0], True)
    po = dict(zip(names, _project(x_prompt.reshape(b * t, D_MODEL), g_row, w_packed, cadd, segs, dtypes, 512, t)))
    u_p = po["u"].reshape(b, t, CONV_CH)
    conv_p = _conv_tail(jnp.zeros((b, HALO, CONV_CH), F32), u_p, conv_w[0], conv_b[0],
                        ln_conv_g[0], ln_conv_b[0], 512)
    ppb = t // PAGE_SIZE
    cw = _compress_weights(w_cmp1[0], w_cmp2[0], pe_cmp[0], v_ext=False)
    cmp_p = _compress(jnp.arange(b * ppb, dtype=jnp.int32).reshape(b, ppb),
                      po["kc"].reshape(b * ppb, PAGE_SIZE, KV_COLS), *cw, pages=ppb)
    n_chunk = t // CMP_STRIDE
    cover, expand = _selection_constants(n_chunk, n_chunk - 1, t // SLC_BLOCK, t)
    attn_p = _nsa_prompt(po["q_aug"].reshape(b, t, -1), po["gates"].reshape(b, t, LANES), cmp_p,
                         po["ks_aug"].reshape(b, t, -1), po["kw_aug"].reshape(b, t, -1), cover, expand)
    y_prompt = _layer_tail(x_prompt.reshape(b * t, D_MODEL),
                           [attn_p.reshape(b * t, -1), conv_p.reshape(b * t, CONV_CH)],
                           [w_out_attn, w_out_conv], None, p_prompt[0].reshape(b * t, PLE_DIM),
                           *tail_w, tm=512).reshape(b, t, D_MODEL)

    xs = jnp.pad(x_sample, ((0, 0), (0, TQ_PAD - tq), (0, 0))).reshape(db * TQ_PAD, D_MODEL)
    w_packed_s, cadd_s, segs_s, dtypes_s, names_s = _proj_weights(w_in[0], False)
    so = dict(zip(names_s, _project(xs, g_row, w_packed_s, cadd_s, segs_s, dtypes_s, db * TQ_PAD, TQ_PAD)))
    rs = lambda a: a.reshape(db, TQ_PAD, -1)
    u_s = rs(so["u"])
    kc_s, ks_s, kw_s = rs(so["kc"])[:, :tq], rs(so["ks"])[:, :tq], rs(so["kw"])[:, :tq]
    state32 = jnp.pad(state_conv[0], ((0, 0), (HALO - (CONV_W - 1), 0), (0, 0)))
    conv_s = _conv_tail(state32, u_s, conv_w[0], conv_b[0], ln_conv_g[0], ln_conv_b[0], TQ_PAD)

    q_s = (rs(so["q_aug"]).reshape(db, TQ_PAD, N_HEADS, LANES)[:, :tq, :, :HEAD_DIM].astype(F32)
           * (HEAD_DIM ** 0.5))
    gates_s = rs(so["gates"])[:, :tq, :GATE_COLS].reshape(db, tq, 3, N_HEADS)
    attn_s = _tmp_nsa_sample(q_s, kv5(kc_s, db, tq)[0], kv5(ks_s, db, tq)[0], kv5(kw_s, db, tq)[0], gates_s,
                             cache_cmp_kv[0], cache_slc_kv[0], cache_win_kv[0], page_table,
                             w_cmp1[0], w_cmp2[0], pe_cmp[0])
    attn_s = jnp.pad(attn_s.reshape(db, tq, N_HEADS, HEAD_DIM),
                     ((0, 0), (0, TQ_PAD - tq), (0, 0), (0, LANES - HEAD_DIM)))
    p_s = jnp.pad(p_sample[0], ((0, 0), (0, TQ_PAD - tq), (0, 0))).reshape(db * TQ_PAD, PLE_DIM)
    y_sample = _layer_tail(xs, [attn_s.reshape(db * TQ_PAD, -1).astype(BF16), conv_s.reshape(db * TQ_PAD, CONV_CH)],
                           [w_out_attn, w_out_conv], None, p_s, *tail_w,
                           tm=db * TQ_PAD).reshape(db, TQ_PAD, D_MODEL)[:, :tq]

    kc_p, ks_p, kw_p = (po[n].reshape(b, t, KV_COLS) for n in ("kc", "ks", "kw"))
    new_win = jnp.concatenate([cache_win_kv[0].reshape(db, win_buf, KV_COLS), kw_s], 1)[:, -win_buf:]
    new_conv_s = jnp.concatenate([state_conv[0], u_s[:, :tq]], 1)[:, -(CONV_W - 1):]
    return (y_prompt, y_sample,
            kv5(kc_p, b, t), kv5(ks_p, b, t), kv5(kw_p[:, -win_buf:], b, win_buf),
            u_p[:, -(CONV_W - 1):][None],
            kv5(kc_s, db, tq), kv5(ks_s, db, tq), kv5(new_win, db, win_buf), new_conv_s[None])
```

```python
import functools

import numpy as np
import jax
import jax.numpy as jnp
from jax import lax
from jax.experimental import pallas as pl
from jax.experimental.pallas import tpu as pltpu

D_MODEL = 1024
N_HEADS = 8
HEAD_DIM = 64
N_KV_HEADS = 2
GROUP = N_HEADS // N_KV_HEADS
MIX_ATTN = N_HEADS * HEAD_DIM
CONV_CH = D_MODEL - MIX_ATTN
CONV_W = 31
CMP_BLOCK = 32
CMP_STRIDE = 16
CMP_HID = 2 * HEAD_DIM
SLC_BLOCK = 64
TOP_K = 16
WINDOW = 512
PAGE_SIZE = 128
D_FF = 4 * D_MODEL
PLE_DIM = 256
Q_COLS = N_HEADS * HEAD_DIM
KV_COLS = 2 * N_KV_HEADS * HEAD_DIM
GATE_COLS = 3 * N_HEADS
EPS = 1e-6
NEG = -1e30
FORCE_BONUS = 1e4

LANES = 128
Q_TILE = 128
KEY_TILE = 256
VMEM_LIMIT = 56 * 1024 * 1024

F32 = jnp.float32
BF16 = jnp.bfloat16


def _dot(a, b):
    return jnp.dot(a, b, preferred_element_type=F32)


def _dot_nt(a, b):
    return lax.dot_general(a, b, (((1,), (1,)), ((), ())), preferred_element_type=F32)


def _sigmoid(x):
    return 1.0 / (1.0 + jnp.exp(-x))


def _rms(x, g):
    return x * lax.rsqrt(jnp.mean(x * x, -1, keepdims=True) + EPS) * g


def _slopes():
    return 2.0 ** (-8.0 * np.arange(1, N_HEADS + 1) / N_HEADS)


def _proj_kernel(x_ref, g_ref, w_ref, cadd_ref, *out_refs, segs, seq_len):
    x = x_ref[...]
    xn = _rms(x, g_ref[...]).astype(BF16)
    tm = x.shape[0]
    row0 = pl.program_id(0) * tm
    for seg, o_ref in zip(segs, out_refs):
        kind, c0, width = seg[0], seg[1], seg[2]
        y = _dot(xn, w_ref[:, c0:c0 + width])
        if kind == "plain":
            pass
        elif kind == "const":
            y = y + cadd_ref[:, c0:c0 + width]
        elif kind == "sigmoid":
            y = _sigmoid(y)
        elif kind == "glu":
            c1 = seg[3]
            y = y * _sigmoid(_dot(xn, w_ref[:, c1:c1 + width]))
        elif kind == "kvaug":
            pos = (row0 + lax.broadcasted_iota(jnp.int32, (tm, width), 0)) % seq_len
            lane = lax.broadcasted_iota(jnp.int32, (tm, width), 1)
            is_key = lane < (width // 2)
            l = lane % LANES
            y = y + cadd_ref[:, c0:c0 + width]
            y = y + jnp.where(is_key & (l == HEAD_DIM), (pos // SLC_BLOCK).astype(F32), 0.0)
            y = y + jnp.where(is_key & (l == HEAD_DIM + 1), (pos % SLC_BLOCK).astype(F32), 0.0)
        o_ref[...] = y.astype(o_ref.dtype)


def _project(x2d, g, w_packed, cadd, segs, out_dtypes, tm, seq_len):
    rows = x2d.shape[0]
    nw = w_packed.shape[1]
    out_shape = [jax.ShapeDtypeStruct((rows, s[2]), dt) for s, dt in zip(segs, out_dtypes)]
    out_specs = [pl.BlockSpec((tm, s[2]), lambda i: (i, 0)) for s in segs]
    return pl.pallas_call(
        functools.partial(_proj_kernel, segs=tuple(segs), seq_len=seq_len),
        grid=(rows // tm,),
        in_specs=[
            pl.BlockSpec((tm, D_MODEL), lambda i: (i, 0)),
            pl.BlockSpec((1, D_MODEL), lambda i: (0, 0)),
            pl.BlockSpec((D_MODEL, nw), lambda i: (0, 0), pipeline_mode=pl.Buffered(1)),
            pl.BlockSpec((1, nw), lambda i: (0, 0)),
        ],
        out_specs=out_specs,
        out_shape=out_shape,
        compiler_params=pltpu.CompilerParams(
            dimension_semantics=("arbitrary",), vmem_limit_bytes=VMEM_LIMIT),
        name="proj",
    )(x2d, g, w_packed, cadd)


HALO = 32


def _conv_kernel(st_ref, u_ref, w_ref, cb_ref, lg_ref, lb_ref, o_ref, ext_ref, *, tt):
    @pl.when(pl.program_id(1) == 0)
    def _():
        ext_ref[0:HALO, :] = st_ref[0]

    ext_ref[HALO:HALO + tt, :] = u_ref[0]
    w = w_ref[...]
    y = jnp.zeros((tt, CONV_CH), F32) + cb_ref[...]
    for r in range(8):
        n_q = 5 if r == 0 else 4
        s = ext_ref[pl.ds(r, tt + 8 * (n_q - 1)), :]
        for q in range(n_q):
            k = 8 * q + r - 2
            if 0 <= k < CONV_W:
                y = y + w[k:k + 1, :] * s[8 * q:8 * q + tt, :]
    mu = jnp.mean(y, -1, keepdims=True)
    yc = y - mu
    yn = yc * lax.rsqrt(jnp.mean(yc * yc, -1, keepdims=True) + EPS) * lg_ref[...] + lb_ref[...]
    o_ref[0] = (yn * _sigmoid(yn)).astype(o_ref.dtype)
    ext_ref[0:HALO, :] = ext_ref[tt:tt + HALO, :]


def _conv_tail(state32, u, conv_w, conv_b, ln_g, ln_b, tt):
    b, t, _ = u.shape
    w32 = jnp.pad(conv_w, ((0, 1), (0, 0)))
    row = lambda a: a.reshape(1, CONV_CH)
    return pl.pallas_call(
        functools.partial(_conv_kernel, tt=tt),
        grid=(b, t // tt),
        in_specs=[
            pl.BlockSpec((1, HALO, CONV_CH), lambda i, j: (i, 0, 0)),
            pl.BlockSpec((1, tt, CONV_CH), lambda i, j: (i, j, 0)),
            pl.BlockSpec((HALO, CONV_CH), lambda i, j: (0, 0)),
            pl.BlockSpec((1, CONV_CH), lambda i, j: (0, 0)),
            pl.BlockSpec((1, CONV_CH), lambda i, j: (0, 0)),
            pl.BlockSpec((1, CONV_CH), lambda i, j: (0, 0)),
        ],
        out_specs=pl.BlockSpec((1, tt, CONV_CH), lambda i, j: (i, j, 0)),
        out_shape=jax.ShapeDtypeStruct((b, t, CONV_CH), BF16),
        scratch_shapes=[pltpu.VMEM((HALO + tt, CONV_CH), F32)],
        compiler_params=pltpu.CompilerParams(
            dimension_semantics=("arbitrary", "arbitrary"), vmem_limit_bytes=VMEM_LIMIT),
        name="conv_tail",
    )(state32, u, w32, row(conv_b), row(ln_g), row(ln_b))


def _gelu_tanh(x):
    return 0.5 * x * (1.0 + jnp.tanh(np.sqrt(2.0 / np.pi) * (x + 0.044715 * (x * x * x))))


def _compress_kernel(pt_ref, cache_ref, w1_ref, pe_ref, w1f_ref, w2k_ref, w2v_ref, o_ref,
                     xbuf, pet, sem, tsem, *stage, pages, n_groups, ppb, transposed):
    b = pl.program_id(0)
    gi = pl.program_id(1)
    rows = pages * PAGE_SIZE
    n = rows // CMP_STRIDE

    @pl.when((b == 0) & (gi == 0))
    def _():
        for c in range(2):
            t = lax.dot_general(pe_ref[c], w1f_ref[c], (((1,), (0,)), ((), ())),
                                precision=lax.Precision.HIGHEST, preferred_element_type=F32)
            pet[c] = jnp.concatenate([t, t], axis=1)

    has_next = gi + 1 < n_groups
    if transposed:
        stg = stage[0]

        def page_copies(p):
            page = pt_ref[b * ppb + gi * pages + p]
            return [pltpu.make_async_copy(cache_ref.at[page], stg.at[p], sem)]

        def tail_copies():
            page = pt_ref[b * ppb + (gi + 1) * pages]
            return [pltpu.make_async_copy(cache_ref.at[page], stg.at[pages], tsem)]
    else:
        def page_copies(p):
            page = pt_ref[b * ppb + gi * pages + p]
            return [pltpu.make_async_copy(cache_ref.at[page, :, pl.ds(c * LANES, LANES)],
                                          xbuf.at[c, pl.ds(p * PAGE_SIZE, PAGE_SIZE)], sem) for c in range(2)]

        def tail_copies():
            page = pt_ref[b * ppb + (gi + 1) * pages]
            return [pltpu.make_async_copy(cache_ref.at[page, pl.ds(0, CMP_STRIDE), pl.ds(c * LANES, LANES)],
                                          xbuf.at[c, pl.ds(rows, CMP_STRIDE)], tsem) for c in range(2)]

    for p in range(pages):
        for cp in page_copies(p):
            cp.start()

    @pl.when(has_next)
    def _():
        for cp in tail_copies():
            cp.start()

    for p in range(pages):
        for cp in page_copies(p):
            cp.wait()

    @pl.when(has_next)
    def _():
        for cp in tail_copies():
            cp.wait()

    if transposed:
        def untranspose(p, carry):
            x_t = stg[p]
            r0 = pl.multiple_of(p * PAGE_SIZE, PAGE_SIZE)
            for c in range(2):
                xbuf[c, pl.ds(r0, PAGE_SIZE), :] = x_t[c * LANES:(c + 1) * LANES, :].T
            return carry

        lax.fori_loop(0, pages, untranspose, 0)

        @pl.when(has_next)
        def _():
            x_t = stg[pages]
            for c in range(2):
                xbuf[c, rows:rows + CMP_STRIDE, :] = x_t[c * LANES:(c + 1) * LANES, :].T[0:CMP_STRIDE, :]

    @pl.when(jnp.logical_not(has_next))
    def _():
        for c in range(2):
            xbuf[c, rows:rows + CMP_STRIDE, :] = jnp.zeros((CMP_STRIDE, LANES), F32)

    nidx = gi * n + lax.broadcasted_iota(jnp.int32, (n, 2 * LANES), 0)
    cend = nidx * CMP_STRIDE + (CMP_BLOCK - 1)
    lane = lax.broadcasted_iota(jnp.int32, (n, 2 * LANES), 1) % LANES
    aug = (jnp.where(lane == HEAD_DIM, (cend // SLC_BLOCK).astype(F32), 0.0)
           + jnp.where(lane == HEAD_DIM + 1, (cend % SLC_BLOCK).astype(F32), 0.0))
    for c, w2_ref in ((0, w2k_ref), (1, w2v_ref)):
        lhs = jnp.concatenate(
            [xbuf[c, pl.ds(j, n, stride=CMP_STRIDE), :].astype(BF16)
             for j in range(CMP_BLOCK)], axis=1)
        pre = _dot(lhs, w1_ref[c]) + pet[c][0:1, :]
        h = _gelu_tanh(pre).astype(BF16)
        out = _dot(h, w2_ref[...])
        if c == 0:
            o_ref[0, :, 0:2 * LANES] = (out + aug).astype(o_ref.dtype)
        else:
            o_ref[0, :, 2 * LANES:] = out.astype(o_ref.dtype)


def _compress(page_table, cache, w1big, pe8, w1f, w2k, w2v, pages, transposed):
    nb, ppb = page_table.shape
    n_groups = ppb // pages
    n = pages * PAGE_SIZE // CMP_STRIDE
    ncols = 2 * LANES + w2v.shape[1]
    once = dict(pipeline_mode=pl.Buffered(1))
    scratch = [
        pltpu.VMEM((2, pages * PAGE_SIZE + CMP_STRIDE, LANES), F32),
        pltpu.VMEM((2, 8, 2 * CMP_HID), F32),
        pltpu.SemaphoreType.DMA(()),
        pltpu.SemaphoreType.DMA(()),
    ]
    if transposed:
        scratch.append(pltpu.VMEM((pages + 1, KV_COLS, PAGE_SIZE), F32))
    grid_spec = pltpu.PrefetchScalarGridSpec(
        num_scalar_prefetch=1,
        grid=(nb, n_groups),
        in_specs=[
            pl.BlockSpec(memory_space=pl.ANY),
            pl.BlockSpec(w1big.shape, lambda i, j, pt: (0, 0, 0), **once),
            pl.BlockSpec(pe8.shape, lambda i, j, pt: (0, 0, 0), **once),
            pl.BlockSpec(w1f.shape, lambda i, j, pt: (0, 0, 0), **once),
            pl.BlockSpec(w2k.shape, lambda i, j, pt: (0, 0), **once),
            pl.BlockSpec(w2v.shape, lambda i, j, pt: (0, 0), **once),
        ],
        out_specs=pl.BlockSpec((1, n, ncols), lambda i, j, pt: (i, j, 0)),
        scratch_shapes=scratch,
    )
    return pl.pallas_call(
        functools.partial(_compress_kernel, pages=pages, n_groups=n_groups, ppb=ppb,
                          transposed=transposed),
        grid_spec=grid_spec,
        out_shape=jax.ShapeDtypeStruct((nb, n_groups * n, ncols), BF16),
        compiler_params=pltpu.CompilerParams(
            dimension_semantics=("arbitrary", "arbitrary"), vmem_limit_bytes=VMEM_LIMIT),
        name="compress",
    )(page_table.reshape(-1), cache, w1big, pe8, w1f, w2k, w2v)


def _compress_weights(w_cmp1, w_cmp2, pe_cmp):
    w1 = w_cmp1
    z = jnp.zeros_like(w1)
    w1big = jnp.stack([jnp.concatenate([w1, z], -1), jnp.concatenate([z, w1], -1)], axis=2)
    w1big = w1big.reshape(2, CMP_BLOCK * 2 * HEAD_DIM, 2 * CMP_HID).astype(BF16)
    pe8 = jnp.broadcast_to(pe_cmp.reshape(2, 1, CMP_BLOCK * HEAD_DIM), (2, 8, CMP_BLOCK * HEAD_DIM))
    w1f = w1.reshape(2, CMP_BLOCK * HEAD_DIM, CMP_HID)

    def w2_layout(w2):
        out = jnp.zeros((2, CMP_HID, 2, LANES), F32)
        for h in range(2):
            out = out.at[h, :, h, 0:HEAD_DIM].set(w2)
        return out.reshape(2 * CMP_HID, 2 * LANES).astype(BF16)

    return w1big, pe8, w1f, w2_layout(w_cmp2[0]), w2_layout(w_cmp2[1])


def _flash(q, kv_ref, k, lo, hi, bias_fn):
    rows = q.shape[0]

    def body(step, carry):
        m, acc = carry
        kt = hi - 1 - step
        off = pl.multiple_of(kt * KEY_TILE, KEY_TILE)
        k_t = kv_ref[0, pl.ds(off, KEY_TILE), k * LANES:(k + 1) * LANES]
        v_t = kv_ref[0, pl.ds(off, KEY_TILE), (2 + k) * LANES:(3 + k) * LANES]
        s = _dot_nt(q, k_t) + bias_fn(kt)
        m_new = jnp.maximum(m, jnp.max(s, -1, keepdims=True))
        p = jnp.exp(s - m_new)
        acc = jnp.exp(m - m_new) * acc + _dot(p.astype(BF16), v_t)
        return m_new, acc

    m0 = jnp.full((rows, 1), 0.1 * NEG, F32)
    acc0 = jnp.zeros((rows, LANES), F32)
    _, acc = lax.fori_loop(0, hi - lo, body, (m0, acc0))
    return acc / jnp.maximum(acc[:, HEAD_DIM:HEAD_DIM + 1], 1e-30)


def _nsa_prompt_kernel(q_ref, gt_ref, cmp_ref, ks_ref, kw_ref, cov_ref, e_ref, o_ref):
    i = pl.program_id(1)
    qs = i * Q_TILE
    q = q_ref[0]
    gates = gt_ref[0]
    qpos_t = qs + lax.broadcasted_iota(jnp.int32, (Q_TILE, 1), 0)
    qpos_gt = jnp.concatenate([qpos_t] * GROUP, axis=0)
    n_cmp_pad = cmp_ref.shape[1]
    cend = lax.broadcasted_iota(jnp.int32, (1, n_cmp_pad), 1) * CMP_STRIDE + (CMP_BLOCK - 1)
    jl = lax.broadcasted_iota(jnp.int32, (1, LANES), 1)
    cur = qpos_t // SLC_BLOCK
    forced = (jl == 0) | (jl == cur) | (jl == cur - 1)
    n_sel = ks_ref.shape[1] // SLC_BLOCK
    jrow = lax.broadcasted_iota(jnp.int32, (n_sel, Q_TILE), 0)
    kiota = lax.broadcasted_iota(jnp.int32, (1, KEY_TILE), 1)

    for k in range(N_KV_HEADS):
        qk = jnp.concatenate(
            [q[:, (GROUP * k + g) * LANES:(GROUP * k + g + 1) * LANES] for g in range(GROUP)], axis=0)

        kc = cmp_ref[0, :, k * LANES:(k + 1) * LANES]
        vc = cmp_ref[0, :, (2 + k) * LANES:(3 + k) * LANES]
        valid = qpos_gt >= cend
        s = jnp.where(valid, _dot_nt(qk, kc), NEG)
        e = jnp.where(valid, jnp.exp(s - jnp.max(s, -1, keepdims=True)), 0.0)
        p = e / jnp.maximum(jnp.sum(e, -1, keepdims=True), 1e-30)
        o_cmp = _dot(p.astype(BF16), vc)

        psum = p[0:Q_TILE]
        for g in range(1, GROUP):
            psum = psum + p[g * Q_TILE:(g + 1) * Q_TILE]
        p_hi = psum.astype(BF16)
        p_lo = (psum - p_hi.astype(F32)).astype(BF16)
        imp = _dot(p_hi, cov_ref[...]) + _dot(p_lo, cov_ref[...])
        score = (jnp.where(jl * SLC_BLOCK <= qpos_t, imp, -1.0)
                 + jnp.where(forced, FORCE_BONUS, 0.0))
        sc_t = score.T[0:n_sel]
        cnt = jnp.zeros((n_sel, Q_TILE), jnp.int32)
        for jp in range(n_sel):
            row = sc_t[jp:jp + 1, :]
            tie = jnp.where(jrow > jp, 1, 0)
            cnt = cnt + jnp.where(row > sc_t, 1, jnp.where(row == sc_t, tie, 0))
        sel_t = jnp.where(cnt < TOP_K, 1.0, 0.0)
        sel_t = jnp.concatenate([sel_t, jnp.zeros((LANES - n_sel, Q_TILE), F32)], axis=0)
        sel = sel_t.T.astype(BF16)

        def slc_bias(kt, sel=sel):
            in_sel = _dot(sel, e_ref[kt])
            kpos = kt * KEY_TILE + kiota
            b_t = jnp.where((in_sel > 0.5) & (kpos <= qpos_t), 0.0, NEG)
            return jnp.concatenate([b_t] * GROUP, axis=0)

        def win_bias(kt):
            d = qpos_t - (kt * KEY_TILE + kiota)
            b_t = jnp.where((d >= 0) & (d < WINDOW), 0.0, NEG)
            return jnp.concatenate([b_t] * GROUP, axis=0)

        hi = i // 2 + 1
        o_slc = _flash(qk, ks_ref, k, 0, hi, slc_bias)
        o_win = _flash(qk, kw_ref, k, jnp.maximum(i - 4, 0) // 2, hi, win_bias)

        for g in range(GROUP):
            h = GROUP * k + g
            rs = slice(g * Q_TILE, (g + 1) * Q_TILE)
            o = (gates[:, h:h + 1] * o_cmp[rs]
                 + gates[:, N_HEADS + h:N_HEADS + h + 1] * o_slc[rs]
                 + gates[:, 2 * N_HEADS + h:2 * N_HEADS + h + 1] * o_win[rs])
            o_ref[0, :, h * LANES:(h + 1) * LANES] = o.astype(o_ref.dtype)


def _nsa_prompt(q_aug, gates, cmp, ks_aug, kw_aug, cover, expand):
    b, t, _ = q_aug.shape
    return pl.pallas_call(
        _nsa_prompt_kernel,
        grid=(b, t // Q_TILE),
        in_specs=[
            pl.BlockSpec((1, Q_TILE, N_HEADS * LANES), lambda i, j: (i, j, 0)),
            pl.BlockSpec((1, Q_TILE, LANES), lambda i, j: (i, j, 0)),
            pl.BlockSpec((1,) + cmp.shape[1:], lambda i, j: (i, 0, 0)),
            pl.BlockSpec((1,) + ks_aug.shape[1:], lambda i, j: (i, 0, 0)),
            pl.BlockSpec((1,) + kw_aug.shape[1:], lambda i, j: (i, 0, 0)),
            pl.BlockSpec(cover.shape, lambda i, j: (0, 0)),
            pl.BlockSpec(expand.shape, lambda i, j: (0, 0, 0)),
        ],
        out_specs=pl.BlockSpec((1, Q_TILE, N_HEADS * LANES), lambda i, j: (i, j, 0)),
        out_shape=jax.ShapeDtypeStruct((b, t, N_HEADS * LANES), BF16),
        compiler_params=pltpu.CompilerParams(
            dimension_semantics=("arbitrary", "arbitrary"), vmem_limit_bytes=VMEM_LIMIT),
        name="nsa_prompt",
    )(q_aug, gates, cmp, ks_aug, kw_aug, cover, expand)


def _selection_constants(n_cmp_pad, n_cmp, n_sel, seq_len):
    ci = np.arange(n_cmp_pad)[:, None] * CMP_STRIDE
    sj = np.arange(LANES)[None, :] * SLC_BLOCK
    cover = ((ci + CMP_BLOCK > sj) & (ci < sj + SLC_BLOCK)
             & (np.arange(n_cmp_pad)[:, None] < n_cmp) & (np.arange(LANES)[None, :] < n_sel))
    m = np.arange(seq_len).reshape(seq_len // KEY_TILE, 1, KEY_TILE)
    expand = (m // SLC_BLOCK) == np.arange(LANES).reshape(1, LANES, 1)
    return jnp.asarray(cover, BF16), jnp.asarray(expand, BF16)


FF_CHUNK = 1024


def _tail_kernel(*refs, n_mix, has_pre):
    x_ref = refs[0]
    mix_refs = refs[1:1 + n_mix]
    pos = 1 + n_mix
    pre_ref = refs[pos] if has_pre else None
    pos += int(has_pre)
    p_ref = refs[pos]
    wmix_refs = refs[pos + 1:pos + 1 + n_mix]
    (gmlp_ref, wup_ref, wdown_ref, gple_ref, wple_ref, wpg_ref, gfin_ref,
     y_ref) = refs[pos + 1 + n_mix:]
    x = x_ref[...]
    for m_ref, w_ref in zip(mix_refs, wmix_refs):
        x = x + _dot(m_ref[...], w_ref[...])
    if has_pre:
        x = x + pre_ref[...]
    xn = _rms(x, gmlp_ref[...]).astype(BF16)
    acc = jnp.zeros_like(x)
    for c in range(D_FF // FF_CHUNK):
        h = _dot(xn, wup_ref[:, c * FF_CHUNK:(c + 1) * FF_CHUNK])
        h = jnp.square(jnp.maximum(h, 0.0)).astype(BF16)
        acc = acc + _dot(h, wdown_ref[c * FF_CHUNK:(c + 1) * FF_CHUNK, :])
    x = x + acc
    gate = _sigmoid(_dot(_rms(x, gple_ref[...]).astype(BF16), wpg_ref[...]))
    x = x + _dot(p_ref[...].astype(BF16), wple_ref[...]) * gate
    y_ref[...] = _rms(x, gfin_ref[...])


def _layer_tail(x2d, mixes, wmixes, pre, p2d, g_mlp, w_up, w_down, g_ple, w_ple, w_pg, g_final, tm):
    rows = x2d.shape[0]
    row_spec = lambda a: pl.BlockSpec((tm, a.shape[1]), lambda i: (i, 0))
    const = lambda a: pl.BlockSpec(a.shape, lambda i: (0, 0), pipeline_mode=pl.Buffered(1))
    vec = lambda a: a.reshape(1, -1)
    has_pre = pre is not None
    args = [x2d, *mixes] + ([pre] if has_pre else []) + [p2d]
    specs = [row_spec(a) for a in args]
    consts = [*wmixes, vec(g_mlp), w_up, w_down, vec(g_ple), w_ple, w_pg, vec(g_final)]
    return pl.pallas_call(
        functools.partial(_tail_kernel, n_mix=len(mixes), has_pre=has_pre),
        grid=(rows // tm,),
        in_specs=specs + [const(a) for a in consts],
        out_specs=pl.BlockSpec((tm, D_MODEL), lambda i: (i, 0)),
        out_shape=jax.ShapeDtypeStruct((rows, D_MODEL), F32),
        compiler_params=pltpu.CompilerParams(
            dimension_semantics=("arbitrary",), vmem_limit_bytes=VMEM_LIMIT),
        name="layer_tail",
    )(*args, *consts)


def _pad_heads(w, n_groups):
    d = w.shape[0]
    return jnp.pad(w.reshape(d, n_groups, HEAD_DIM), ((0, 0), (0, 0), (0, LANES - HEAD_DIM))).reshape(
        d, n_groups * LANES)


def _proj_weights(w_in, with_kv_aug):
    c_q, c_kc, c_ks, c_kw, c_gt = np.cumsum([Q_COLS, KV_COLS, KV_COLS, KV_COLS, GATE_COLS])
    w_q = w_in[:, :c_q] * (HEAD_DIM ** -0.5)
    w_kc, w_ks, w_kw = w_in[:, c_q:c_kc], w_in[:, c_kc:c_ks], w_in[:, c_ks:c_kw]
    w_gt = jnp.pad(w_in[:, c_kw:c_gt], ((0, 0), (0, LANES - GATE_COLS)))
    w_a, w_b = w_in[:, c_gt:c_gt + CONV_CH], w_in[:, c_gt + CONV_CH:]
    pieces = [("q_aug", "const", _pad_heads(w_q, N_HEADS), BF16),
              ("kc", "plain", w_kc, F32), ("ks", "plain", w_ks, F32), ("kw", "plain", w_kw, F32),
              ("gates", "sigmoid", w_gt, F32),
              ("u", "glu", w_a, F32), (None, None, w_b, None)]
    if with_kv_aug:
        pieces += [("ks_aug", "kvaug", _pad_heads(w_ks, 4), BF16),
                   ("kw_aug", "kvaug", _pad_heads(w_kw, 4), BF16)]
    cols, segs, dtypes, names = [], [], [], []
    off = 0
    offsets = {}
    for name, kind, w, dt in pieces:
        offsets[name] = off
        if name is not None:
            seg = [kind, off, w.shape[1]]
            if kind == "glu":
                seg.append(off + w.shape[1])
            segs.append(tuple(seg))
            dtypes.append(dt)
            names.append(name)
        cols.append(w)
        off += w.shape[1]
    w_packed = jnp.concatenate(cols, axis=1).astype(BF16)
    cadd = np.zeros((1, off), np.float32)
    sl = _slopes()
    for h in range(N_HEADS):
        cadd[0, offsets["q_aug"] + h * LANES + HEAD_DIM] = SLC_BLOCK * sl[h]
        cadd[0, offsets["q_aug"] + h * LANES + HEAD_DIM + 1] = sl[h]
    if with_kv_aug:
        for nm in ("ks_aug", "kw_aug"):
            for h in range(N_KV_HEADS):
                cadd[0, offsets[nm] + (2 + h) * LANES + HEAD_DIM] = 1.0
    return w_packed, jnp.asarray(cadd), segs, dtypes, names


TQ_PAD = 8
SEL_LANES = 384
BIG = 1e9


def _masked_softmax(s, valid):
    s = jnp.where(valid, s, NEG)
    e = jnp.where(valid, jnp.exp(s - jnp.max(s, -1, keepdims=True)), 0.0)
    return e / jnp.maximum(jnp.sum(e, -1, keepdims=True), 1e-30)


def _rows_gt(x, k, width):
    return jnp.concatenate(
        [x[:, (GROUP * k + g) * LANES:(GROUP * k + g) * LANES + width] for g in range(GROUP)], axis=0)


def _sample_select_kernel(q_ref, cmp_ref, cov_ref, ocmp_ref, idx_ref, *, past_len, n_sel):
    q = q_ref[0].astype(F32)
    rows = GROUP * TQ_PAD
    t_gt = lax.broadcasted_iota(jnp.int32, (rows, 1), 0) % TQ_PAD
    n_cmp_pad = cmp_ref.shape[1]
    cend = lax.broadcasted_iota(jnp.int32, (1, n_cmp_pad), 1) * CMP_STRIDE + (CMP_BLOCK - 1)
    qpos_t = past_len + lax.broadcasted_iota(jnp.int32, (TQ_PAD, 1), 0)
    jl = lax.broadcasted_iota(jnp.int32, (1, SEL_LANES), 1)
    cur = qpos_t // SLC_BLOCK
    forced = (jl == 0) | (jl == cur) | (jl == cur - 1)
    scores = []
    for k in range(N_KV_HEADS):
        qk = _rows_gt(q, k, LANES).astype(BF16)
        kc = cmp_ref[0, :, k * LANES:(k + 1) * LANES]
        vc = cmp_ref[0, :, (2 + k) * LANES:(3 + k) * LANES]
        p = _masked_softmax(_dot_nt(qk, kc), (past_len + t_gt) >= cend)
        ocmp_ref[0, k] = _dot(p.astype(BF16), vc)
        psum = p[0:TQ_PAD]
        for g in range(1, GROUP):
            psum = psum + p[g * TQ_PAD:(g + 1) * TQ_PAD]
        p_hi = psum.astype(BF16)
        p_lo = (psum - p_hi.astype(F32)).astype(BF16)
        imp = _dot(p_hi, cov_ref[...]) + _dot(p_lo, cov_ref[...])
        score = (jnp.where(jl * SLC_BLOCK <= qpos_t, imp, -1.0) + jnp.where(forced, FORCE_BONUS, 0.0))
        scores.append(jnp.where(jl < n_sel, score, -BIG))
    sc = jnp.concatenate(scores, axis=0)
    lane = lax.broadcasted_iota(jnp.int32, sc.shape, 1).astype(F32)
    out_lane = lax.broadcasted_iota(jnp.int32, (sc.shape[0], LANES), 1)
    picked = jnp.zeros((sc.shape[0], LANES), F32)
    for s in range(TOP_K):
        m = jnp.max(sc, -1, keepdims=True)
        am = jnp.min(jnp.where(sc == m, lane, BIG), -1, keepdims=True)
        picked = jnp.where(out_lane == s, am, picked)
        sc = jnp.where(lane == am, -2.0 * BIG, sc)
    idx_ref[0] = picked.astype(jnp.int32)


def _sample_select(q_s, cmp_s, cover_s, past_len, n_sel):
    db = q_s.shape[0]
    return pl.pallas_call(
        functools.partial(_sample_select_kernel, past_len=past_len, n_sel=n_sel),
        grid=(db,),
        in_specs=[
            pl.BlockSpec((1,) + q_s.shape[1:], lambda i: (i, 0, 0)),
            pl.BlockSpec((1,) + cmp_s.shape[1:], lambda i: (i, 0, 0)),
            pl.BlockSpec(cover_s.shape, lambda i: (0, 0)),
        ],
        out_specs=[
            pl.BlockSpec((1, N_KV_HEADS, GROUP * TQ_PAD, LANES), lambda i: (i, 0, 0, 0)),
            pl.BlockSpec((1, N_KV_HEADS * TQ_PAD, LANES), lambda i: (i, 0, 0)),
        ],
        out_shape=[
            jax.ShapeDtypeStruct((db, N_KV_HEADS, GROUP * TQ_PAD, LANES), F32),
            jax.ShapeDtypeStruct((db, N_KV_HEADS * TQ_PAD, LANES), jnp.int32),
        ],
        compiler_params=pltpu.CompilerParams(
            dimension_semantics=("arbitrary",), vmem_limit_bytes=VMEM_LIMIT),
        name="sample_select",
    )(q_s, cmp_s, cover_s)


def _joint_softmax(s_a, valid_a, s_b, valid_b):
    s_a = jnp.where(valid_a, s_a, NEG)
    s_b = jnp.where(valid_b, s_b, NEG)
    m = jnp.maximum(jnp.max(s_a, -1, keepdims=True), jnp.max(s_b, -1, keepdims=True))
    e_a = jnp.where(valid_a, jnp.exp(s_a - m), 0.0)
    e_b = jnp.where(valid_b, jnp.exp(s_b - m), 0.0)
    inv = 1.0 / jnp.maximum(jnp.sum(e_a, -1, keepdims=True) + jnp.sum(e_b, -1, keepdims=True), 1e-30)
    return e_a * inv, e_b * inv


def _sample_attn_kernel(pt_ref, idx_ref, q_ref, gt_ref, ocmp_ref, ksn_ref, kwn_ref, slc_ref, win_ref,
                        wout_ref, o_ref, kbuf, vbuf, sem, *, past_len, tq, ppb):
    b = pl.program_id(0)
    n_past_blocks = past_len // SLC_BLOCK
    bpp = PAGE_SIZE // SLC_BLOCK
    rows = GROUP * TQ_PAD

    def sel_index(k, t, s):
        return idx_ref[((b * N_KV_HEADS + k) * TQ_PAD + t) * TOP_K + s]

    def block_copies(k, t, s):
        blk = jnp.minimum(sel_index(k, t, s), n_past_blocks - 1)
        page = pt_ref[b * ppb + blk // bpp]
        dst = pl.ds(s * PAGE_SIZE, PAGE_SIZE)
        return [
            pltpu.make_async_copy(slc_ref.at[page, pl.ds(k * HEAD_DIM, HEAD_DIM), :],
                                  kbuf.at[k, t, :, dst], sem),
            pltpu.make_async_copy(slc_ref.at[page, pl.ds((N_KV_HEADS + k) * HEAD_DIM, HEAD_DIM), :],
                                  vbuf.at[k, t, :, dst], sem),
        ]

    for k in range(N_KV_HEADS):
        for t in range(tq):
            for s in range(TOP_K):
                for cp in block_copies(k, t, s):
                    cp.start()

    q = q_ref[0].astype(F32)
    gates = gt_ref[0]
    t_gt = lax.broadcasted_iota(jnp.int32, (rows, 1), 0) % TQ_PAD
    g_gt = lax.broadcasted_iota(jnp.int32, (rows, 1), 0) // TQ_PAD
    qpos = past_len + t_gt
    lane_pg = lax.broadcasted_iota(jnp.int32, (1, PAGE_SIZE), 1)
    new_pos = past_len + lax.broadcasted_iota(jnp.int32, (1, TQ_PAD), 1)
    win_len = win_ref.shape[2]
    win_pos = past_len - win_len + lax.broadcasted_iota(jnp.int32, (1, win_len), 1)
    slopes = _slopes()

    for k in range(N_KV_HEADS):
        for t in range(tq):
            for s in range(TOP_K):
                for cp in block_copies(k, t, s):
                    cp.wait()

    acc = jnp.zeros((TQ_PAD, D_MODEL), F32)
    for k in range(N_KV_HEADS):
        qk = _rows_gt(q, k, HEAD_DIM).astype(BF16)
        slope = jnp.zeros((rows, 1), F32)
        for g in range(GROUP):
            slope = jnp.where(g_gt == g, float(slopes[GROUP * k + g]), slope)
        ks_new = ksn_ref[0, :, k * HEAD_DIM:(k + 1) * HEAD_DIM].astype(BF16)
        vs_new = ksn_ref[0, :, (N_KV_HEADS + k) * HEAD_DIM:(N_KV_HEADS + k + 1) * HEAD_DIM].astype(BF16)
        kw_new = kwn_ref[0, :, k * HEAD_DIM:(k + 1) * HEAD_DIM].astype(BF16)
        vw_new = kwn_ref[0, :, (N_KV_HEADS + k) * HEAD_DIM:(N_KV_HEADS + k + 1) * HEAD_DIM].astype(BF16)

        s_new = _dot_nt(qk, ks_new) - slope * (qpos - new_pos).astype(F32)
        o_slc = jnp.zeros((rows, HEAD_DIM), F32)
        for t in range(tq):
            kpos, chosen, n_new = [], [], 0
            for s in range(TOP_K):
                blk = sel_index(k, t, s)
                is_past = blk < n_past_blocks
                page_blk = jnp.minimum(blk, n_past_blocks - 1)
                kpos.append((page_blk // bpp) * PAGE_SIZE + lane_pg)
                half = jnp.where(is_past, page_blk % bpp, -1)
                chosen.append(jnp.where((lane_pg // SLC_BLOCK) == half, 1, 0))
                n_new = n_new + jnp.where(is_past, 0, 1)
            kpos = jnp.concatenate(kpos, axis=1)
            chosen = jnp.concatenate(chosen, axis=1)
            mine = t_gt == t
            valid = mine & (chosen > 0) & (kpos <= qpos)
            valid_new = mine & (new_pos <= qpos) & ((new_pos * 0 + n_new) > 0)
            s_past = _dot(qk, kbuf[k, t].astype(BF16)) - slope * (qpos - kpos).astype(F32)
            p_past, p_new = _joint_softmax(s_past, valid, s_new, valid_new)
            o_slc = (o_slc + _dot_nt(p_past.astype(BF16), vbuf[k, t].astype(BF16))
                     + _dot(p_new.astype(BF16), vs_new))

        d_w = qpos - win_pos
        d_n = qpos - new_pos
        s_w = _dot(qk, win_ref[0, k * HEAD_DIM:(k + 1) * HEAD_DIM, :].astype(BF16)) - slope * d_w.astype(F32)
        s_n = _dot_nt(qk, kw_new) - slope * d_n.astype(F32)
        p_w, p_n = _joint_softmax(s_w, (d_w >= 0) & (d_w < WINDOW) & (win_pos >= 0),
                                  s_n, (d_n >= 0) & (d_n < WINDOW))
        v_w = win_ref[0, (N_KV_HEADS + k) * HEAD_DIM:(N_KV_HEADS + k + 1) * HEAD_DIM, :].astype(BF16)
        o_win = _dot_nt(p_w.astype(BF16), v_w) + _dot(p_n.astype(BF16), vw_new)

        def gate(branch):
            return jnp.concatenate(
                [gates[:, branch * N_HEADS + GROUP * k + g:branch * N_HEADS + GROUP * k + g + 1]
                 for g in range(GROUP)], axis=0)

        o = (gate(0) * ocmp_ref[0, k][:, 0:HEAD_DIM] + gate(1) * o_slc + gate(2) * o_win).astype(BF16)
        for g in range(GROUP):
            h = GROUP * k + g
            acc = acc + _dot(o[g * TQ_PAD:(g + 1) * TQ_PAD], wout_ref[h * HEAD_DIM:(h + 1) * HEAD_DIM, :])
    o_ref[0] = acc


def _sample_attn(page_table, idx, q_s, gates_s, ocmp, ks_new, kw_new, slc_t, win_t, w_out_attn,
                 past_len, tq):
    db, ppb = page_table.shape
    blk3 = lambda a: pl.BlockSpec((1,) + a.shape[1:], lambda i, pt, ix: (i, 0, 0))
    grid_spec = pltpu.PrefetchScalarGridSpec(
        num_scalar_prefetch=2,
        grid=(db,),
        in_specs=[
            blk3(q_s), blk3(gates_s),
            pl.BlockSpec((1,) + ocmp.shape[1:], lambda i, pt, ix: (i, 0, 0, 0)),
            blk3(ks_new), blk3(kw_new),
            pl.BlockSpec(memory_space=pl.ANY),
            blk3(win_t),
            pl.BlockSpec(w_out_attn.shape, lambda i, pt, ix: (0, 0)),
        ],
        out_specs=pl.BlockSpec((1, TQ_PAD, D_MODEL), lambda i, pt, ix: (i, 0, 0)),
        scratch_shapes=[
            pltpu.VMEM((N_KV_HEADS, tq, HEAD_DIM, TOP_K * PAGE_SIZE), F32),
            pltpu.VMEM((N_KV_HEADS, tq, HEAD_DIM, TOP_K * PAGE_SIZE), F32),
            pltpu.SemaphoreType.DMA(()),
        ],
    )
    return pl.pallas_call(
        functools.partial(_sample_attn_kernel, past_len=past_len, tq=tq, ppb=ppb),
        grid_spec=grid_spec,
        out_shape=jax.ShapeDtypeStruct((db, TQ_PAD, D_MODEL), F32),
        compiler_params=pltpu.CompilerParams(
            dimension_semantics=("arbitrary",), vmem_limit_bytes=VMEM_LIMIT),
        name="sample_attn",
    )(page_table.reshape(-1), idx, q_s, gates_s, ocmp, ks_new, kw_new, slc_t, win_t, w_out_attn)


def _pages_feature_major(cache):
    return jnp.transpose(cache, (0, 2, 3, 4, 1)).reshape(cache.shape[0], KV_COLS, cache.shape[1])


def kernel(x_prompt, x_sample, p_prompt, p_sample, cache_cmp_kv, cache_slc_kv, cache_win_kv, state_conv, page_table, g_attn, w_in, w_cmp1, w_cmp2, pe_cmp, conv_w, conv_b, ln_conv_g, ln_conv_b, w_out, g_mlp, w_up, w_down, g_ple, w_ple, w_ple_gate, g_final):
    b, t, _ = x_prompt.shape
    db, tq, _ = x_sample.shape
    win_buf = cache_win_kv.shape[2]
    kv5 = lambda a, nb, nt: a.reshape(1, nb, nt, 2, N_KV_HEADS, HEAD_DIM)

    w_out_attn = jnp.pad(w_out[0][:MIX_ATTN].reshape(N_HEADS, HEAD_DIM, D_MODEL),
                         ((0, 0), (0, LANES - HEAD_DIM), (0, 0))).reshape(N_HEADS * LANES, D_MODEL).astype(BF16)
    w_out_conv = w_out[0][MIX_ATTN:].astype(BF16)
    tail_w = (g_mlp[0], w_up[0].astype(BF16), w_down[0].astype(BF16), g_ple[0],
              w_ple[0].astype(BF16), w_ple_gate[0].astype(BF16), g_final)
    g_row = g_attn[0].reshape(1, D_MODEL)

    w_packed, cadd, segs, dtypes, names = _proj_weights(w_in[0], True)
    po = dict(zip(names, _project(x_prompt.reshape(b * t, D_MODEL), g_row, w_packed, cadd, segs, dtypes, 512, t)))
    u_p = po["u"].reshape(b, t, CONV_CH)
    conv_p = _conv_tail(jnp.zeros((b, HALO, CONV_CH), F32), u_p, conv_w[0], conv_b[0],
                        ln_conv_g[0], ln_conv_b[0], 512)
    ppb = t // PAGE_SIZE
    cw = _compress_weights(w_cmp1[0], w_cmp2[0], pe_cmp[0])
    cmp_p = _compress(jnp.arange(b * ppb, dtype=jnp.int32).reshape(b, ppb),
                      po["kc"].reshape(b * ppb, PAGE_SIZE, KV_COLS), *cw, pages=ppb, transposed=False)
    n_chunk = t // CMP_STRIDE
    cover, expand = _selection_constants(n_chunk, n_chunk - 1, t // SLC_BLOCK, t)
    attn_p = _nsa_prompt(po["q_aug"].reshape(b, t, -1), po["gates"].reshape(b, t, LANES), cmp_p,
                         po["ks_aug"].reshape(b, t, -1), po["kw_aug"].reshape(b, t, -1), cover, expand)
    y_prompt = _layer_tail(x_prompt.reshape(b * t, D_MODEL),
                           [attn_p.reshape(b * t, -1), conv_p.reshape(b * t, CONV_CH)],
                           [w_out_attn, w_out_conv], None, p_prompt[0].reshape(b * t, PLE_DIM),
                           *tail_w, tm=512).reshape(b, t, D_MODEL)

    xs = jnp.pad(x_sample, ((0, 0), (0, TQ_PAD - tq), (0, 0))).reshape(db * TQ_PAD, D_MODEL)
    w_packed_s, cadd_s, segs_s, dtypes_s, names_s = _proj_weights(w_in[0], False)
    so = dict(zip(names_s, _project(xs, g_row, w_packed_s, cadd_s, segs_s, dtypes_s, db * TQ_PAD, TQ_PAD)))
    rs = lambda a: a.reshape(db, TQ_PAD, -1)
    u_s = rs(so["u"])
    kc_s, ks_s, kw_s = rs(so["kc"])[:, :tq], rs(so["ks"])[:, :tq], rs(so["kw"])[:, :tq]
    state32 = jnp.pad(state_conv[0], ((0, 0), (HALO - (CONV_W - 1), 0), (0, 0)))
    conv_s = _conv_tail(state32, u_s, conv_w[0], conv_b[0], ln_conv_g[0], ln_conv_b[0], TQ_PAD)

    past_len = page_table.shape[1] * PAGE_SIZE
    assert (past_len + tq) // CMP_STRIDE == past_len // CMP_STRIDE and tq <= TQ_PAD
    n_chunk_s = past_len // CMP_STRIDE
    n_sel_s = past_len // SLC_BLOCK + 1
    cmp_s = _compress(page_table, _pages_feature_major(cache_cmp_kv[0]), *cw, pages=64, transposed=True)
    ci = np.arange(n_chunk_s)[:, None] * CMP_STRIDE
    sj = np.arange(SEL_LANES)[None, :] * SLC_BLOCK
    cover_s = jnp.asarray((ci + CMP_BLOCK > sj) & (ci < sj + SLC_BLOCK)
                          & (np.arange(n_chunk_s)[:, None] < n_chunk_s - 1)
                          & (np.arange(SEL_LANES)[None, :] < n_sel_s), BF16)
    q_s = rs(so["q_aug"])
    ocmp_s, idx_s = _sample_select(q_s, cmp_s, cover_s, past_len, n_sel_s)
    pre_s = _sample_attn(page_table, idx_s[:, :, :TOP_K].reshape(-1), q_s, rs(so["gates"]), ocmp_s,
                         rs(so["ks"]), rs(so["kw"]), _pages_feature_major(cache_slc_kv[0]),
                         _pages_feature_major(cache_win_kv[0]), w_out[0][:MIX_ATTN].astype(BF16),
                         past_len, tq)
    p_s = jnp.pad(p_sample[0], ((0, 0), (0, TQ_PAD - tq), (0, 0))).reshape(db * TQ_PAD, PLE_DIM)
    y_sample = _layer_tail(xs, [conv_s.reshape(db * TQ_PAD, CONV_CH)], [w_out_conv],
                           pre_s.reshape(db * TQ_PAD, D_MODEL), p_s, *tail_w,
                           tm=db * TQ_PAD).reshape(db, TQ_PAD, D_MODEL)[:, :tq]

    kc_p, ks_p, kw_p = (po[n].reshape(b, t, KV_COLS) for n in ("kc", "ks", "kw"))
    new_win = jnp.concatenate([cache_win_kv[:, :, tq:], kv5(kw_s, db, tq)], 2)
    new_conv_s = jnp.concatenate([state_conv[0], u_s[:, :tq]], 1)[:, -(CONV_W - 1):]
    return (y_prompt, y_sample,
            kv5(kc_p, b, t), kv5(ks_p, b, t), kv5(kw_p[:, -win_buf:], b, win_buf),
            u_p[:, -(CONV_W - 1):][None],
            kv5(kc_s, db, tq), kv5(ks_s, db, tq), new_win, new_conv_s[None])
```

```python
import functools

import numpy as np
import jax
import jax.numpy as jnp
from jax import lax
from jax.experimental import pallas as pl
from jax.experimental.pallas import tpu as pltpu

D_MODEL = 1024
N_HEADS = 8
HEAD_DIM = 64
N_KV_HEADS = 2
GROUP = N_HEADS // N_KV_HEADS
MIX_ATTN = N_HEADS * HEAD_DIM
CONV_CH = D_MODEL - MIX_ATTN
CONV_W = 31
CMP_BLOCK = 32
CMP_STRIDE = 16
CMP_HID = 2 * HEAD_DIM
SLC_BLOCK = 64
TOP_K = 16
WINDOW = 512
PAGE_SIZE = 128
D_FF = 4 * D_MODEL
PLE_DIM = 256
Q_COLS = N_HEADS * HEAD_DIM
KV_COLS = 2 * N_KV_HEADS * HEAD_DIM
GATE_COLS = 3 * N_HEADS
EPS = 1e-6
NEG = -1e30
FORCE_BONUS = 1e4

LANES = 128
Q_TILE = 128
KEY_TILE = 256
VMEM_LIMIT = 56 * 1024 * 1024

F32 = jnp.float32
BF16 = jnp.bfloat16


def _dot(a, b):
    return jnp.dot(a, b, preferred_element_type=F32)


def _dot_nt(a, b):
    return lax.dot_general(a, b, (((1,), (1,)), ((), ())), preferred_element_type=F32)


def _sigmoid(x):
    return 1.0 / (1.0 + jnp.exp(-x))


def _rms(x, g):
    return x * lax.rsqrt(jnp.mean(x * x, -1, keepdims=True) + EPS) * g


def _slopes():
    return 2.0 ** (-8.0 * np.arange(1, N_HEADS + 1) / N_HEADS)


def _proj_kernel(x_ref, g_ref, w_ref, cadd_ref, *out_refs, segs, seq_len):
    x = x_ref[...]
    xn = _rms(x, g_ref[...]).astype(BF16)
    tm = x.shape[0]
    row0 = pl.program_id(0) * tm
    outs = iter(out_refs)
    for seg in segs:
        kind, c0, width = seg[0], seg[1], seg[2]
        o_ref = next(outs)
        y = _dot(xn, w_ref[:, c0:c0 + width])
        if kind == "plain":
            pass
        elif kind == "plain_t":
            next(outs)[0] = y.T
        elif kind == "t_only":
            o_ref[0] = y.T
            continue
        elif kind == "const":
            y = y + cadd_ref[:, c0:c0 + width]
        elif kind == "sigmoid":
            y = _sigmoid(y)
        elif kind == "glu":
            c1 = seg[3]
            y = y * _sigmoid(_dot(xn, w_ref[:, c1:c1 + width]))
        elif kind == "kvaug":
            pos = (row0 + lax.broadcasted_iota(jnp.int32, (tm, width), 0)) % seq_len
            lane = lax.broadcasted_iota(jnp.int32, (tm, width), 1)
            is_key = lane < (width // 2)
            l = lane % LANES
            y = y + cadd_ref[:, c0:c0 + width]
            y = y + jnp.where(is_key & (l == HEAD_DIM), (pos // SLC_BLOCK).astype(F32), 0.0)
            y = y + jnp.where(is_key & (l == HEAD_DIM + 1), (pos % SLC_BLOCK).astype(F32), 0.0)
        o_ref[...] = y.astype(o_ref.dtype)


def _project(x2d, g, w_packed, cadd, segs, out_dtypes, tm, seq_len):
    rows = x2d.shape[0]
    nw = w_packed.shape[1]
    tiles = seq_len // tm if seq_len >= tm else 1
    out_shape, out_specs = [], []
    for s, dt in zip(segs, out_dtypes):
        if s[0] != "t_only":
            out_shape.append(jax.ShapeDtypeStruct((rows, s[2]), dt))
            out_specs.append(pl.BlockSpec((tm, s[2]), lambda i: (i, 0)))
        if s[0] in ("plain_t", "t_only"):
            out_shape.append(jax.ShapeDtypeStruct((rows // seq_len, s[2], seq_len), dt))
            out_specs.append(pl.BlockSpec((1, s[2], tm), lambda i: (i // tiles, 0, i % tiles)))
    return pl.pallas_call(
        functools.partial(_proj_kernel, segs=tuple(segs), seq_len=seq_len),
        grid=(rows // tm,),
        in_specs=[
            pl.BlockSpec((tm, D_MODEL), lambda i: (i, 0)),
            pl.BlockSpec((1, D_MODEL), lambda i: (0, 0)),
            pl.BlockSpec((D_MODEL, nw), lambda i: (0, 0), pipeline_mode=pl.Buffered(1)),
            pl.BlockSpec((1, nw), lambda i: (0, 0)),
        ],
        out_specs=out_specs,
        out_shape=out_shape,
        compiler_params=pltpu.CompilerParams(
            dimension_semantics=("arbitrary",), vmem_limit_bytes=VMEM_LIMIT),
        name="proj",
    )(x2d, g, w_packed, cadd)


HALO = 32


def _conv_kernel(st_ref, u_ref, w_ref, cb_ref, lg_ref, lb_ref, o_ref, ext_ref, sh_ref, *, tt):
    @pl.when(pl.program_id(1) == 0)
    def _():
        ext_ref[0:HALO, :] = st_ref[0]

    ext_ref[HALO:HALO + tt, :] = u_ref[0]
    for r in range(1, 8):
        sh_ref[r] = ext_ref[pl.ds(r, tt + HALO - 8), :]
    w = w_ref[...]
    y = jnp.zeros((tt, CONV_CH), F32) + cb_ref[...]
    for r in range(8):
        for q in range(5 if r == 0 else 4):
            k = 8 * q + r - 2
            if 0 <= k < CONV_W:
                src = ext_ref[8 * q:8 * q + tt, :] if r == 0 else sh_ref[r, 8 * q:8 * q + tt, :]
                y = y + w[k:k + 1, :] * src
    mu = jnp.mean(y, -1, keepdims=True)
    yc = y - mu
    yn = yc * lax.rsqrt(jnp.mean(yc * yc, -1, keepdims=True) + EPS) * lg_ref[...] + lb_ref[...]
    o_ref[0] = (yn * _sigmoid(yn)).astype(o_ref.dtype)
    ext_ref[0:HALO, :] = ext_ref[tt:tt + HALO, :]


def _conv_tail(state32, u, conv_w, conv_b, ln_g, ln_b, tt):
    b, t, _ = u.shape
    w32 = jnp.pad(conv_w, ((0, 1), (0, 0)))
    row = lambda a: a.reshape(1, CONV_CH)
    return pl.pallas_call(
        functools.partial(_conv_kernel, tt=tt),
        grid=(b, t // tt),
        in_specs=[
            pl.BlockSpec((1, HALO, CONV_CH), lambda i, j: (i, 0, 0)),
            pl.BlockSpec((1, tt, CONV_CH), lambda i, j: (i, j, 0)),
            pl.BlockSpec((HALO, CONV_CH), lambda i, j: (0, 0)),
            pl.BlockSpec((1, CONV_CH), lambda i, j: (0, 0)),
            pl.BlockSpec((1, CONV_CH), lambda i, j: (0, 0)),
            pl.BlockSpec((1, CONV_CH), lambda i, j: (0, 0)),
        ],
        out_specs=pl.BlockSpec((1, tt, CONV_CH), lambda i, j: (i, j, 0)),
        out_shape=jax.ShapeDtypeStruct((b, t, CONV_CH), BF16),
        scratch_shapes=[pltpu.VMEM((HALO + tt, CONV_CH), F32),
                        pltpu.VMEM((8, HALO + tt - 8, CONV_CH), F32)],
        compiler_params=pltpu.CompilerParams(
            dimension_semantics=("arbitrary", "arbitrary"), vmem_limit_bytes=VMEM_LIMIT),
        name="conv_tail",
    )(state32, u, w32, row(conv_b), row(ln_g), row(ln_b))


def _gelu_tanh(x):
    return 0.5 * x * (1.0 + jnp.tanh(np.sqrt(2.0 / np.pi) * (x + 0.044715 * (x * x * x))))


def _pe_term(pe_ref, w1f_ref, pet):
    for c in range(2):
        t = lax.dot_general(pe_ref[c], w1f_ref[c], (((1,), (0,)), ((), ())),
                            precision=lax.Precision.HIGHEST, preferred_element_type=F32)
        pet[c] = jnp.concatenate([t, t], axis=1)


def _cmp_key_columns(n0, n):
    nidx = n0 + lax.broadcasted_iota(jnp.int32, (n, 2 * LANES), 0)
    cend = nidx * CMP_STRIDE + (CMP_BLOCK - 1)
    lane = lax.broadcasted_iota(jnp.int32, (n, 2 * LANES), 1) % LANES
    return (jnp.where(lane == HEAD_DIM, (cend // SLC_BLOCK).astype(F32), 0.0)
            + jnp.where(lane == HEAD_DIM + 1, (cend % SLC_BLOCK).astype(F32), 0.0))


def _compress_kernel(pt_ref, cache_ref, w1_ref, pe_ref, w1f_ref, w2k_ref, w2v_ref, o_ref,
                     xbuf, pet, sem, tsem, *, pages, n_groups, ppb):
    b = pl.program_id(0)
    gi = pl.program_id(1)
    rows = pages * PAGE_SIZE
    n = rows // CMP_STRIDE

    @pl.when((b == 0) & (gi == 0))
    def _():
        _pe_term(pe_ref, w1f_ref, pet)

    has_next = gi + 1 < n_groups

    def page_copies(p):
        page = pt_ref[b * ppb + gi * pages + p]
        return [pltpu.make_async_copy(cache_ref.at[page, :, pl.ds(c * LANES, LANES)],
                                      xbuf.at[c, pl.ds(p * PAGE_SIZE, PAGE_SIZE)], sem) for c in range(2)]

    def tail_copies():
        page = pt_ref[b * ppb + (gi + 1) * pages]
        return [pltpu.make_async_copy(cache_ref.at[page, pl.ds(0, CMP_STRIDE), pl.ds(c * LANES, LANES)],
                                      xbuf.at[c, pl.ds(rows, CMP_STRIDE)], tsem) for c in range(2)]

    for p in range(pages):
        for cp in page_copies(p):
            cp.start()

    @pl.when(has_next)
    def _():
        for cp in tail_copies():
            cp.start()

    for p in range(pages):
        for cp in page_copies(p):
            cp.wait()

    @pl.when(has_next)
    def _():
        for cp in tail_copies():
            cp.wait()

    @pl.when(jnp.logical_not(has_next))
    def _():
        for c in range(2):
            xbuf[c, rows:rows + CMP_STRIDE, :] = jnp.zeros((CMP_STRIDE, LANES), F32)

    aug = _cmp_key_columns(gi * n, n)
    for c, w2_ref in ((0, w2k_ref), (1, w2v_ref)):
        lhs = jnp.concatenate(
            [xbuf[c, pl.ds(j, n, stride=CMP_STRIDE), :].astype(BF16)
             for j in range(CMP_BLOCK)], axis=1)
        pre = _dot(lhs, w1_ref[c]) + pet[c][0:1, :]
        h = _gelu_tanh(pre).astype(BF16)
        out = _dot(h, w2_ref[...])
        if c == 0:
            o_ref[0, :, 0:2 * LANES] = (out + aug).astype(o_ref.dtype)
        else:
            o_ref[0, :, 2 * LANES:] = out.astype(o_ref.dtype)


def _compress(page_table, cache, w1big, pe8, w1f, w2k, w2v, pages):
    nb, ppb = page_table.shape
    n_groups = ppb // pages
    n = pages * PAGE_SIZE // CMP_STRIDE
    ncols = 2 * LANES + w2v.shape[1]
    once = dict(pipeline_mode=pl.Buffered(1))
    scratch = [
        pltpu.VMEM((2, pages * PAGE_SIZE + CMP_STRIDE, LANES), F32),
        pltpu.VMEM((2, 8, 2 * CMP_HID), F32),
        pltpu.SemaphoreType.DMA(()),
        pltpu.SemaphoreType.DMA(()),
    ]
    grid_spec = pltpu.PrefetchScalarGridSpec(
        num_scalar_prefetch=1,
        grid=(nb, n_groups),
        in_specs=[
            pl.BlockSpec(memory_space=pl.ANY),
            pl.BlockSpec(w1big.shape, lambda i, j, pt: (0, 0, 0), **once),
            pl.BlockSpec(pe8.shape, lambda i, j, pt: (0, 0, 0), **once),
            pl.BlockSpec(w1f.shape, lambda i, j, pt: (0, 0, 0), **once),
            pl.BlockSpec(w2k.shape, lambda i, j, pt: (0, 0), **once),
            pl.BlockSpec(w2v.shape, lambda i, j, pt: (0, 0), **once),
        ],
        out_specs=pl.BlockSpec((1, n, ncols), lambda i, j, pt: (i, j, 0)),
        scratch_shapes=scratch,
    )
    return pl.pallas_call(
        functools.partial(_compress_kernel, pages=pages, n_groups=n_groups, ppb=ppb),
        grid_spec=grid_spec,
        out_shape=jax.ShapeDtypeStruct((nb, n_groups * n, ncols), BF16),
        compiler_params=pltpu.CompilerParams(
            dimension_semantics=("arbitrary", "arbitrary"), vmem_limit_bytes=VMEM_LIMIT),
        name="compress",
    )(page_table.reshape(-1), cache, w1big, pe8, w1f, w2k, w2v)


def _compress_paged_kernel(pt_ref, cache_ref, w1_ref, pe_ref, w1f_ref, w2k_ref, w2v_ref, o_ref,
                           stg, xj, pet, sem, tsem, *, pages, n_groups, ppb, n_steps):
    b = pl.program_id(0)
    gi = pl.program_id(1)
    step = b * n_groups + gi
    slot = step % 2
    chunks_per_page = PAGE_SIZE // CMP_STRIDE
    n = pages * chunks_per_page
    pitch = n + 8
    hid = 2 * CMP_HID

    @pl.when(step == 0)
    def _():
        _pe_term(pe_ref, w1f_ref, pet)

    def page_copies(bb, gg, sl):
        return [pltpu.make_async_copy(cache_ref.at[pt_ref[bb * ppb + gg * pages + p]], stg.at[sl, p], sem.at[sl])
                for p in range(pages)]

    def tail_copy(bb, gg, sl):
        return pltpu.make_async_copy(cache_ref.at[pt_ref[bb * ppb + (gg + 1) * pages]], stg.at[sl, pages],
                                     tsem.at[sl])

    def start_fetch(bb, gg, sl):
        for cp in page_copies(bb, gg, sl):
            cp.start()

        @pl.when(gg + 1 < n_groups)
        def _():
            tail_copy(bb, gg, sl).start()

    @pl.when(step == 0)
    def _():
        start_fetch(b, gi, slot)

    @pl.when(step + 1 < n_steps)
    def _():
        wraps = gi + 1 >= n_groups
        start_fetch(jnp.where(wraps, b + 1, b), jnp.where(wraps, 0, gi + 1), 1 - slot)

    has_next = gi + 1 < n_groups
    for cp in page_copies(b, gi, slot):
        cp.wait()

    @pl.when(has_next)
    def _():
        tail_copy(b, gi, slot).wait()

    for c in range(2):
        for j in range(CMP_STRIDE):
            xj[c, j * pitch + n:(j + 1) * pitch, :] = jnp.zeros((pitch - n, LANES), F32)

    def scatter_tokens(x_t, chunk0, n_tok):
        for c in range(2):
            xt = x_t[c * LANES:(c + 1) * LANES, :].T
            for i in range(n_tok // 8):
                first = (8 * (i % 2)) * pitch + chunk0 + i // 2
                xj[c, pl.ds(first, 8, stride=pitch), :] = xt[8 * i:8 * i + 8, :]

    def untranspose(p, carry):
        scatter_tokens(stg[slot, p], p * chunks_per_page, PAGE_SIZE)
        return carry

    lax.fori_loop(0, pages, untranspose, 0, unroll=2)

    @pl.when(has_next)
    def _():
        scatter_tokens(stg[slot, pages], n, CMP_STRIDE)

    aug = _cmp_key_columns(gi * n, n)
    for c, w2_ref in ((0, w2k_ref), (1, w2v_ref)):
        lhs = jnp.concatenate(
            [xj[c, j * pitch:(j + 1) * pitch, :].astype(BF16) for j in range(CMP_STRIDE)], axis=1)
        res = _dot(lhs, w1_ref[c])
        second_half_next = pltpu.roll(res[:, hid:], pitch - 1, axis=0)
        pre = (res[:, :hid] + second_half_next)[0:n] + pet[c][0:1, :]
        h = _gelu_tanh(pre).astype(BF16)
        out = _dot(h, w2_ref[...])
        if c == 0:
            o_ref[0, :, 0:2 * LANES] = (out + aug).astype(o_ref.dtype)
        else:
            o_ref[0, :, 2 * LANES:] = out.astype(o_ref.dtype)


def _compress_paged(page_table, cache_t, w1cat, pe8, w1f, w2k, w2v, pages):
    nb, ppb = page_table.shape
    n_groups = ppb // pages
    n = pages * PAGE_SIZE // CMP_STRIDE
    ncols = 2 * LANES + w2v.shape[1]
    once = dict(pipeline_mode=pl.Buffered(1))
    grid_spec = pltpu.PrefetchScalarGridSpec(
        num_scalar_prefetch=1,
        grid=(nb, n_groups),
        in_specs=[
            pl.BlockSpec(memory_space=pl.ANY),
            pl.BlockSpec(w1cat.shape, lambda i, j, pt: (0, 0, 0), **once),
            pl.BlockSpec(pe8.shape, lambda i, j, pt: (0, 0, 0), **once),
            pl.BlockSpec(w1f.shape, lambda i, j, pt: (0, 0, 0), **once),
            pl.BlockSpec(w2k.shape, lambda i, j, pt: (0, 0), **once),
            pl.BlockSpec(w2v.shape, lambda i, j, pt: (0, 0), **once),
        ],
        out_specs=pl.BlockSpec((1, n, ncols), lambda i, j, pt: (i, j, 0)),
        scratch_shapes=[
            pltpu.VMEM((2, pages + 1, KV_COLS, PAGE_SIZE), F32),
            pltpu.VMEM((2, CMP_STRIDE * (n + 8), LANES), F32),
            pltpu.VMEM((2, 8, 2 * CMP_HID), F32),
            pltpu.SemaphoreType.DMA((2,)),
            pltpu.SemaphoreType.DMA((2,)),
        ],
    )
    return pl.pallas_call(
        functools.partial(_compress_paged_kernel, pages=pages, n_groups=n_groups, ppb=ppb,
                          n_steps=nb * n_groups),
        grid_spec=grid_spec,
        out_shape=jax.ShapeDtypeStruct((nb, n_groups * n, ncols), BF16),
        compiler_params=pltpu.CompilerParams(
            dimension_semantics=("arbitrary", "arbitrary"), vmem_limit_bytes=VMEM_LIMIT),
        name="compress_paged",
    )(page_table.reshape(-1), cache_t, w1cat, pe8, w1f, w2k, w2v)


def _compress_weights(w_cmp1, w_cmp2, pe_cmp):
    w1 = w_cmp1
    z = jnp.zeros_like(w1)
    w1big = jnp.stack([jnp.concatenate([w1, z], -1), jnp.concatenate([z, w1], -1)], axis=2)
    w1cat = jnp.concatenate([w1big[:, :CMP_STRIDE], w1big[:, CMP_STRIDE:]], axis=-1)
    w1cat = w1cat.reshape(2, CMP_STRIDE * 2 * HEAD_DIM, 4 * CMP_HID).astype(BF16)
    w1big = w1big.reshape(2, CMP_BLOCK * 2 * HEAD_DIM, 2 * CMP_HID).astype(BF16)
    pe8 = jnp.broadcast_to(pe_cmp.reshape(2, 1, CMP_BLOCK * HEAD_DIM), (2, 8, CMP_BLOCK * HEAD_DIM))
    w1f = w1.reshape(2, CMP_BLOCK * HEAD_DIM, CMP_HID)

    def w2_layout(w2):
        out = jnp.zeros((2, CMP_HID, 2, LANES), F32)
        for h in range(2):
            out = out.at[h, :, h, 0:HEAD_DIM].set(w2)
        return out.reshape(2 * CMP_HID, 2 * LANES).astype(BF16)

    return w1big, w1cat, (pe8, w1f, w2_layout(w_cmp2[0]), w2_layout(w_cmp2[1]))


SLC_CHUNK = 1024
WIN_SPAN = 768


def _normalize(acc):
    return acc / jnp.maximum(acc[:, HEAD_DIM:HEAD_DIM + 1], 1e-30)


def _nsa_prompt_kernel(q_ref, gt_ref, cmp_ref, ks_ref, kw_ref, cov_ref, e_ref, o_ref):
    i = pl.program_id(1)
    qs = i * Q_TILE
    q = q_ref[0]
    gates = gt_ref[0]
    qpos_t = qs + lax.broadcasted_iota(jnp.int32, (Q_TILE, 1), 0)
    qpos_gt = jnp.concatenate([qpos_t] * GROUP, axis=0)
    n_cmp_pad = cmp_ref.shape[1]
    cend = lax.broadcasted_iota(jnp.int32, (1, n_cmp_pad), 1) * CMP_STRIDE + (CMP_BLOCK - 1)
    jl = lax.broadcasted_iota(jnp.int32, (1, LANES), 1)
    cur = qpos_t // SLC_BLOCK
    forced = (jl == 0) | (jl == cur) | (jl == cur - 1)
    n_sel = ks_ref.shape[1] // SLC_BLOCK
    jrow = lax.broadcasted_iota(jnp.int32, (n_sel, Q_TILE), 0)

    qks, o_cmps, sels = [], [], []
    for k in range(N_KV_HEADS):
        qk = jnp.concatenate(
            [q[:, (GROUP * k + g) * LANES:(GROUP * k + g + 1) * LANES] for g in range(GROUP)], axis=0)

        kc = cmp_ref[0, :, k * LANES:(k + 1) * LANES]
        vc = cmp_ref[0, :, (2 + k) * LANES:(3 + k) * LANES]
        valid = qpos_gt >= cend
        s = jnp.where(valid, _dot_nt(qk, kc), NEG)
        e = jnp.where(valid, jnp.exp(s - jnp.max(s, -1, keepdims=True)), 0.0)
        p = e / jnp.maximum(jnp.sum(e, -1, keepdims=True), 1e-30)
        o_cmp = _dot(p.astype(BF16), vc)

        psum = p[0:Q_TILE]
        for g in range(1, GROUP):
            psum = psum + p[g * Q_TILE:(g + 1) * Q_TILE]
        p_hi = psum.astype(BF16)
        p_lo = (psum - p_hi.astype(F32)).astype(BF16)
        imp = _dot(p_hi, cov_ref[...]) + _dot(p_lo, cov_ref[...])
        score = (jnp.where(jl * SLC_BLOCK <= qpos_t, imp, -1.0)
                 + jnp.where(forced, FORCE_BONUS, 0.0))
        sc_t = score.T[0:n_sel]
        cnt = jnp.zeros((n_sel, Q_TILE), jnp.int32)
        for jp in range(n_sel):
            row = sc_t[jp:jp + 1, :]
            tie = jnp.where(jrow > jp, 1, 0)
            cnt = cnt + jnp.where(row > sc_t, 1, jnp.where(row == sc_t, tie, 0))
        sel_t = jnp.where(cnt < TOP_K, 1.0, 0.0)
        sel_t = jnp.concatenate([sel_t, jnp.zeros((LANES - n_sel, Q_TILE), F32)], axis=0)
        qks.append(qk)
        o_cmps.append(o_cmp)
        sels.append(sel_t.T.astype(BF16))

    rows = GROUP * Q_TILE
    hi = i // 2 + 1

    ciota = lax.broadcasted_iota(jnp.int32, (1, SLC_CHUNK), 1)

    def slc_step(c, carry):
        off = pl.multiple_of(c * SLC_CHUNK, SLC_CHUNK)
        causal = (c * SLC_CHUNK + ciota) <= qpos_t
        out = []
        for k in range(N_KV_HEADS):
            m, acc = carry[k]
            in_sel = _dot(sels[k], e_ref[c])
            bias = jnp.tile(jnp.where((in_sel > 0.5) & causal, 0.0, NEG), (GROUP, 1))
            s = _dot_nt(qks[k], ks_ref[0, pl.ds(off, SLC_CHUNK), k * LANES:(k + 1) * LANES]) + bias
            m_new = jnp.maximum(m, jnp.max(s, -1, keepdims=True))
            p = jnp.exp(s - m_new).astype(BF16)
            acc = jnp.exp(m - m_new) * acc + _dot(
                p, ks_ref[0, pl.ds(off, SLC_CHUNK), (2 + k) * LANES:(3 + k) * LANES])
            out.append((m_new, acc))
        return tuple(out)

    init = tuple((jnp.full((rows, 1), 0.1 * NEG, F32), jnp.zeros((rows, LANES), F32))
                 for _ in range(N_KV_HEADS))
    chunks = (hi * KEY_TILE + SLC_CHUNK - 1) // SLC_CHUNK
    slc = lax.fori_loop(0, chunks, slc_step, init)

    woff = pl.multiple_of(jnp.maximum(hi * KEY_TILE - WIN_SPAN, 0), KEY_TILE)
    d_w = qpos_t - (woff + lax.broadcasted_iota(jnp.int32, (1, WIN_SPAN), 1))
    bias_w = jnp.tile(jnp.where((d_w >= 0) & (d_w < WINDOW), 0.0, NEG), (GROUP, 1))

    for k in range(N_KV_HEADS):
        s = _dot_nt(qks[k], kw_ref[0, pl.ds(woff, WIN_SPAN), k * LANES:(k + 1) * LANES]) + bias_w
        p = jnp.exp(s - jnp.max(s, -1, keepdims=True)).astype(BF16)
        o_win = _normalize(_dot(p, kw_ref[0, pl.ds(woff, WIN_SPAN), (2 + k) * LANES:(3 + k) * LANES]))
        o_slc = _normalize(slc[k][1])
        o_cmp = o_cmps[k]
        for g in range(GROUP):
            h = GROUP * k + g
            rs = slice(g * Q_TILE, (g + 1) * Q_TILE)
            o = (gates[:, h:h + 1] * o_cmp[rs]
                 + gates[:, N_HEADS + h:N_HEADS + h + 1] * o_slc[rs]
                 + gates[:, 2 * N_HEADS + h:2 * N_HEADS + h + 1] * o_win[rs])
            o_ref[0, :, h * LANES:(h + 1) * LANES] = o.astype(o_ref.dtype)


def _nsa_prompt(q_aug, gates, cmp, ks_aug, kw_aug, cover, expand):
    b, t, _ = q_aug.shape
    return pl.pallas_call(
        _nsa_prompt_kernel,
        grid=(b, t // Q_TILE),
        in_specs=[
            pl.BlockSpec((1, Q_TILE, N_HEADS * LANES), lambda i, j: (i, j, 0)),
            pl.BlockSpec((1, Q_TILE, LANES), lambda i, j: (i, j, 0)),
            pl.BlockSpec((1,) + cmp.shape[1:], lambda i, j: (i, 0, 0)),
            pl.BlockSpec((1,) + ks_aug.shape[1:], lambda i, j: (i, 0, 0)),
            pl.BlockSpec((1,) + kw_aug.shape[1:], lambda i, j: (i, 0, 0)),
            pl.BlockSpec(cover.shape, lambda i, j: (0, 0)),
            pl.BlockSpec(expand.shape, lambda i, j: (0, 0, 0)),
        ],
        out_specs=pl.BlockSpec((1, Q_TILE, N_HEADS * LANES), lambda i, j: (i, j, 0)),
        out_shape=jax.ShapeDtypeStruct((b, t, N_HEADS * LANES), BF16),
        compiler_params=pltpu.CompilerParams(
            dimension_semantics=("arbitrary", "arbitrary"), vmem_limit_bytes=VMEM_LIMIT),
        name="nsa_prompt",
    )(q_aug, gates, cmp, ks_aug, kw_aug, cover, expand)


def _selection_constants(n_cmp_pad, n_cmp, n_sel, seq_len):
    ci = np.arange(n_cmp_pad)[:, None] * CMP_STRIDE
    sj = np.arange(LANES)[None, :] * SLC_BLOCK
    cover = ((ci + CMP_BLOCK > sj) & (ci < sj + SLC_BLOCK)
             & (np.arange(n_cmp_pad)[:, None] < n_cmp) & (np.arange(LANES)[None, :] < n_sel))
    m = np.arange(seq_len).reshape(seq_len // SLC_CHUNK, 1, SLC_CHUNK)
    expand = (m // SLC_BLOCK) == np.arange(LANES).reshape(1, LANES, 1)
    return jnp.asarray(cover, BF16), jnp.asarray(expand, BF16)


FF_CHUNK = 1024


def _tail_kernel(*refs, n_mix, has_pre):
    x_ref = refs[0]
    mix_refs = refs[1:1 + n_mix]
    pos = 1 + n_mix
    pre_ref = refs[pos] if has_pre else None
    pos += int(has_pre)
    p_ref = refs[pos]
    wmix_refs = refs[pos + 1:pos + 1 + n_mix]
    (gmlp_ref, wup_ref, wdown_ref, gple_ref, wple_ref, wpg_ref, gfin_ref,
     y_ref) = refs[pos + 1 + n_mix:]
    x = x_ref[...]
    for m_ref, w_ref in zip(mix_refs, wmix_refs):
        x = x + _dot(m_ref[...], w_ref[...])
    if has_pre:
        x = x + pre_ref[...]
    xn = _rms(x, gmlp_ref[...]).astype(BF16)
    acc = jnp.zeros_like(x)
    for c in range(D_FF // FF_CHUNK):
        h = _dot(xn, wup_ref[:, c * FF_CHUNK:(c + 1) * FF_CHUNK])
        h = jnp.square(jnp.maximum(h, 0.0)).astype(BF16)
        acc = acc + _dot(h, wdown_ref[c * FF_CHUNK:(c + 1) * FF_CHUNK, :])
    x = x + acc
    gate = _sigmoid(_dot(_rms(x, gple_ref[...]).astype(BF16), wpg_ref[...]))
    x = x + _dot(p_ref[...].astype(BF16), wple_ref[...]) * gate
    y_ref[...] = _rms(x, gfin_ref[...])


def _layer_tail(x2d, mixes, wmixes, pre, p2d, g_mlp, w_up, w_down, g_ple, w_ple, w_pg, g_final, tm):
    rows = x2d.shape[0]
    row_spec = lambda a: pl.BlockSpec((tm, a.shape[1]), lambda i: (i, 0))
    const = lambda a: pl.BlockSpec(a.shape, lambda i: (0, 0), pipeline_mode=pl.Buffered(1))
    vec = lambda a: a.reshape(1, -1)
    has_pre = pre is not None
    args = [x2d, *mixes] + ([pre] if has_pre else []) + [p2d]
    specs = [row_spec(a) for a in args]
    consts = [*wmixes, vec(g_mlp), w_up, w_down, vec(g_ple), w_ple, w_pg, vec(g_final)]
    return pl.pallas_call(
        functools.partial(_tail_kernel, n_mix=len(mixes), has_pre=has_pre),
        grid=(rows // tm,),
        in_specs=specs + [const(a) for a in consts],
        out_specs=pl.BlockSpec((tm, D_MODEL), lambda i: (i, 0)),
        out_shape=jax.ShapeDtypeStruct((rows, D_MODEL), F32),
        compiler_params=pltpu.CompilerParams(
            dimension_semantics=("arbitrary",), vmem_limit_bytes=VMEM_LIMIT),
        name="layer_tail",
    )(*args, *consts)


def _pad_heads(w, n_groups):
    d = w.shape[0]
    return jnp.pad(w.reshape(d, n_groups, HEAD_DIM), ((0, 0), (0, 0), (0, LANES - HEAD_DIM))).reshape(
        d, n_groups * LANES)


def _proj_weights(w_in, with_kv_aug):
    c_q, c_kc, c_ks, c_kw, c_gt = np.cumsum([Q_COLS, KV_COLS, KV_COLS, KV_COLS, GATE_COLS])
    w_q = w_in[:, :c_q] * (HEAD_DIM ** -0.5)
    w_kc, w_ks, w_kw = w_in[:, c_q:c_kc], w_in[:, c_kc:c_ks], w_in[:, c_ks:c_kw]
    w_gt = jnp.pad(w_in[:, c_kw:c_gt], ((0, 0), (0, LANES - GATE_COLS)))
    w_a, w_b = w_in[:, c_gt:c_gt + CONV_CH], w_in[:, c_gt + CONV_CH:]
    kinds = ("plain_t", "t_only", "t_only") if with_kv_aug else ("plain", "plain", "plain")
    pieces = [("q_aug", "const", _pad_heads(w_q, N_HEADS), BF16),
              ("kc", kinds[0], w_kc, F32), ("ks", kinds[1], w_ks, F32), ("kw", kinds[2], w_kw, F32),
              ("gates", "sigmoid", w_gt, F32),
              ("u", "glu", w_a, F32), (None, None, w_b, None)]
    if with_kv_aug:
        pieces += [("ks_aug", "kvaug", _pad_heads(w_ks, 4), BF16),
                   ("kw_aug", "kvaug", _pad_heads(w_kw, 4), BF16)]
    cols, segs, dtypes, names = [], [], [], []
    off = 0
    offsets = {}
    for name, kind, w, dt in pieces:
        offsets[name] = off
        if name is not None:
            seg = [kind, off, w.shape[1]]
            if kind == "glu":
                seg.append(off + w.shape[1])
            segs.append(tuple(seg))
            dtypes.append(dt)
            if kind != "t_only":
                names.append(name)
            if kind in ("plain_t", "t_only"):
                names.append(name + "_t")
        cols.append(w)
        off += w.shape[1]
    w_packed = jnp.concatenate(cols, axis=1).astype(BF16)
    cadd = np.zeros((1, off), np.float32)
    sl = _slopes()
    for h in range(N_HEADS):
        cadd[0, offsets["q_aug"] + h * LANES + HEAD_DIM] = SLC_BLOCK * sl[h]
        cadd[0, offsets["q_aug"] + h * LANES + HEAD_DIM + 1] = sl[h]
    if with_kv_aug:
        for nm in ("ks_aug", "kw_aug"):
            for h in range(N_KV_HEADS):
                cadd[0, offsets[nm] + (2 + h) * LANES + HEAD_DIM] = 1.0
    return w_packed, jnp.asarray(cadd), segs, dtypes, names


TQ_PAD = 8
SEL_LANES = 384
BIG = 1e9


def _masked_softmax(s, valid):
    s = jnp.where(valid, s, NEG)
    e = jnp.where(valid, jnp.exp(s - jnp.max(s, -1, keepdims=True)), 0.0)
    return e / jnp.maximum(jnp.sum(e, -1, keepdims=True), 1e-30)


def _rows_gt(x, k, width):
    return jnp.concatenate(
        [x[:, (GROUP * k + g) * LANES:(GROUP * k + g) * LANES + width] for g in range(GROUP)], axis=0)


def _sample_select_kernel(q_ref, cmp_ref, cov_ref, ocmp_ref, idx_ref, *, past_len, n_sel):
    rows = GROUP * TQ_PAD
    t_gt = lax.broadcasted_iota(jnp.int32, (rows, 1), 0) % TQ_PAD
    n_cmp_pad = cmp_ref.shape[1]
    cend = lax.broadcasted_iota(jnp.int32, (1, n_cmp_pad), 1) * CMP_STRIDE + (CMP_BLOCK - 1)
    qpos_t = past_len + lax.broadcasted_iota(jnp.int32, (TQ_PAD, 1), 0)
    jl = lax.broadcasted_iota(jnp.int32, (1, SEL_LANES), 1)
    cur = qpos_t // SLC_BLOCK
    forced = (jl == 0) | (jl == cur) | (jl == cur - 1)
    scores = []
    n_seq = q_ref.shape[0]
    for bb in range(n_seq):
        q = q_ref[bb].astype(F32)
        for k in range(N_KV_HEADS):
            qk = _rows_gt(q, k, LANES).astype(BF16)
            kc = cmp_ref[bb, :, k * LANES:(k + 1) * LANES]
            vc = cmp_ref[bb, :, (2 + k) * LANES:(3 + k) * LANES]
            p = _masked_softmax(_dot_nt(qk, kc), (past_len + t_gt) >= cend)
            ocmp_ref[bb, k] = _dot(p.astype(BF16), vc)
            psum = p[0:TQ_PAD]
            for g in range(1, GROUP):
                psum = psum + p[g * TQ_PAD:(g + 1) * TQ_PAD]
            p_hi = psum.astype(BF16)
            p_lo = (psum - p_hi.astype(F32)).astype(BF16)
            imp = _dot(p_hi, cov_ref[...]) + _dot(p_lo, cov_ref[...])
            score = (jnp.where(jl * SLC_BLOCK <= qpos_t, imp, -1.0)
                     + jnp.where(forced, FORCE_BONUS, 0.0))
            scores.append(jnp.where(jl < n_sel, score, -BIG))
    sc = jnp.concatenate(scores, axis=0)
    lane = lax.broadcasted_iota(jnp.int32, sc.shape, 1).astype(F32)
    out_lane = lax.broadcasted_iota(jnp.int32, (sc.shape[0], LANES), 1)
    picked = jnp.zeros((sc.shape[0], LANES), F32)
    for s in range(TOP_K):
        m = jnp.max(sc, -1, keepdims=True)
        am = jnp.min(jnp.where(sc == m, lane, BIG), -1, keepdims=True)
        picked = jnp.where(out_lane == s, am, picked)
        sc = jnp.where(lane == am, -2.0 * BIG, sc)
    per_seq = N_KV_HEADS * TQ_PAD
    for bb in range(n_seq):
        idx_ref[bb] = picked[bb * per_seq:(bb + 1) * per_seq].astype(jnp.int32)


SELECT_SEQS = 4


def _sample_select(q_s, cmp_s, cover_s, past_len, n_sel):
    db = q_s.shape[0]
    ns = SELECT_SEQS
    return pl.pallas_call(
        functools.partial(_sample_select_kernel, past_len=past_len, n_sel=n_sel),
        grid=(db // ns,),
        in_specs=[
            pl.BlockSpec((ns,) + q_s.shape[1:], lambda i: (i, 0, 0)),
            pl.BlockSpec((ns,) + cmp_s.shape[1:], lambda i: (i, 0, 0)),
            pl.BlockSpec(cover_s.shape, lambda i: (0, 0)),
        ],
        out_specs=[
            pl.BlockSpec((ns, N_KV_HEADS, GROUP * TQ_PAD, LANES), lambda i: (i, 0, 0, 0)),
            pl.BlockSpec((ns, N_KV_HEADS * TQ_PAD, LANES), lambda i: (i, 0, 0)),
        ],
        out_shape=[
            jax.ShapeDtypeStruct((db, N_KV_HEADS, GROUP * TQ_PAD, LANES), F32),
            jax.ShapeDtypeStruct((db, N_KV_HEADS * TQ_PAD, LANES), jnp.int32),
        ],
        compiler_params=pltpu.CompilerParams(
            dimension_semantics=("arbitrary",), vmem_limit_bytes=VMEM_LIMIT),
        name="sample_select",
    )(q_s, cmp_s, cover_s)


def _joint_softmax(s_a, valid_a, s_b, valid_b):
    s_a = jnp.where(valid_a, s_a, NEG)
    s_b = jnp.where(valid_b, s_b, NEG)
    m = jnp.maximum(jnp.max(s_a, -1, keepdims=True), jnp.max(s_b, -1, keepdims=True))
    e_a = jnp.where(valid_a, jnp.exp(s_a - m), 0.0)
    e_b = jnp.where(valid_b, jnp.exp(s_b - m), 0.0)
    inv = 1.0 / jnp.maximum(jnp.sum(e_a, -1, keepdims=True) + jnp.sum(e_b, -1, keepdims=True), 1e-30)
    return e_a * inv, e_b * inv


def _sample_attn_kernel(pt_ref, idx_ref, q_ref, gt_ref, ocmp_ref, ksn_ref, kwn_ref, slc_ref, win_ref,
                        wout_ref, o_ref, kbuf, vbuf, sem, *, past_len, tq, ppb):
    b = pl.program_id(0)
    n_past_blocks = past_len // SLC_BLOCK
    bpp = PAGE_SIZE // SLC_BLOCK
    rows = GROUP * TQ_PAD

    slot = b % 2

    def sel_index(k, t, s, bb=b):
        return idx_ref[((bb * N_KV_HEADS + k) * TQ_PAD + t) * TOP_K + s]

    def block_copies(bb, sl):
        cps = []
        for k in range(N_KV_HEADS):
            for t in range(tq):
                for s in range(TOP_K):
                    blk = jnp.minimum(sel_index(k, t, s, bb), n_past_blocks - 1)
                    page = pt_ref[bb * ppb + blk // bpp]
                    dst = pl.ds(s * PAGE_SIZE, PAGE_SIZE)
                    cps.append(pltpu.make_async_copy(
                        slc_ref.at[page, pl.ds(k * HEAD_DIM, HEAD_DIM), :], kbuf.at[sl, k, t, :, dst],
                        sem.at[sl]))
                    cps.append(pltpu.make_async_copy(
                        slc_ref.at[page, pl.ds((N_KV_HEADS + k) * HEAD_DIM, HEAD_DIM), :],
                        vbuf.at[sl, k, t, :, dst], sem.at[sl]))
        return cps

    @pl.when(b == 0)
    def _():
        for cp in block_copies(b, slot):
            cp.start()

    @pl.when(b + 1 < pl.num_programs(0))
    def _():
        for cp in block_copies(b + 1, 1 - slot):
            cp.start()

    q = q_ref[0].astype(F32)
    gates = gt_ref[0]
    t_gt = lax.broadcasted_iota(jnp.int32, (rows, 1), 0) % TQ_PAD
    g_gt = lax.broadcasted_iota(jnp.int32, (rows, 1), 0) // TQ_PAD
    qpos = past_len + t_gt
    lane_pg = lax.broadcasted_iota(jnp.int32, (1, PAGE_SIZE), 1)
    new_pos = past_len + lax.broadcasted_iota(jnp.int32, (1, TQ_PAD), 1)
    win_len = win_ref.shape[2]
    win_pos = past_len - win_len + lax.broadcasted_iota(jnp.int32, (1, win_len), 1)
    slopes = _slopes()

    for cp in block_copies(b, slot):
        cp.wait()

    acc = jnp.zeros((TQ_PAD, D_MODEL), F32)
    for k in range(N_KV_HEADS):
        qk = _rows_gt(q, k, HEAD_DIM).astype(BF16)
        slope = jnp.zeros((rows, 1), F32)
        for g in range(GROUP):
            slope = jnp.where(g_gt == g, float(slopes[GROUP * k + g]), slope)
        ks_new = ksn_ref[0, :, k * HEAD_DIM:(k + 1) * HEAD_DIM].astype(BF16)
        vs_new = ksn_ref[0, :, (N_KV_HEADS + k) * HEAD_DIM:(N_KV_HEADS + k + 1) * HEAD_DIM].astype(BF16)
        kw_new = kwn_ref[0, :, k * HEAD_DIM:(k + 1) * HEAD_DIM].astype(BF16)
        vw_new = kwn_ref[0, :, (N_KV_HEADS + k) * HEAD_DIM:(N_KV_HEADS + k + 1) * HEAD_DIM].astype(BF16)

        s_new = _dot_nt(qk, ks_new) - slope * (qpos - new_pos).astype(F32)
        o_slc = jnp.zeros((rows, HEAD_DIM), F32)
        for t in range(tq):
            kpos, chosen, n_new = [], [], 0
            for s in range(TOP_K):
                blk = sel_index(k, t, s)
                is_past = blk < n_past_blocks
                page_blk = jnp.minimum(blk, n_past_blocks - 1)
                kpos.append((page_blk // bpp) * PAGE_SIZE + lane_pg)
                half = jnp.where(is_past, page_blk % bpp, -1)
                chosen.append(jnp.where((lane_pg // SLC_BLOCK) == half, 1, 0))
                n_new = n_new + jnp.where(is_past, 0, 1)
            kpos = jnp.concatenate(kpos, axis=1)
            chosen = jnp.concatenate(chosen, axis=1)
            mine = t_gt == t
            valid = mine & (chosen > 0) & (kpos <= qpos)
            valid_new = mine & (new_pos <= qpos) & ((new_pos * 0 + n_new) > 0)
            s_past = _dot(qk, kbuf[slot, k, t].astype(BF16)) - slope * (qpos - kpos).astype(F32)
            p_past, p_new = _joint_softmax(s_past, valid, s_new, valid_new)
            o_slc = (o_slc + _dot_nt(p_past.astype(BF16), vbuf[slot, k, t].astype(BF16))
                     + _dot(p_new.astype(BF16), vs_new))

        d_w = qpos - win_pos
        d_n = qpos - new_pos
        s_w = _dot(qk, win_ref[0, k * HEAD_DIM:(k + 1) * HEAD_DIM, :].astype(BF16)) - slope * d_w.astype(F32)
        s_n = _dot_nt(qk, kw_new) - slope * d_n.astype(F32)
        p_w, p_n = _joint_softmax(s_w, (d_w >= 0) & (d_w < WINDOW) & (win_pos >= 0),
                                  s_n, (d_n >= 0) & (d_n < WINDOW))
        v_w = win_ref[0, (N_KV_HEADS + k) * HEAD_DIM:(N_KV_HEADS + k + 1) * HEAD_DIM, :].astype(BF16)
        o_win = _dot_nt(p_w.astype(BF16), v_w) + _dot(p_n.astype(BF16), vw_new)

        def gate(branch):
            return jnp.concatenate(
                [gates[:, branch * N_HEADS + GROUP * k + g:branch * N_HEADS + GROUP * k + g + 1]
                 for g in range(GROUP)], axis=0)

        o = (gate(0) * ocmp_ref[0, k][:, 0:HEAD_DIM] + gate(1) * o_slc + gate(2) * o_win).astype(BF16)
        for g in range(GROUP):
            h = GROUP * k + g
            acc = acc + _dot(o[g * TQ_PAD:(g + 1) * TQ_PAD], wout_ref[h * HEAD_DIM:(h + 1) * HEAD_DIM, :])
    o_ref[0] = acc


def _sample_attn(page_table, idx, q_s, gates_s, ocmp, ks_new, kw_new, slc_t, win_t, w_out_attn,
                 past_len, tq):
    db, ppb = page_table.shape
    blk3 = lambda a: pl.BlockSpec((1,) + a.shape[1:], lambda i, pt, ix: (i, 0, 0))
    grid_spec = pltpu.PrefetchScalarGridSpec(
        num_scalar_prefetch=2,
        grid=(db,),
        in_specs=[
            blk3(q_s), blk3(gates_s),
            pl.BlockSpec((1,) + ocmp.shape[1:], lambda i, pt, ix: (i, 0, 0, 0)),
            blk3(ks_new), blk3(kw_new),
            pl.BlockSpec(memory_space=pl.ANY),
            blk3(win_t),
            pl.BlockSpec(w_out_attn.shape, lambda i, pt, ix: (0, 0)),
        ],
        out_specs=pl.BlockSpec((1, TQ_PAD, D_MODEL), lambda i, pt, ix: (i, 0, 0)),
        scratch_shapes=[
            pltpu.VMEM((2, N_KV_HEADS, tq, HEAD_DIM, TOP_K * PAGE_SIZE), F32),
            pltpu.VMEM((2, N_KV_HEADS, tq, HEAD_DIM, TOP_K * PAGE_SIZE), F32),
            pltpu.SemaphoreType.DMA((2,)),
        ],
    )
    return pl.pallas_call(
        functools.partial(_sample_attn_kernel, past_len=past_len, tq=tq, ppb=ppb),
        grid_spec=grid_spec,
        out_shape=jax.ShapeDtypeStruct((db, TQ_PAD, D_MODEL), F32),
        compiler_params=pltpu.CompilerParams(
            dimension_semantics=("arbitrary",), vmem_limit_bytes=VMEM_LIMIT),
        name="sample_attn",
    )(page_table.reshape(-1), idx, q_s, gates_s, ocmp, ks_new, kw_new, slc_t, win_t, w_out_attn)


def _pages_feature_major(cache):
    return jnp.transpose(cache, (0, 2, 3, 4, 1)).reshape(cache.shape[0], KV_COLS, cache.shape[1])


def kernel(x_prompt, x_sample, p_prompt, p_sample, cache_cmp_kv, cache_slc_kv, cache_win_kv, state_conv, page_table, g_attn, w_in, w_cmp1, w_cmp2, pe_cmp, conv_w, conv_b, ln_conv_g, ln_conv_b, w_out, g_mlp, w_up, w_down, g_ple, w_ple, w_ple_gate, g_final):
    b, t, _ = x_prompt.shape
    db, tq, _ = x_sample.shape
    win_buf = cache_win_kv.shape[2]
    kv5 = lambda a, nb, nt: a.reshape(1, nb, nt, 2, N_KV_HEADS, HEAD_DIM)

    w_out_attn = jnp.pad(w_out[0][:MIX_ATTN].reshape(N_HEADS, HEAD_DIM, D_MODEL),
                         ((0, 0), (0, LANES - HEAD_DIM), (0, 0))).reshape(N_HEADS * LANES, D_MODEL).astype(BF16)
    w_out_conv = w_out[0][MIX_ATTN:].astype(BF16)
    tail_w = (g_mlp[0], w_up[0].astype(BF16), w_down[0].astype(BF16), g_ple[0],
              w_ple[0].astype(BF16), w_ple_gate[0].astype(BF16), g_final)
    g_row = g_attn[0].reshape(1, D_MODEL)

    w_packed, cadd, segs, dtypes, names = _proj_weights(w_in[0], True)
    po = dict(zip(names, _project(x_prompt.reshape(b * t, D_MODEL), g_row, w_packed, cadd, segs, dtypes, 512, t)))
    u_p = po["u"].reshape(b, t, CONV_CH)
    conv_p = _conv_tail(jnp.zeros((b, HALO, CONV_CH), F32), u_p, conv_w[0], conv_b[0],
                        ln_conv_g[0], ln_conv_b[0], 512)
    ppb = t // PAGE_SIZE
    w1big, w1cat, cw = _compress_weights(w_cmp1[0], w_cmp2[0], pe_cmp[0])
    cmp_p = _compress(jnp.arange(b * ppb, dtype=jnp.int32).reshape(b, ppb),
                      po["kc"].reshape(b * ppb, PAGE_SIZE, KV_COLS), w1big, *cw, pages=ppb)
    n_chunk = t // CMP_STRIDE
    cover, expand = _selection_constants(n_chunk, n_chunk - 1, t // SLC_BLOCK, t)
    attn_p = _nsa_prompt(po["q_aug"].reshape(b, t, -1), po["gates"].reshape(b, t, LANES), cmp_p,
                         po["ks_aug"].reshape(b, t, -1), po["kw_aug"].reshape(b, t, -1), cover, expand)
    y_prompt = _layer_tail(x_prompt.reshape(b * t, D_MODEL),
                           [attn_p.reshape(b * t, -1), conv_p.reshape(b * t, CONV_CH)],
                           [w_out_attn, w_out_conv], None, p_prompt[0].reshape(b * t, PLE_DIM),
                           *tail_w, tm=512).reshape(b, t, D_MODEL)

    xs = jnp.pad(x_sample, ((0, 0), (0, TQ_PAD - tq), (0, 0))).reshape(db * TQ_PAD, D_MODEL)
    w_packed_s, cadd_s, segs_s, dtypes_s, names_s = _proj_weights(w_in[0], False)
    so = dict(zip(names_s, _project(xs, g_row, w_packed_s, cadd_s, segs_s, dtypes_s, db * TQ_PAD, TQ_PAD)))
    rs = lambda a: a.reshape(db, TQ_PAD, -1)
    u_s = rs(so["u"])
    kc_s, ks_s, kw_s = rs(so["kc"])[:, :tq], rs(so["ks"])[:, :tq], rs(so["kw"])[:, :tq]
    state32 = jnp.pad(state_conv[0], ((0, 0), (HALO - (CONV_W - 1), 0), (0, 0)))
    conv_s = _conv_tail(state32, u_s, conv_w[0], conv_b[0], ln_conv_g[0], ln_conv_b[0], TQ_PAD)

    past_len = page_table.shape[1] * PAGE_SIZE
    assert (past_len + tq) // CMP_STRIDE == past_len // CMP_STRIDE and tq <= TQ_PAD
    n_chunk_s = past_len // CMP_STRIDE
    n_sel_s = past_len // SLC_BLOCK + 1
    cmp_s = _compress_paged(page_table, _pages_feature_major(cache_cmp_kv[0]), w1cat, *cw, pages=64)
    ci = np.arange(n_chunk_s)[:, None] * CMP_STRIDE
    sj = np.arange(SEL_LANES)[None, :] * SLC_BLOCK
    cover_s = jnp.asarray((ci + CMP_BLOCK > sj) & (ci < sj + SLC_BLOCK)
                          & (np.arange(n_chunk_s)[:, None] < n_chunk_s - 1)
                          & (np.arange(SEL_LANES)[None, :] < n_sel_s), BF16)
    q_s = rs(so["q_aug"])
    ocmp_s, idx_s = _sample_select(q_s, cmp_s, cover_s, past_len, n_sel_s)
    pre_s = _sample_attn(page_table, idx_s[:, :, :TOP_K].reshape(-1), q_s, rs(so["gates"]), ocmp_s,
                         rs(so["ks"]), rs(so["kw"]), _pages_feature_major(cache_slc_kv[0]),
                         _pages_feature_major(cache_win_kv[0]), w_out[0][:MIX_ATTN].astype(BF16),
                         past_len, tq)
    p_s = jnp.pad(p_sample[0], ((0, 0), (0, TQ_PAD - tq), (0, 0))).reshape(db * TQ_PAD, PLE_DIM)
    y_sample = _layer_tail(xs, [conv_s.reshape(db * TQ_PAD, CONV_CH)], [w_out_conv],
                           pre_s.reshape(db * TQ_PAD, D_MODEL), p_s, *tail_w,
                           tm=db * TQ_PAD).reshape(db, TQ_PAD, D_MODEL)[:, :tq]

    def token_major(a_t):
        nt = a_t.shape[2]
        return jnp.transpose(a_t.reshape(b, 2, N_KV_HEADS, HEAD_DIM, nt), (0, 4, 1, 2, 3))[None]

    new_win = jnp.concatenate([cache_win_kv[:, :, tq:], kv5(kw_s, db, tq)], 2)
    new_conv_s = jnp.concatenate([state_conv[0], u_s[:, :tq]], 1)[:, -(CONV_W - 1):]
    return (y_prompt, y_sample,
            token_major(po["kc_t"]), token_major(po["ks_t"]), token_major(po["kw_t"][:, :, t - win_buf:]),
            u_p[:, -(CONV_W - 1):][None],
            kv5(kc_s, db, tq), kv5(ks_s, db, tq), new_win, new_conv_s[None])
```

```python
import functools

import numpy as np
import jax
import jax.numpy as jnp
from jax import lax
from jax.experimental import pallas as pl
from jax.experimental.pallas import tpu as pltpu

D_MODEL = 1024
N_HEADS = 8
HEAD_DIM = 64
N_KV_HEADS = 2
GROUP = N_HEADS // N_KV_HEADS
MIX_ATTN = N_HEADS * HEAD_DIM
CONV_CH = D_MODEL - MIX_ATTN
CONV_W = 31
CMP_BLOCK = 32
CMP_STRIDE = 16
CMP_HID = 2 * HEAD_DIM
SLC_BLOCK = 64
TOP_K = 16
WINDOW = 512
PAGE_SIZE = 128
D_FF = 4 * D_MODEL
PLE_DIM = 256
Q_COLS = N_HEADS * HEAD_DIM
KV_COLS = 2 * N_KV_HEADS * HEAD_DIM
GATE_COLS = 3 * N_HEADS
EPS = 1e-6
NEG = -1e30
FORCE_BONUS = 1e4

LANES = 128
Q_TILE = 256
KEY_TILE = 256
VMEM_LIMIT = 56 * 1024 * 1024

F32 = jnp.float32
BF16 = jnp.bfloat16


def _dot(a, b):
    return jnp.dot(a, b, preferred_element_type=F32)


def _dot_nt(a, b):
    return lax.dot_general(a, b, (((1,), (1,)), ((), ())), preferred_element_type=F32)


def _sigmoid(x):
    return 1.0 / (1.0 + jnp.exp(-x))


def _rms(x, g):
    return x * lax.rsqrt(jnp.mean(x * x, -1, keepdims=True) + EPS) * g


def _slopes():
    return 2.0 ** (-8.0 * np.arange(1, N_HEADS + 1) / N_HEADS)


def _expand_heads(y, fill):
    low = lax.broadcasted_iota(jnp.int32, (y.shape[0], LANES), 1) < HEAD_DIM
    tiles = []
    for i in range(y.shape[1] // HEAD_DIM):
        pair = y[:, (i // 2) * LANES:(i // 2 + 1) * LANES]
        data = pair if i % 2 == 0 else pltpu.roll(pair, HEAD_DIM, axis=1)
        tiles.append(jnp.where(low, data, fill[i]))
    return tiles


def _store_tiles(o_ref, tiles):
    for i, t in enumerate(tiles):
        o_ref[:, i * LANES:(i + 1) * LANES] = t.astype(o_ref.dtype)


PROJ_OUTPUTS_PROMPT = ("q_aug", "kc", "kc_t", "ks_t", "ks_aug", "kw_t", "kw_aug", "gates", "u")
PROJ_OUTPUTS_SAMPLE = ("q_aug", "kc", "ks", "kw", "gates", "u")


def _proj_kernel(x_ref, g_ref, w_ref, qfill_ref, *out_refs, seq_len, prompt):
    x = x_ref[...]
    xn = _rms(x, g_ref[...]).astype(BF16)
    tm = x.shape[0]
    outs = dict(zip(PROJ_OUTPUTS_PROMPT if prompt else PROJ_OUTPUTS_SAMPLE, out_refs))
    col = [0]

    def matmul(width):
        y = _dot(xn, w_ref[:, col[0]:col[0] + width])
        col[0] += width
        return y

    _store_tiles(outs["q_aug"], _expand_heads(
        matmul(Q_COLS), [qfill_ref[:, h * LANES:(h + 1) * LANES] for h in range(N_HEADS)]))
    if prompt:
        pos = (pl.program_id(0) * tm + lax.broadcasted_iota(jnp.int32, (tm, LANES), 0)) % seq_len
        lane = lax.broadcasted_iota(jnp.int32, (tm, LANES), 1)
        k_fill = (jnp.where(lane == HEAD_DIM, (pos // SLC_BLOCK).astype(F32), 0.0)
                  + jnp.where(lane == HEAD_DIM + 1, (pos % SLC_BLOCK).astype(F32), 0.0))
        v_fill = jnp.where(lane == HEAD_DIM, 1.0, 0.0)
        y = matmul(KV_COLS)
        outs["kc"][...] = y
        outs["kc_t"][0] = y.T
        for name in ("ks", "kw"):
            y = matmul(KV_COLS)
            outs[name + "_t"][0] = y.T
            _store_tiles(outs[name + "_aug"], _expand_heads(y, [k_fill, k_fill, v_fill, v_fill]))
    else:
        for name in ("kc", "ks", "kw"):
            outs[name][...] = matmul(KV_COLS)
    outs["gates"][...] = _sigmoid(matmul(LANES))
    a = matmul(CONV_CH)
    outs["u"][...] = a * _sigmoid(matmul(CONV_CH))


def _project(x2d, g, w_packed, qfill, tm, seq_len, prompt):
    rows = x2d.shape[0]
    nw = w_packed.shape[1]
    tiles = seq_len // tm if prompt else 1
    widths = {"q_aug": (N_HEADS * LANES, BF16), "kc": (KV_COLS, F32), "ks": (KV_COLS, F32),
              "kw": (KV_COLS, F32), "ks_aug": (4 * LANES, BF16), "kw_aug": (4 * LANES, BF16),
              "gates": (LANES, F32), "u": (CONV_CH, F32)}
    names = PROJ_OUTPUTS_PROMPT if prompt else PROJ_OUTPUTS_SAMPLE
    out_shape, out_specs = [], []
    for name in names:
        if name.endswith("_t"):
            out_shape.append(jax.ShapeDtypeStruct((rows // seq_len, KV_COLS, seq_len), F32))
            out_specs.append(pl.BlockSpec((1, KV_COLS, tm), lambda i: (i // tiles, 0, i % tiles)))
        else:
            width, dt = widths[name]
            out_shape.append(jax.ShapeDtypeStruct((rows, width), dt))
            out_specs.append(pl.BlockSpec((tm, width), lambda i: (i, 0)))
    outs = pl.pallas_call(
        functools.partial(_proj_kernel, seq_len=seq_len, prompt=prompt),
        grid=(rows // tm,),
        in_specs=[
            pl.BlockSpec((tm, D_MODEL), lambda i: (i, 0)),
            pl.BlockSpec((1, D_MODEL), lambda i: (0, 0)),
            pl.BlockSpec((D_MODEL, nw), lambda i: (0, 0), pipeline_mode=pl.Buffered(1)),
            pl.BlockSpec(qfill.shape, lambda i: (0, 0)),
        ],
        out_specs=out_specs,
        out_shape=out_shape,
        compiler_params=pltpu.CompilerParams(
            dimension_semantics=("arbitrary",), vmem_limit_bytes=VMEM_LIMIT),
        name="proj",
    )(x2d, g, w_packed, qfill)
    return dict(zip(names, outs))


HALO = 32


def _conv_kernel(st_ref, u_ref, w_ref, cb_ref, lg_ref, lb_ref, o_ref, ext_ref, sh_ref, *, tt):
    @pl.when(pl.program_id(1) == 0)
    def _():
        ext_ref[0:HALO, :] = st_ref[0]

    ext_ref[HALO:HALO + tt, :] = u_ref[0]
    for r in range(1, 8):
        sh_ref[r] = ext_ref[pl.ds(r, tt + HALO - 8), :]
    w = w_ref[...]
    y = jnp.zeros((tt, CONV_CH), F32) + cb_ref[...]
    for r in range(8):
        for q in range(5 if r == 0 else 4):
            k = 8 * q + r - 2
            if 0 <= k < CONV_W:
                src = ext_ref[8 * q:8 * q + tt, :] if r == 0 else sh_ref[r, 8 * q:8 * q + tt, :]
                y = y + w[k:k + 1, :] * src
    mu = jnp.mean(y, -1, keepdims=True)
    yc = y - mu
    yn = yc * lax.rsqrt(jnp.mean(yc * yc, -1, keepdims=True) + EPS) * lg_ref[...] + lb_ref[...]
    o_ref[0] = (yn * _sigmoid(yn)).astype(o_ref.dtype)
    ext_ref[0:HALO, :] = ext_ref[tt:tt + HALO, :]


def _conv_tail(state32, u, conv_w, conv_b, ln_g, ln_b, tt):
    b, t, _ = u.shape
    w32 = jnp.pad(conv_w, ((0, 1), (0, 0)))
    row = lambda a: a.reshape(1, CONV_CH)
    return pl.pallas_call(
        functools.partial(_conv_kernel, tt=tt),
        grid=(b, t // tt),
        in_specs=[
            pl.BlockSpec((1, HALO, CONV_CH), lambda i, j: (i, 0, 0)),
            pl.BlockSpec((1, tt, CONV_CH), lambda i, j: (i, j, 0)),
            pl.BlockSpec((HALO, CONV_CH), lambda i, j: (0, 0)),
            pl.BlockSpec((1, CONV_CH), lambda i, j: (0, 0)),
            pl.BlockSpec((1, CONV_CH), lambda i, j: (0, 0)),
            pl.BlockSpec((1, CONV_CH), lambda i, j: (0, 0)),
        ],
        out_specs=pl.BlockSpec((1, tt, CONV_CH), lambda i, j: (i, j, 0)),
        out_shape=jax.ShapeDtypeStruct((b, t, CONV_CH), BF16),
        scratch_shapes=[pltpu.VMEM((HALO + tt, CONV_CH), F32),
                        pltpu.VMEM((8, HALO + tt - 8, CONV_CH), F32)],
        compiler_params=pltpu.CompilerParams(
            dimension_semantics=("arbitrary", "arbitrary"), vmem_limit_bytes=VMEM_LIMIT),
        name="conv_tail",
    )(state32, u, w32, row(conv_b), row(ln_g), row(ln_b))


def _gelu_tanh(x):
    return 0.5 * x * (1.0 + jnp.tanh(np.sqrt(2.0 / np.pi) * (x + 0.044715 * (x * x * x))))


def _pe_term(pe_ref, w1f_ref, pet):
    for c in range(2):
        t = lax.dot_general(pe_ref[c], w1f_ref[c], (((1,), (0,)), ((), ())),
                            precision=lax.Precision.HIGHEST, preferred_element_type=F32)
        pet[c] = jnp.concatenate([t, t], axis=1)


def _cmp_key_columns(n0, n):
    nidx = n0 + lax.broadcasted_iota(jnp.int32, (n, 2 * LANES), 0)
    cend = nidx * CMP_STRIDE + (CMP_BLOCK - 1)
    lane = lax.broadcasted_iota(jnp.int32, (n, 2 * LANES), 1) % LANES
    return (jnp.where(lane == HEAD_DIM, (cend // SLC_BLOCK).astype(F32), 0.0)
            + jnp.where(lane == HEAD_DIM + 1, (cend % SLC_BLOCK).astype(F32), 0.0))


def _compress_kernel(pt_ref, cache_ref, w1_ref, pe_ref, w1f_ref, w2k_ref, w2v_ref, o_ref,
                     xbuf, pet, sem, tsem, *, pages, n_groups, ppb):
    b = pl.program_id(0)
    gi = pl.program_id(1)
    rows = pages * PAGE_SIZE
    n = rows // CMP_STRIDE

    @pl.when((b == 0) & (gi == 0))
    def _():
        _pe_term(pe_ref, w1f_ref, pet)

    has_next = gi + 1 < n_groups

    def page_copies(p):
        page = pt_ref[b * ppb + gi * pages + p]
        return [pltpu.make_async_copy(cache_ref.at[page, :, pl.ds(c * LANES, LANES)],
                                      xbuf.at[c, pl.ds(p * PAGE_SIZE, PAGE_SIZE)], sem) for c in range(2)]

    def tail_copies():
        page = pt_ref[b * ppb + (gi + 1) * pages]
        return [pltpu.make_async_copy(cache_ref.at[page, pl.ds(0, CMP_STRIDE), pl.ds(c * LANES, LANES)],
                                      xbuf.at[c, pl.ds(rows, CMP_STRIDE)], tsem) for c in range(2)]

    for p in range(pages):
        for cp in page_copies(p):
            cp.start()

    @pl.when(has_next)
    def _():
        for cp in tail_copies():
            cp.start()

    for p in range(pages):
        for cp in page_copies(p):
            cp.wait()

    @pl.when(has_next)
    def _():
        for cp in tail_copies():
            cp.wait()

    @pl.when(jnp.logical_not(has_next))
    def _():
        for c in range(2):
            xbuf[c, rows:rows + CMP_STRIDE, :] = jnp.zeros((CMP_STRIDE, LANES), F32)

    aug = _cmp_key_columns(gi * n, n)
    for c, w2_ref in ((0, w2k_ref), (1, w2v_ref)):
        lhs = jnp.concatenate(
            [xbuf[c, pl.ds(j, n, stride=CMP_STRIDE), :].astype(BF16)
             for j in range(CMP_BLOCK)], axis=1)
        pre = _dot(lhs, w1_ref[c]) + pet[c][0:1, :]
        h = _gelu_tanh(pre).astype(BF16)
        out = _dot(h, w2_ref[...])
        if c == 0:
            o_ref[0, :, 0:2 * LANES] = (out + aug).astype(o_ref.dtype)
        else:
            o_ref[0, :, 2 * LANES:] = out.astype(o_ref.dtype)


def _compress(page_table, cache, w1big, pe8, w1f, w2k, w2v, pages):
    nb, ppb = page_table.shape
    n_groups = ppb // pages
    n = pages * PAGE_SIZE // CMP_STRIDE
    ncols = 2 * LANES + w2v.shape[1]
    once = dict(pipeline_mode=pl.Buffered(1))
    scratch = [
        pltpu.VMEM((2, pages * PAGE_SIZE + CMP_STRIDE, LANES), F32),
        pltpu.VMEM((2, 8, 2 * CMP_HID), F32),
        pltpu.SemaphoreType.DMA(()),
        pltpu.SemaphoreType.DMA(()),
    ]
    grid_spec = pltpu.PrefetchScalarGridSpec(
        num_scalar_prefetch=1,
        grid=(nb, n_groups),
        in_specs=[
            pl.BlockSpec(memory_space=pl.ANY),
            pl.BlockSpec(w1big.shape, lambda i, j, pt: (0, 0, 0), **once),
            pl.BlockSpec(pe8.shape, lambda i, j, pt: (0, 0, 0), **once),
            pl.BlockSpec(w1f.shape, lambda i, j, pt: (0, 0, 0), **once),
            pl.BlockSpec(w2k.shape, lambda i, j, pt: (0, 0), **once),
            pl.BlockSpec(w2v.shape, lambda i, j, pt: (0, 0), **once),
        ],
        out_specs=pl.BlockSpec((1, n, ncols), lambda i, j, pt: (i, j, 0)),
        scratch_shapes=scratch,
    )
    return pl.pallas_call(
        functools.partial(_compress_kernel, pages=pages, n_groups=n_groups, ppb=ppb),
        grid_spec=grid_spec,
        out_shape=jax.ShapeDtypeStruct((nb, n_groups * n, ncols), BF16),
        compiler_params=pltpu.CompilerParams(
            dimension_semantics=("arbitrary", "arbitrary"), vmem_limit_bytes=VMEM_LIMIT),
        name="compress",
    )(page_table.reshape(-1), cache, w1big, pe8, w1f, w2k, w2v)


def _compress_paged_kernel(pt_ref, cache_ref, w1_ref, pe_ref, w1f_ref, w2k_ref, w2v_ref, o_ref,
                           stg, xj, pet, sem, tsem, *, pages, n_groups, ppb, n_steps):
    b = pl.program_id(0)
    gi = pl.program_id(1)
    step = b * n_groups + gi
    slot = step % 2
    chunks_per_page = PAGE_SIZE // CMP_STRIDE
    n = pages * chunks_per_page
    pitch = n + 8
    hid = 2 * CMP_HID

    @pl.when(step == 0)
    def _():
        _pe_term(pe_ref, w1f_ref, pet)

    def page_copies(bb, gg, sl):
        return [pltpu.make_async_copy(cache_ref.at[pt_ref[bb * ppb + gg * pages + p]], stg.at[sl, p], sem.at[sl])
                for p in range(pages)]

    def tail_copy(bb, gg, sl):
        return pltpu.make_async_copy(cache_ref.at[pt_ref[bb * ppb + (gg + 1) * pages]], stg.at[sl, pages],
                                     tsem.at[sl])

    def start_fetch(bb, gg, sl):
        for cp in page_copies(bb, gg, sl):
            cp.start()

        @pl.when(gg + 1 < n_groups)
        def _():
            tail_copy(bb, gg, sl).start()

    @pl.when(step == 0)
    def _():
        start_fetch(b, gi, slot)

    @pl.when(step + 1 < n_steps)
    def _():
        wraps = gi + 1 >= n_groups
        start_fetch(jnp.where(wraps, b + 1, b), jnp.where(wraps, 0, gi + 1), 1 - slot)

    has_next = gi + 1 < n_groups
    for cp in page_copies(b, gi, slot):
        cp.wait()

    @pl.when(has_next)
    def _():
        tail_copy(b, gi, slot).wait()

    for c in range(2):
        for j in range(CMP_STRIDE):
            xj[c, j * pitch + n:(j + 1) * pitch, :] = jnp.zeros((pitch - n, LANES), F32)

    def scatter_tokens(x_t, chunk0, n_tok):
        for c in range(2):
            xt = x_t[c * LANES:(c + 1) * LANES, :].T
            for i in range(n_tok // 8):
                first = (8 * (i % 2)) * pitch + chunk0 + i // 2
                xj[c, pl.ds(first, 8, stride=pitch), :] = xt[8 * i:8 * i + 8, :]

    def untranspose(p, carry):
        scatter_tokens(stg[slot, p], p * chunks_per_page, PAGE_SIZE)
        return carry

    lax.fori_loop(0, pages, untranspose, 0, unroll=8)

    @pl.when(has_next)
    def _():
        scatter_tokens(stg[slot, pages], n, CMP_STRIDE)

    aug = _cmp_key_columns(gi * n, n)
    for c, w2_ref in ((0, w2k_ref), (1, w2v_ref)):
        lhs = jnp.concatenate(
            [xj[c, j * pitch:(j + 1) * pitch, :].astype(BF16) for j in range(CMP_STRIDE)], axis=1)
        res = _dot(lhs, w1_ref[c])
        second_half_next = pltpu.roll(res[:, hid:], pitch - 1, axis=0)
        pre = (res[:, :hid] + second_half_next)[0:n] + pet[c][0:1, :]
        h = _gelu_tanh(pre).astype(BF16)
        out = _dot(h, w2_ref[...])
        if c == 0:
            o_ref[0, :, 0:2 * LANES] = (out + aug).astype(o_ref.dtype)
        else:
            o_ref[0, :, 2 * LANES:] = out.astype(o_ref.dtype)


def _compress_paged(page_table, cache_t, w1cat, pe8, w1f, w2k, w2v, pages):
    nb, ppb = page_table.shape
    n_groups = ppb // pages
    n = pages * PAGE_SIZE // CMP_STRIDE
    ncols = 2 * LANES + w2v.shape[1]
    once = dict(pipeline_mode=pl.Buffered(1))
    grid_spec = pltpu.PrefetchScalarGridSpec(
        num_scalar_prefetch=1,
        grid=(nb, n_groups),
        in_specs=[
            pl.BlockSpec(memory_space=pl.ANY),
            pl.BlockSpec(w1cat.shape, lambda i, j, pt: (0, 0, 0), **once),
            pl.BlockSpec(pe8.shape, lambda i, j, pt: (0, 0, 0), **once),
            pl.BlockSpec(w1f.shape, lambda i, j, pt: (0, 0, 0), **once),
            pl.BlockSpec(w2k.shape, lambda i, j, pt: (0, 0), **once),
            pl.BlockSpec(w2v.shape, lambda i, j, pt: (0, 0), **once),
        ],
        out_specs=pl.BlockSpec((1, n, ncols), lambda i, j, pt: (i, j, 0)),
        scratch_shapes=[
            pltpu.VMEM((2, pages + 1, KV_COLS, PAGE_SIZE), F32),
            pltpu.VMEM((2, CMP_STRIDE * (n + 8), LANES), F32),
            pltpu.VMEM((2, 8, 2 * CMP_HID), F32),
            pltpu.SemaphoreType.DMA((2,)),
            pltpu.SemaphoreType.DMA((2,)),
        ],
    )
    return pl.pallas_call(
        functools.partial(_compress_paged_kernel, pages=pages, n_groups=n_groups, ppb=ppb,
                          n_steps=nb * n_groups),
        grid_spec=grid_spec,
        out_shape=jax.ShapeDtypeStruct((nb, n_groups * n, ncols), BF16),
        compiler_params=pltpu.CompilerParams(
            dimension_semantics=("arbitrary", "arbitrary"), vmem_limit_bytes=VMEM_LIMIT),
        name="compress_paged",
    )(page_table.reshape(-1), cache_t, w1cat, pe8, w1f, w2k, w2v)


def _compress_weights(w_cmp1, w_cmp2, pe_cmp):
    w1 = w_cmp1
    z = jnp.zeros_like(w1)
    w1big = jnp.stack([jnp.concatenate([w1, z], -1), jnp.concatenate([z, w1], -1)], axis=2)
    w1cat = jnp.concatenate([w1big[:, :CMP_STRIDE], w1big[:, CMP_STRIDE:]], axis=-1)
    w1cat = w1cat.reshape(2, CMP_STRIDE * 2 * HEAD_DIM, 4 * CMP_HID).astype(BF16)
    w1big = w1big.reshape(2, CMP_BLOCK * 2 * HEAD_DIM, 2 * CMP_HID).astype(BF16)
    pe8 = jnp.broadcast_to(pe_cmp.reshape(2, 1, CMP_BLOCK * HEAD_DIM), (2, 8, CMP_BLOCK * HEAD_DIM))
    w1f = w1.reshape(2, CMP_BLOCK * HEAD_DIM, CMP_HID)

    def w2_layout(w2):
        out = jnp.zeros((2, CMP_HID, 2, LANES), F32)
        for h in range(2):
            out = out.at[h, :, h, 0:HEAD_DIM].set(w2)
        return out.reshape(2 * CMP_HID, 2 * LANES).astype(BF16)

    return w1big, w1cat, (pe8, w1f, w2_layout(w_cmp2[0]), w2_layout(w_cmp2[1]))


SLC_CHUNK = 1024
WIN_SPAN = WINDOW + Q_TILE


def _nsa_prompt_kernel(q_ref, gt_ref, cmp_ref, ks_ref, kw_ref, cov_ref, e_ref, o_ref):
    i = pl.program_id(1)
    qs = i * Q_TILE
    q = q_ref[0]
    gates = gt_ref[0]
    qpos_t = qs + lax.broadcasted_iota(jnp.int32, (Q_TILE, 1), 0)
    qpos_gt = jnp.concatenate([qpos_t] * GROUP, axis=0)
    n_cmp_pad = cmp_ref.shape[1]
    cend = lax.broadcasted_iota(jnp.int32, (1, n_cmp_pad), 1) * CMP_STRIDE + (CMP_BLOCK - 1)
    jl = lax.broadcasted_iota(jnp.int32, (1, LANES), 1)
    cur = qpos_t // SLC_BLOCK
    forced = (jl == 0) | (jl == cur) | (jl == cur - 1)
    n_sel = ks_ref.shape[1] // SLC_BLOCK
    jrow = lax.broadcasted_iota(jnp.int32, (n_sel, Q_TILE), 0)

    qks, o_cmps, sels = [], [], []
    for k in range(N_KV_HEADS):
        qk = jnp.concatenate(
            [q[:, (GROUP * k + g) * LANES:(GROUP * k + g + 1) * LANES] for g in range(GROUP)], axis=0)

        kc = cmp_ref[0, :, k * LANES:(k + 1) * LANES]
        vc = cmp_ref[0, :, (2 + k) * LANES:(3 + k) * LANES]
        valid = qpos_gt >= cend
        s = jnp.where(valid, _dot_nt(qk, kc), NEG)
        e = jnp.where(valid, jnp.exp(s - jnp.max(s, -1, keepdims=True)), 0.0)
        p = e / jnp.maximum(jnp.sum(e, -1, keepdims=True), 1e-30)
        o_cmp = _dot(p.astype(BF16), vc)

        psum = p[0:Q_TILE]
        for g in range(1, GROUP):
            psum = psum + p[g * Q_TILE:(g + 1) * Q_TILE]
        p_hi = psum.astype(BF16)
        p_lo = (psum - p_hi.astype(F32)).astype(BF16)
        imp = _dot(p_hi, cov_ref[...]) + _dot(p_lo, cov_ref[...])
        score = (jnp.where(jl * SLC_BLOCK <= qpos_t, imp, -1.0)
                 + jnp.where(forced, FORCE_BONUS, 0.0))
        sc_t = score.T[0:n_sel]
        cnt = jnp.zeros((n_sel, Q_TILE), jnp.int32)
        for jp in range(n_sel):
            row = sc_t[jp:jp + 1, :]
            tie = jnp.where(jrow > jp, 1, 0)
            cnt = cnt + jnp.where(row > sc_t, 1, jnp.where(row == sc_t, tie, 0))
        sel_t = jnp.where(cnt < TOP_K, 1.0, 0.0)
        sel_t = jnp.concatenate([sel_t, jnp.zeros((LANES - n_sel, Q_TILE), F32)], axis=0)
        qks.append(qk)
        o_cmps.append(o_cmp)
        sels.append(sel_t.T.astype(BF16))

    rows = GROUP * Q_TILE
    kend = qs + Q_TILE

    ciota = lax.broadcasted_iota(jnp.int32, (1, SLC_CHUNK), 1)

    def slc_step(c, carry):
        off = pl.multiple_of(c * SLC_CHUNK, SLC_CHUNK)
        causal = (c * SLC_CHUNK + ciota) <= qpos_t
        out = []
        for k in range(N_KV_HEADS):
            m, acc = carry[k]
            in_sel = _dot(sels[k], e_ref[c])
            bias = jnp.tile(jnp.where((in_sel > 0.5) & causal, 0.0, NEG), (GROUP, 1))
            s = _dot_nt(qks[k], ks_ref[0, pl.ds(off, SLC_CHUNK), k * LANES:(k + 1) * LANES]) + bias
            m_new = jnp.maximum(m, jnp.max(s, -1, keepdims=True))
            p = jnp.exp(s - m_new).astype(BF16)
            acc = jnp.exp(m - m_new) * acc + _dot(
                p, ks_ref[0, pl.ds(off, SLC_CHUNK), (2 + k) * LANES:(3 + k) * LANES])
            out.append((m_new, acc))
        return tuple(out)

    init = tuple((jnp.full((rows, 1), 0.1 * NEG, F32), jnp.zeros((rows, LANES), F32))
                 for _ in range(N_KV_HEADS))
    slc = lax.fori_loop(0, (kend + SLC_CHUNK - 1) // SLC_CHUNK, slc_step, init)

    woff = pl.multiple_of(jnp.maximum(kend - WIN_SPAN, 0), Q_TILE)
    d_w = qpos_t - (woff + lax.broadcasted_iota(jnp.int32, (1, WIN_SPAN), 1))
    bias_w = jnp.tile(jnp.where((d_w >= 0) & (d_w < WINDOW), 0.0, NEG), (GROUP, 1))

    for k in range(N_KV_HEADS):
        s = _dot_nt(qks[k], kw_ref[0, pl.ds(woff, WIN_SPAN), k * LANES:(k + 1) * LANES]) + bias_w
        p = jnp.exp(s - jnp.max(s, -1, keepdims=True)).astype(BF16)
        a_win = _dot(p, kw_ref[0, pl.ds(woff, WIN_SPAN), (2 + k) * LANES:(3 + k) * LANES])
        a_slc = slc[k][1]
        o_cmp = o_cmps[k]
        heads = []
        for g in range(GROUP):
            h = GROUP * k + g
            rs = slice(g * Q_TILE, (g + 1) * Q_TILE)
            g_slc = gates[:, N_HEADS + h:N_HEADS + h + 1] / jnp.maximum(
                a_slc[rs][:, HEAD_DIM:HEAD_DIM + 1], 1e-30)
            g_win = gates[:, 2 * N_HEADS + h:2 * N_HEADS + h + 1] / jnp.maximum(
                a_win[rs][:, HEAD_DIM:HEAD_DIM + 1], 1e-30)
            heads.append(gates[:, h:h + 1] * o_cmp[rs] + g_slc * a_slc[rs] + g_win * a_win[rs])
        low = lax.broadcasted_iota(jnp.int32, (Q_TILE, LANES), 1) < HEAD_DIM
        for pair in range(GROUP // 2):
            both = jnp.where(low, heads[2 * pair], pltpu.roll(heads[2 * pair + 1], HEAD_DIM, axis=1))
            col = (GROUP // 2 * k + pair) * LANES
            o_ref[0, :, col:col + LANES] = both.astype(o_ref.dtype)


def _nsa_prompt(q_aug, gates, cmp, ks_aug, kw_aug, cover, expand):
    b, t, _ = q_aug.shape
    return pl.pallas_call(
        _nsa_prompt_kernel,
        grid=(b, t // Q_TILE),
        in_specs=[
            pl.BlockSpec((1, Q_TILE, N_HEADS * LANES), lambda i, j: (i, j, 0)),
            pl.BlockSpec((1, Q_TILE, LANES), lambda i, j: (i, j, 0)),
            pl.BlockSpec((1,) + cmp.shape[1:], lambda i, j: (i, 0, 0)),
            pl.BlockSpec((1,) + ks_aug.shape[1:], lambda i, j: (i, 0, 0)),
            pl.BlockSpec((1,) + kw_aug.shape[1:], lambda i, j: (i, 0, 0)),
            pl.BlockSpec(cover.shape, lambda i, j: (0, 0)),
            pl.BlockSpec(expand.shape, lambda i, j: (0, 0, 0)),
        ],
        out_specs=pl.BlockSpec((1, Q_TILE, MIX_ATTN), lambda i, j: (i, j, 0)),
        out_shape=jax.ShapeDtypeStruct((b, t, MIX_ATTN), BF16),
        compiler_params=pltpu.CompilerParams(
            dimension_semantics=("arbitrary", "arbitrary"), vmem_limit_bytes=VMEM_LIMIT),
        name="nsa_prompt",
    )(q_aug, gates, cmp, ks_aug, kw_aug, cover, expand)


def _selection_constants(n_cmp_pad, n_cmp, n_sel, seq_len):
    ci = np.arange(n_cmp_pad)[:, None] * CMP_STRIDE
    sj = np.arange(LANES)[None, :] * SLC_BLOCK
    cover = ((ci + CMP_BLOCK > sj) & (ci < sj + SLC_BLOCK)
             & (np.arange(n_cmp_pad)[:, None] < n_cmp) & (np.arange(LANES)[None, :] < n_sel))
    m = np.arange(seq_len).reshape(seq_len // SLC_CHUNK, 1, SLC_CHUNK)
    expand = (m // SLC_BLOCK) == np.arange(LANES).reshape(1, LANES, 1)
    return jnp.asarray(cover, BF16), jnp.asarray(expand, BF16)


FF_CHUNK = 1024


def _tail_kernel(*refs, n_mix, has_pre):
    x_ref = refs[0]
    mix_refs = refs[1:1 + n_mix]
    pos = 1 + n_mix
    pre_ref = refs[pos] if has_pre else None
    pos += int(has_pre)
    p_ref = refs[pos]
    wmix_refs = refs[pos + 1:pos + 1 + n_mix]
    (gmlp_ref, wup_ref, wdown_ref, gple_ref, wple_ref, wpg_ref, gfin_ref,
     y_ref) = refs[pos + 1 + n_mix:]
    x = x_ref[...]
    for m_ref, w_ref in zip(mix_refs, wmix_refs):
        x = x + _dot(m_ref[...], w_ref[...])
    if has_pre:
        x = x + pre_ref[...]
    xn = _rms(x, gmlp_ref[...]).astype(BF16)
    acc = jnp.zeros_like(x)
    for c in range(D_FF // FF_CHUNK):
        h = _dot(xn, wup_ref[:, c * FF_CHUNK:(c + 1) * FF_CHUNK])
        h = jnp.square(jnp.maximum(h, 0.0)).astype(BF16)
        acc = acc + _dot(h, wdown_ref[c * FF_CHUNK:(c + 1) * FF_CHUNK, :])
    x = x + acc
    gate = _sigmoid(_dot(_rms(x, gple_ref[...]).astype(BF16), wpg_ref[...]))
    x = x + _dot(p_ref[...].astype(BF16), wple_ref[...]) * gate
    y_ref[...] = _rms(x, gfin_ref[...])


def _layer_tail(x2d, mixes, wmixes, pre, p2d, g_mlp, w_up, w_down, g_ple, w_ple, w_pg, g_final, tm):
    rows = x2d.shape[0]
    row_spec = lambda a: pl.BlockSpec((tm, a.shape[1]), lambda i: (i, 0))
    const = lambda a: pl.BlockSpec(a.shape, lambda i: (0, 0), pipeline_mode=pl.Buffered(1))
    vec = lambda a: a.reshape(1, -1)
    has_pre = pre is not None
    args = [x2d, *mixes] + ([pre] if has_pre else []) + [p2d]
    specs = [row_spec(a) for a in args]
    consts = [*wmixes, vec(g_mlp), w_up, w_down, vec(g_ple), w_ple, w_pg, vec(g_final)]
    return pl.pallas_call(
        functools.partial(_tail_kernel, n_mix=len(mixes), has_pre=has_pre),
        grid=(rows // tm,),
        in_specs=specs + [const(a) for a in consts],
        out_specs=pl.BlockSpec((tm, D_MODEL), lambda i: (i, 0)),
        out_shape=jax.ShapeDtypeStruct((rows, D_MODEL), F32),
        compiler_params=pltpu.CompilerParams(
            dimension_semantics=("arbitrary",), vmem_limit_bytes=VMEM_LIMIT),
        name="layer_tail",
    )(*args, *consts)


def _proj_weights(w_in):
    c_q = Q_COLS
    c_gt = Q_COLS + 3 * KV_COLS
    w_packed = jnp.concatenate([
        w_in[:, :c_q] * (HEAD_DIM ** -0.5), w_in[:, c_q:c_gt],
        jnp.pad(w_in[:, c_gt:c_gt + GATE_COLS], ((0, 0), (0, LANES - GATE_COLS))),
        w_in[:, c_gt + GATE_COLS:]], axis=1).astype(BF16)
    qfill = np.zeros((1, N_HEADS * LANES), np.float32)
    for h, slope in enumerate(_slopes()):
        qfill[0, h * LANES + HEAD_DIM] = SLC_BLOCK * slope
        qfill[0, h * LANES + HEAD_DIM + 1] = slope
    return w_packed, jnp.asarray(qfill)


TQ_PAD = 8
SEL_LANES = 384
BIG = 1e9


def _masked_softmax(s, valid):
    s = jnp.where(valid, s, NEG)
    e = jnp.where(valid, jnp.exp(s - jnp.max(s, -1, keepdims=True)), 0.0)
    return e / jnp.maximum(jnp.sum(e, -1, keepdims=True), 1e-30)


def _rows_gt(x, k, width):
    return jnp.concatenate(
        [x[:, (GROUP * k + g) * LANES:(GROUP * k + g) * LANES + width] for g in range(GROUP)], axis=0)


def _sample_select_kernel(q_ref, cmp_ref, cov_ref, ocmp_ref, idx_ref, *, past_len, n_sel):
    rows = GROUP * TQ_PAD
    t_gt = lax.broadcasted_iota(jnp.int32, (rows, 1), 0) % TQ_PAD
    n_cmp_pad = cmp_ref.shape[1]
    cend = lax.broadcasted_iota(jnp.int32, (1, n_cmp_pad), 1) * CMP_STRIDE + (CMP_BLOCK - 1)
    qpos_t = past_len + lax.broadcasted_iota(jnp.int32, (TQ_PAD, 1), 0)
    jl = lax.broadcasted_iota(jnp.int32, (1, SEL_LANES), 1)
    cur = qpos_t // SLC_BLOCK
    forced = (jl == 0) | (jl == cur) | (jl == cur - 1)
    scores = []
    n_seq = q_ref.shape[0]
    for bb in range(n_seq):
        q = q_ref[bb].astype(F32)
        for k in range(N_KV_HEADS):
            qk = _rows_gt(q, k, LANES).astype(BF16)
            kc = cmp_ref[bb, :, k * LANES:(k + 1) * LANES]
            vc = cmp_ref[bb, :, (2 + k) * LANES:(3 + k) * LANES]
            p = _masked_softmax(_dot_nt(qk, kc), (past_len + t_gt) >= cend)
            ocmp_ref[bb, k] = _dot(p.astype(BF16), vc)
            psum = p[0:TQ_PAD]
            for g in range(1, GROUP):
                psum = psum + p[g * TQ_PAD:(g + 1) * TQ_PAD]
            p_hi = psum.astype(BF16)
            p_lo = (psum - p_hi.astype(F32)).astype(BF16)
            imp = _dot(p_hi, cov_ref[...]) + _dot(p_lo, cov_ref[...])
            score = (jnp.where(jl * SLC_BLOCK <= qpos_t, imp, -1.0)
                     + jnp.where(forced, FORCE_BONUS, 0.0))
            scores.append(jnp.where(jl < n_sel, score, -BIG))
    sc = jnp.concatenate(scores, axis=0)
    lane = lax.broadcasted_iota(jnp.int32, sc.shape, 1).astype(F32)
    out_lane = lax.broadcasted_iota(jnp.int32, (sc.shape[0], LANES), 1)
    picked = jnp.zeros((sc.shape[0], LANES), F32)
    for s in range(TOP_K):
        m = jnp.max(sc, -1, keepdims=True)
        am = jnp.min(jnp.where(sc == m, lane, BIG), -1, keepdims=True)
        picked = jnp.where(out_lane == s, am, picked)
        sc = jnp.where(lane == am, -2.0 * BIG, sc)
    per_seq = N_KV_HEADS * TQ_PAD
    for bb in range(n_seq):
        idx_ref[bb] = picked[bb * per_seq:(bb + 1) * per_seq].astype(jnp.int32)


SELECT_SEQS = 4


def _sample_select(q_s, cmp_s, cover_s, past_len, n_sel):
    db = q_s.shape[0]
    ns = SELECT_SEQS
    return pl.pallas_call(
        functools.partial(_sample_select_kernel, past_len=past_len, n_sel=n_sel),
        grid=(db // ns,),
        in_specs=[
            pl.BlockSpec((ns,) + q_s.shape[1:], lambda i: (i, 0, 0)),
            pl.BlockSpec((ns,) + cmp_s.shape[1:], lambda i: (i, 0, 0)),
            pl.BlockSpec(cover_s.shape, lambda i: (0, 0)),
        ],
        out_specs=[
            pl.BlockSpec((ns, N_KV_HEADS, GROUP * TQ_PAD, LANES), lambda i: (i, 0, 0, 0)),
            pl.BlockSpec((ns, N_KV_HEADS * TQ_PAD, LANES), lambda i: (i, 0, 0)),
        ],
        out_shape=[
            jax.ShapeDtypeStruct((db, N_KV_HEADS, GROUP * TQ_PAD, LANES), F32),
            jax.ShapeDtypeStruct((db, N_KV_HEADS * TQ_PAD, LANES), jnp.int32),
        ],
        compiler_params=pltpu.CompilerParams(
            dimension_semantics=("arbitrary",), vmem_limit_bytes=VMEM_LIMIT),
        name="sample_select",
    )(q_s, cmp_s, cover_s)


def _joint_softmax(s_a, valid_a, s_b, valid_b):
    s_a = jnp.where(valid_a, s_a, NEG)
    s_b = jnp.where(valid_b, s_b, NEG)
    m = jnp.maximum(jnp.max(s_a, -1, keepdims=True), jnp.max(s_b, -1, keepdims=True))
    e_a = jnp.where(valid_a, jnp.exp(s_a - m), 0.0)
    e_b = jnp.where(valid_b, jnp.exp(s_b - m), 0.0)
    inv = 1.0 / jnp.maximum(jnp.sum(e_a, -1, keepdims=True) + jnp.sum(e_b, -1, keepdims=True), 1e-30)
    return e_a * inv, e_b * inv


def _sample_attn_kernel(pt_ref, idx_ref, q_ref, gt_ref, ocmp_ref, ksn_ref, kwn_ref, slc_ref, win_ref,
                        wout_ref, o_ref, kbuf, vbuf, sem, *, past_len, tq, ppb):
    b = pl.program_id(0)
    n_past_blocks = past_len // SLC_BLOCK
    bpp = PAGE_SIZE // SLC_BLOCK
    rows = GROUP * TQ_PAD

    slot = b % 2

    def sel_index(k, t, s, bb=b):
        return idx_ref[((bb * N_KV_HEADS + k) * TQ_PAD + t) * TOP_K + s]

    def block_copies(bb, sl):
        cps = []
        for k in range(N_KV_HEADS):
            for t in range(tq):
                for s in range(TOP_K):
                    blk = jnp.minimum(sel_index(k, t, s, bb), n_past_blocks - 1)
                    page = pt_ref[bb * ppb + blk // bpp]
                    dst = pl.ds(s * PAGE_SIZE, PAGE_SIZE)
                    cps.append(pltpu.make_async_copy(
                        slc_ref.at[page, pl.ds(k * HEAD_DIM, HEAD_DIM), :], kbuf.at[sl, k, t, :, dst],
                        sem.at[sl]))
                    cps.append(pltpu.make_async_copy(
                        slc_ref.at[page, pl.ds((N_KV_HEADS + k) * HEAD_DIM, HEAD_DIM), :],
                        vbuf.at[sl, k, t, :, dst], sem.at[sl]))
        return cps

    @pl.when(b == 0)
    def _():
        for cp in block_copies(b, slot):
            cp.start()

    @pl.when(b + 1 < pl.num_programs(0))
    def _():
        for cp in block_copies(b + 1, 1 - slot):
            cp.start()

    q = q_ref[0].astype(F32)
    gates = gt_ref[0]
    t_gt = lax.broadcasted_iota(jnp.int32, (rows, 1), 0) % TQ_PAD
    g_gt = lax.broadcasted_iota(jnp.int32, (rows, 1), 0) // TQ_PAD
    qpos = past_len + t_gt
    lane_pg = lax.broadcasted_iota(jnp.int32, (1, PAGE_SIZE), 1)
    new_pos = past_len + lax.broadcasted_iota(jnp.int32, (1, TQ_PAD), 1)
    win_len = win_ref.shape[2]
    win_pos = past_len - win_len + lax.broadcasted_iota(jnp.int32, (1, win_len), 1)
    slopes = _slopes()

    for cp in block_copies(b, slot):
        cp.wait()

    acc = jnp.zeros((TQ_PAD, D_MODEL), F32)
    for k in range(N_KV_HEADS):
        qk = _rows_gt(q, k, HEAD_DIM).astype(BF16)
        slope = jnp.zeros((rows, 1), F32)
        for g in range(GROUP):
            slope = jnp.where(g_gt == g, float(slopes[GROUP * k + g]), slope)
        ks_new = ksn_ref[0, :, k * HEAD_DIM:(k + 1) * HEAD_DIM].astype(BF16)
        vs_new = ksn_ref[0, :, (N_KV_HEADS + k) * HEAD_DIM:(N_KV_HEADS + k + 1) * HEAD_DIM].astype(BF16)
        kw_new = kwn_ref[0, :, k * HEAD_DIM:(k + 1) * HEAD_DIM].astype(BF16)
        vw_new = kwn_ref[0, :, (N_KV_HEADS + k) * HEAD_DIM:(N_KV_HEADS + k + 1) * HEAD_DIM].astype(BF16)

        s_new = _dot_nt(qk, ks_new) - slope * (qpos - new_pos).astype(F32)
        o_slc = jnp.zeros((rows, HEAD_DIM), F32)
        for t in range(tq):
            kpos, chosen, n_new = [], [], 0
            for s in range(TOP_K):
                blk = sel_index(k, t, s)
                is_past = blk < n_past_blocks
                page_blk = jnp.minimum(blk, n_past_blocks - 1)
                kpos.append((page_blk // bpp) * PAGE_SIZE + lane_pg)
                half = jnp.where(is_past, page_blk % bpp, -1)
                chosen.append(jnp.where((lane_pg // SLC_BLOCK) == half, 1, 0))
                n_new = n_new + jnp.where(is_past, 0, 1)
            kpos = jnp.concatenate(kpos, axis=1)
            chosen = jnp.concatenate(chosen, axis=1)
            mine = t_gt == t
            valid = mine & (chosen > 0) & (kpos <= qpos)
            valid_new = mine & (new_pos <= qpos) & ((new_pos * 0 + n_new) > 0)
            s_past = _dot(qk, kbuf[slot, k, t].astype(BF16)) - slope * (qpos - kpos).astype(F32)
            p_past, p_new = _joint_softmax(s_past, valid, s_new, valid_new)
            o_slc = (o_slc + _dot_nt(p_past.astype(BF16), vbuf[slot, k, t].astype(BF16))
                     + _dot(p_new.astype(BF16), vs_new))

        d_w = qpos - win_pos
        d_n = qpos - new_pos
        s_w = _dot(qk, win_ref[0, k * HEAD_DIM:(k + 1) * HEAD_DIM, :].astype(BF16)) - slope * d_w.astype(F32)
        s_n = _dot_nt(qk, kw_new) - slope * d_n.astype(F32)
        p_w, p_n = _joint_softmax(s_w, (d_w >= 0) & (d_w < WINDOW) & (win_pos >= 0),
                                  s_n, (d_n >= 0) & (d_n < WINDOW))
        v_w = win_ref[0, (N_KV_HEADS + k) * HEAD_DIM:(N_KV_HEADS + k + 1) * HEAD_DIM, :].astype(BF16)
        o_win = _dot_nt(p_w.astype(BF16), v_w) + _dot(p_n.astype(BF16), vw_new)

        def gate(branch):
            return jnp.concatenate(
                [gates[:, branch * N_HEADS + GROUP * k + g:branch * N_HEADS + GROUP * k + g + 1]
                 for g in range(GROUP)], axis=0)

        o = (gate(0) * ocmp_ref[0, k][:, 0:HEAD_DIM] + gate(1) * o_slc + gate(2) * o_win).astype(BF16)
        for g in range(GROUP):
            h = GROUP * k + g
            acc = acc + _dot(o[g * TQ_PAD:(g + 1) * TQ_PAD], wout_ref[h * HEAD_DIM:(h + 1) * HEAD_DIM, :])
    o_ref[0] = acc


def _sample_attn(page_table, idx, q_s, gates_s, ocmp, ks_new, kw_new, slc_t, win_t, w_out_attn,
                 past_len, tq):
    db, ppb = page_table.shape
    blk3 = lambda a: pl.BlockSpec((1,) + a.shape[1:], lambda i, pt, ix: (i, 0, 0))
    grid_spec = pltpu.PrefetchScalarGridSpec(
        num_scalar_prefetch=2,
        grid=(db,),
        in_specs=[
            blk3(q_s), blk3(gates_s),
            pl.BlockSpec((1,) + ocmp.shape[1:], lambda i, pt, ix: (i, 0, 0, 0)),
            blk3(ks_new), blk3(kw_new),
            pl.BlockSpec(memory_space=pl.ANY),
            blk3(win_t),
            pl.BlockSpec(w_out_attn.shape, lambda i, pt, ix: (0, 0)),
        ],
        out_specs=pl.BlockSpec((1, TQ_PAD, D_MODEL), lambda i, pt, ix: (i, 0, 0)),
        scratch_shapes=[
            pltpu.VMEM((2, N_KV_HEADS, tq, HEAD_DIM, TOP_K * PAGE_SIZE), F32),
            pltpu.VMEM((2, N_KV_HEADS, tq, HEAD_DIM, TOP_K * PAGE_SIZE), F32),
            pltpu.SemaphoreType.DMA((2,)),
        ],
    )
    return pl.pallas_call(
        functools.partial(_sample_attn_kernel, past_len=past_len, tq=tq, ppb=ppb),
        grid_spec=grid_spec,
        out_shape=jax.ShapeDtypeStruct((db, TQ_PAD, D_MODEL), F32),
        compiler_params=pltpu.CompilerParams(
            dimension_semantics=("arbitrary",), vmem_limit_bytes=VMEM_LIMIT),
        name="sample_attn",
    )(page_table.reshape(-1), idx, q_s, gates_s, ocmp, ks_new, kw_new, slc_t, win_t, w_out_attn)


def _pages_feature_major(cache):
    return jnp.transpose(cache, (0, 2, 3, 4, 1)).reshape(cache.shape[0], KV_COLS, cache.shape[1])


def kernel(x_prompt, x_sample, p_prompt, p_sample, cache_cmp_kv, cache_slc_kv, cache_win_kv, state_conv, page_table, g_attn, w_in, w_cmp1, w_cmp2, pe_cmp, conv_w, conv_b, ln_conv_g, ln_conv_b, w_out, g_mlp, w_up, w_down, g_ple, w_ple, w_ple_gate, g_final):
    b, t, _ = x_prompt.shape
    db, tq, _ = x_sample.shape
    win_buf = cache_win_kv.shape[2]
    kv5 = lambda a, nb, nt: a.reshape(1, nb, nt, 2, N_KV_HEADS, HEAD_DIM)

    w_out_attn = w_out[0][:MIX_ATTN].astype(BF16)
    w_out_conv = w_out[0][MIX_ATTN:].astype(BF16)
    tail_w = (g_mlp[0], w_up[0].astype(BF16), w_down[0].astype(BF16), g_ple[0],
              w_ple[0].astype(BF16), w_ple_gate[0].astype(BF16), g_final)
    g_row = g_attn[0].reshape(1, D_MODEL)

    w_packed, qfill = _proj_weights(w_in[0])
    po = _project(x_prompt.reshape(b * t, D_MODEL), g_row, w_packed, qfill, 512, t, prompt=True)
    u_p = po["u"].reshape(b, t, CONV_CH)
    conv_p = _conv_tail(jnp.zeros((b, HALO, CONV_CH), F32), u_p, conv_w[0], conv_b[0],
                        ln_conv_g[0], ln_conv_b[0], 512)
    ppb = t // PAGE_SIZE
    w1big, w1cat, cw = _compress_weights(w_cmp1[0], w_cmp2[0], pe_cmp[0])
    cmp_p = _compress(jnp.arange(b * ppb, dtype=jnp.int32).reshape(b, ppb),
                      po["kc"].reshape(b * ppb, PAGE_SIZE, KV_COLS), w1big, *cw, pages=ppb)
    n_chunk = t // CMP_STRIDE
    cover, expand = _selection_constants(n_chunk, n_chunk - 1, t // SLC_BLOCK, t)
    attn_p = _nsa_prompt(po["q_aug"].reshape(b, t, -1), po["gates"].reshape(b, t, LANES), cmp_p,
                         po["ks_aug"].reshape(b, t, -1), po["kw_aug"].reshape(b, t, -1), cover, expand)
    y_prompt = _layer_tail(x_prompt.reshape(b * t, D_MODEL),
                           [attn_p.reshape(b * t, -1), conv_p.reshape(b * t, CONV_CH)],
                           [w_out_attn, w_out_conv], None, p_prompt[0].reshape(b * t, PLE_DIM),
                           *tail_w, tm=512).reshape(b, t, D_MODEL)

    xs = jnp.pad(x_sample, ((0, 0), (0, TQ_PAD - tq), (0, 0))).reshape(db * TQ_PAD, D_MODEL)
    so = _project(xs, g_row, w_packed, qfill, db * TQ_PAD, TQ_PAD, prompt=False)
    rs = lambda a: a.reshape(db, TQ_PAD, -1)
    u_s = rs(so["u"])
    kc_s, ks_s, kw_s = rs(so["kc"])[:, :tq], rs(so["ks"])[:, :tq], rs(so["kw"])[:, :tq]
    state32 = jnp.pad(state_conv[0], ((0, 0), (HALO - (CONV_W - 1), 0), (0, 0)))
    conv_s = _conv_tail(state32, u_s, conv_w[0], conv_b[0], ln_conv_g[0], ln_conv_b[0], TQ_PAD)

    past_len = page_table.shape[1] * PAGE_SIZE
    assert (past_len + tq) // CMP_STRIDE == past_len // CMP_STRIDE and tq <= TQ_PAD
    n_chunk_s = past_len // CMP_STRIDE
    n_sel_s = past_len // SLC_BLOCK + 1
    cmp_s = _compress_paged(page_table, _pages_feature_major(cache_cmp_kv[0]), w1cat, *cw, pages=64)
    ci = np.arange(n_chunk_s)[:, None] * CMP_STRIDE
    sj = np.arange(SEL_LANES)[None, :] * SLC_BLOCK
    cover_s = jnp.asarray((ci + CMP_BLOCK > sj) & (ci < sj + SLC_BLOCK)
                          & (np.arange(n_chunk_s)[:, None] < n_chunk_s - 1)
                          & (np.arange(SEL_LANES)[None, :] < n_sel_s), BF16)
    q_s = rs(so["q_aug"])
    ocmp_s, idx_s = _sample_select(q_s, cmp_s, cover_s, past_len, n_sel_s)
    pre_s = _sample_attn(page_table, idx_s[:, :, :TOP_K].reshape(-1), q_s, rs(so["gates"]), ocmp_s,
                         rs(so["ks"]), rs(so["kw"]), _pages_feature_major(cache_slc_kv[0]),
                         _pages_feature_major(cache_win_kv[0]), w_out_attn,
                         past_len, tq)
    p_s = jnp.pad(p_sample[0], ((0, 0), (0, TQ_PAD - tq), (0, 0))).reshape(db * TQ_PAD, PLE_DIM)
    y_sample = _layer_tail(xs, [conv_s.reshape(db * TQ_PAD, CONV_CH)], [w_out_conv],
                           pre_s.reshape(db * TQ_PAD, D_MODEL), p_s, *tail_w,
                           tm=db * TQ_PAD).reshape(db, TQ_PAD, D_MODEL)[:, :tq]

    def token_major(a_t):
        nt = a_t.shape[2]
        return jnp.transpose(a_t.reshape(b, 2, N_KV_HEADS, HEAD_DIM, nt), (0, 4, 1, 2, 3))[None]

    new_win = jnp.concatenate([cache_win_kv[:, :, tq:], kv5(kw_s, db, tq)], 2)
    new_conv_s = jnp.concatenate([state_conv[0], u_s[:, :tq]], 1)[:, -(CONV_W - 1):]
    return (y_prompt, y_sample,
            token_major(po["kc_t"]), token_major(po["ks_t"]), token_major(po["kw_t"][:, :, t - win_buf:]),
            u_p[:, -(CONV_W - 1):][None],
            kv5(kc_s, db, tq), kv5(ks_s, db, tq), new_win, new_conv_s[None])
```

```python
import functools

import numpy as np
import jax
import jax.numpy as jnp
from jax import lax
from jax.experimental import pallas as pl
from jax.experimental.pallas import tpu as pltpu

D_MODEL = 1024
N_HEADS = 8
HEAD_DIM = 64
N_KV_HEADS = 2
GROUP = N_HEADS // N_KV_HEADS
MIX_ATTN = N_HEADS * HEAD_DIM
CONV_CH = D_MODEL - MIX_ATTN
CONV_W = 31
CMP_BLOCK = 32
CMP_STRIDE = 16
CMP_HID = 2 * HEAD_DIM
SLC_BLOCK = 64
TOP_K = 16
WINDOW = 512
PAGE_SIZE = 128
D_FF = 4 * D_MODEL
PLE_DIM = 256
Q_COLS = N_HEADS * HEAD_DIM
KV_COLS = 2 * N_KV_HEADS * HEAD_DIM
GATE_COLS = 3 * N_HEADS
EPS = 1e-6
NEG = -1e30
FORCE_BONUS = 1e4

LANES = 128
Q_TILE = 256
KEY_TILE = 256
VMEM_LIMIT = 56 * 1024 * 1024

F32 = jnp.float32
BF16 = jnp.bfloat16


def _dot(a, b):
    return jnp.dot(a, b, preferred_element_type=F32)


def _dot_nt(a, b):
    return lax.dot_general(a, b, (((1,), (1,)), ((), ())), preferred_element_type=F32)


def _sigmoid(x):
    return 1.0 / (1.0 + jnp.exp(-x))


def _rms(x, g):
    return x * lax.rsqrt(jnp.mean(x * x, -1, keepdims=True) + EPS) * g


def _slopes():
    return 2.0 ** (-8.0 * np.arange(1, N_HEADS + 1) / N_HEADS)


def _expand_heads(y, fill):
    low = lax.broadcasted_iota(jnp.int32, (y.shape[0], LANES), 1) < HEAD_DIM
    tiles = []
    for i in range(y.shape[1] // HEAD_DIM):
        pair = y[:, (i // 2) * LANES:(i // 2 + 1) * LANES]
        data = pair if i % 2 == 0 else pltpu.roll(pair, HEAD_DIM, axis=1)
        tiles.append(jnp.where(low, data, fill[i]))
    return tiles


def _store_tiles(o_ref, tiles):
    for i, t in enumerate(tiles):
        o_ref[:, i * LANES:(i + 1) * LANES] = t.astype(o_ref.dtype)


PROJ_OUTPUTS_PROMPT = ("q_aug", "kc", "kc_t", "ks_t", "ks_aug", "kw_t", "kw_aug", "gates", "u")
PROJ_OUTPUTS_SAMPLE = ("q_aug", "kc", "ks", "kw", "gates", "u")


def _proj_kernel(x_ref, g_ref, w_ref, qfill_ref, *out_refs, seq_len, prompt):
    x = x_ref[...]
    xn = _rms(x, g_ref[...]).astype(BF16)
    tm = x.shape[0]
    outs = dict(zip(PROJ_OUTPUTS_PROMPT if prompt else PROJ_OUTPUTS_SAMPLE, out_refs))
    col = [0]

    def matmul(width):
        y = _dot(xn, w_ref[:, col[0]:col[0] + width])
        col[0] += width
        return y

    _store_tiles(outs["q_aug"], _expand_heads(
        matmul(Q_COLS), [qfill_ref[:, h * LANES:(h + 1) * LANES] for h in range(N_HEADS)]))
    if prompt:
        pos = (pl.program_id(0) * tm + lax.broadcasted_iota(jnp.int32, (tm, LANES), 0)) % seq_len
        lane = lax.broadcasted_iota(jnp.int32, (tm, LANES), 1)
        k_fill = (jnp.where(lane == HEAD_DIM, (pos // SLC_BLOCK).astype(F32), 0.0)
                  + jnp.where(lane == HEAD_DIM + 1, (pos % SLC_BLOCK).astype(F32), 0.0))
        v_fill = jnp.where(lane == HEAD_DIM, 1.0, 0.0)
        y = matmul(KV_COLS)
        outs["kc"][...] = y
        outs["kc_t"][0] = y.T
        for name in ("ks", "kw"):
            y = matmul(KV_COLS)
            outs[name + "_t"][0] = y.T
            _store_tiles(outs[name + "_aug"], _expand_heads(y, [k_fill, k_fill, v_fill, v_fill]))
    else:
        for name in ("kc", "ks", "kw"):
            outs[name][...] = matmul(KV_COLS)
    outs["gates"][...] = _sigmoid(matmul(LANES))
    a = matmul(CONV_CH)
    outs["u"][...] = a * _sigmoid(matmul(CONV_CH))


def _project(x2d, g, w_packed, qfill, tm, seq_len, prompt):
    rows = x2d.shape[0]
    nw = w_packed.shape[1]
    tiles = seq_len // tm if prompt else 1
    widths = {"q_aug": (N_HEADS * LANES, BF16), "kc": (KV_COLS, F32), "ks": (KV_COLS, F32),
              "kw": (KV_COLS, F32), "ks_aug": (4 * LANES, BF16), "kw_aug": (4 * LANES, BF16),
              "gates": (LANES, F32), "u": (CONV_CH, F32)}
    names = PROJ_OUTPUTS_PROMPT if prompt else PROJ_OUTPUTS_SAMPLE
    out_shape, out_specs = [], []
    for name in names:
        if name.endswith("_t"):
            out_shape.append(jax.ShapeDtypeStruct((rows // seq_len, KV_COLS, seq_len), F32))
            out_specs.append(pl.BlockSpec((1, KV_COLS, tm), lambda i: (i // tiles, 0, i % tiles)))
        else:
            width, dt = widths[name]
            out_shape.append(jax.ShapeDtypeStruct((rows, width), dt))
            out_specs.append(pl.BlockSpec((tm, width), lambda i: (i, 0)))
    outs = pl.pallas_call(
        functools.partial(_proj_kernel, seq_len=seq_len, prompt=prompt),
        grid=(rows // tm,),
        in_specs=[
            pl.BlockSpec((tm, D_MODEL), lambda i: (i, 0)),
            pl.BlockSpec((1, D_MODEL), lambda i: (0, 0)),
            pl.BlockSpec((D_MODEL, nw), lambda i: (0, 0), pipeline_mode=pl.Buffered(1)),
            pl.BlockSpec(qfill.shape, lambda i: (0, 0)),
        ],
        out_specs=out_specs,
        out_shape=out_shape,
        compiler_params=pltpu.CompilerParams(
            dimension_semantics=("arbitrary",), vmem_limit_bytes=VMEM_LIMIT),
        name="proj",
    )(x2d, g, w_packed, qfill)
    return dict(zip(names, outs))


HALO = 32


def _conv_kernel(st_ref, u_ref, w_ref, cb_ref, lg_ref, lb_ref, o_ref, ext_ref, sh_ref, *, tt):
    @pl.when(pl.program_id(1) == 0)
    def _():
        ext_ref[0:HALO, :] = st_ref[0]

    ext_ref[HALO:HALO + tt, :] = u_ref[0]
    for r in range(1, 8):
        sh_ref[r] = ext_ref[pl.ds(r, tt + HALO - 8), :]
    w = w_ref[...]
    y = jnp.zeros((tt, CONV_CH), F32) + cb_ref[...]
    for r in range(8):
        for q in range(5 if r == 0 else 4):
            k = 8 * q + r - 2
            if 0 <= k < CONV_W:
                src = ext_ref[8 * q:8 * q + tt, :] if r == 0 else sh_ref[r, 8 * q:8 * q + tt, :]
                y = y + w[k:k + 1, :] * src
    mu = jnp.mean(y, -1, keepdims=True)
    yc = y - mu
    yn = yc * lax.rsqrt(jnp.mean(yc * yc, -1, keepdims=True) + EPS) * lg_ref[...] + lb_ref[...]
    o_ref[0] = (yn * _sigmoid(yn)).astype(o_ref.dtype)
    ext_ref[0:HALO, :] = ext_ref[tt:tt + HALO, :]


def _conv_tail(state32, u, conv_w, conv_b, ln_g, ln_b, tt):
    b, t, _ = u.shape
    w32 = jnp.pad(conv_w, ((0, 1), (0, 0)))
    row = lambda a: a.reshape(1, CONV_CH)
    return pl.pallas_call(
        functools.partial(_conv_kernel, tt=tt),
        grid=(b, t // tt),
        in_specs=[
            pl.BlockSpec((1, HALO, CONV_CH), lambda i, j: (i, 0, 0)),
            pl.BlockSpec((1, tt, CONV_CH), lambda i, j: (i, j, 0)),
            pl.BlockSpec((HALO, CONV_CH), lambda i, j: (0, 0)),
            pl.BlockSpec((1, CONV_CH), lambda i, j: (0, 0)),
            pl.BlockSpec((1, CONV_CH), lambda i, j: (0, 0)),
            pl.BlockSpec((1, CONV_CH), lambda i, j: (0, 0)),
        ],
        out_specs=pl.BlockSpec((1, tt, CONV_CH), lambda i, j: (i, j, 0)),
        out_shape=jax.ShapeDtypeStruct((b, t, CONV_CH), BF16),
        scratch_shapes=[pltpu.VMEM((HALO + tt, CONV_CH), F32),
                        pltpu.VMEM((8, HALO + tt - 8, CONV_CH), F32)],
        compiler_params=pltpu.CompilerParams(
            dimension_semantics=("arbitrary", "arbitrary"), vmem_limit_bytes=VMEM_LIMIT),
        name="conv_tail",
    )(state32, u, w32, row(conv_b), row(ln_g), row(ln_b))


def _gelu_tanh(x):
    return 0.5 * x * (1.0 + jnp.tanh(np.sqrt(2.0 / np.pi) * (x + 0.044715 * (x * x * x))))


def _pe_term(pe_ref, w1f_ref, pet):
    for c in range(2):
        t = lax.dot_general(pe_ref[c], w1f_ref[c], (((1,), (0,)), ((), ())),
                            precision=lax.Precision.HIGHEST, preferred_element_type=F32)
        pet[c] = jnp.concatenate([t, t], axis=1)


def _cmp_key_columns(n0, n):
    nidx = n0 + lax.broadcasted_iota(jnp.int32, (n, 2 * LANES), 0)
    cend = nidx * CMP_STRIDE + (CMP_BLOCK - 1)
    lane = lax.broadcasted_iota(jnp.int32, (n, 2 * LANES), 1) % LANES
    return (jnp.where(lane == HEAD_DIM, (cend // SLC_BLOCK).astype(F32), 0.0)
            + jnp.where(lane == HEAD_DIM + 1, (cend % SLC_BLOCK).astype(F32), 0.0))


def _compress_kernel(pt_ref, cache_ref, w1_ref, pe_ref, w1f_ref, w2k_ref, w2v_ref, o_ref,
                     xbuf, pet, sem, tsem, *, pages, n_groups, ppb):
    b = pl.program_id(0)
    gi = pl.program_id(1)
    rows = pages * PAGE_SIZE
    n = rows // CMP_STRIDE

    @pl.when((b == 0) & (gi == 0))
    def _():
        _pe_term(pe_ref, w1f_ref, pet)

    has_next = gi + 1 < n_groups

    def page_copies(p):
        page = pt_ref[b * ppb + gi * pages + p]
        return [pltpu.make_async_copy(cache_ref.at[page, :, pl.ds(c * LANES, LANES)],
                                      xbuf.at[c, pl.ds(p * PAGE_SIZE, PAGE_SIZE)], sem) for c in range(2)]

    def tail_copies():
        page = pt_ref[b * ppb + (gi + 1) * pages]
        return [pltpu.make_async_copy(cache_ref.at[page, pl.ds(0, CMP_STRIDE), pl.ds(c * LANES, LANES)],
                                      xbuf.at[c, pl.ds(rows, CMP_STRIDE)], tsem) for c in range(2)]

    for p in range(pages):
        for cp in page_copies(p):
            cp.start()

    @pl.when(has_next)
    def _():
        for cp in tail_copies():
            cp.start()

    for p in range(pages):
        for cp in page_copies(p):
            cp.wait()

    @pl.when(has_next)
    def _():
        for cp in tail_copies():
            cp.wait()

    @pl.when(jnp.logical_not(has_next))
    def _():
        for c in range(2):
            xbuf[c, rows:rows + CMP_STRIDE, :] = jnp.zeros((CMP_STRIDE, LANES), F32)

    aug = _cmp_key_columns(gi * n, n)
    for c, w2_ref in ((0, w2k_ref), (1, w2v_ref)):
        lhs = jnp.concatenate(
            [xbuf[c, pl.ds(j, n, stride=CMP_STRIDE), :].astype(BF16)
             for j in range(CMP_BLOCK)], axis=1)
        pre = _dot(lhs, w1_ref[c]) + pet[c][0:1, :]
        h = _gelu_tanh(pre).astype(BF16)
        out = _dot(h, w2_ref[...])
        if c == 0:
            o_ref[0, :, 0:2 * LANES] = (out + aug).astype(o_ref.dtype)
        else:
            o_ref[0, :, 2 * LANES:] = out.astype(o_ref.dtype)


def _compress(page_table, cache, w1big, pe8, w1f, w2k, w2v, pages):
    nb, ppb = page_table.shape
    n_groups = ppb // pages
    n = pages * PAGE_SIZE // CMP_STRIDE
    ncols = 2 * LANES + w2v.shape[1]
    once = dict(pipeline_mode=pl.Buffered(1))
    scratch = [
        pltpu.VMEM((2, pages * PAGE_SIZE + CMP_STRIDE, LANES), F32),
        pltpu.VMEM((2, 8, 2 * CMP_HID), F32),
        pltpu.SemaphoreType.DMA(()),
        pltpu.SemaphoreType.DMA(()),
    ]
    grid_spec = pltpu.PrefetchScalarGridSpec(
        num_scalar_prefetch=1,
        grid=(nb, n_groups),
        in_specs=[
            pl.BlockSpec(memory_space=pl.ANY),
            pl.BlockSpec(w1big.shape, lambda i, j, pt: (0, 0, 0), **once),
            pl.BlockSpec(pe8.shape, lambda i, j, pt: (0, 0, 0), **once),
            pl.BlockSpec(w1f.shape, lambda i, j, pt: (0, 0, 0), **once),
            pl.BlockSpec(w2k.shape, lambda i, j, pt: (0, 0), **once),
            pl.BlockSpec(w2v.shape, lambda i, j, pt: (0, 0), **once),
        ],
        out_specs=pl.BlockSpec((1, n, ncols), lambda i, j, pt: (i, j, 0)),
        scratch_shapes=scratch,
    )
    return pl.pallas_call(
        functools.partial(_compress_kernel, pages=pages, n_groups=n_groups, ppb=ppb),
        grid_spec=grid_spec,
        out_shape=jax.ShapeDtypeStruct((nb, n_groups * n, ncols), BF16),
        compiler_params=pltpu.CompilerParams(
            dimension_semantics=("arbitrary", "arbitrary"), vmem_limit_bytes=VMEM_LIMIT),
        name="compress",
    )(page_table.reshape(-1), cache, w1big, pe8, w1f, w2k, w2v)


def _compress_paged_kernel(pt_ref, cache_ref, w1_ref, pe_ref, w1f_ref, w2k_ref, w2v_ref, o_ref,
                           stg, xj, pet, sem, tsem, *, pages, n_groups, ppb, n_steps):
    step = pl.program_id(0) * n_groups + pl.program_id(1)
    chunks_per_page = PAGE_SIZE // CMP_STRIDE
    n = pages * chunks_per_page
    pitch = n + 8
    hid = 2 * CMP_HID

    def page_copies(s, sl):
        first = (s // n_groups) * ppb + (s % n_groups) * pages
        return [pltpu.make_async_copy(cache_ref.at[pt_ref[first + p]], stg.at[sl, p], sem.at[sl])
                for p in range(pages)]

    def tail_copy(s, sl):
        nxt = (s // n_groups) * ppb + (s % n_groups + 1) * pages
        return pltpu.make_async_copy(cache_ref.at[pt_ref[nxt]], stg.at[sl, pages], tsem.at[sl])

    def has_lookahead(s):
        return s % n_groups + 1 < n_groups

    def start_fetch(s, sl):
        for cp in page_copies(s, sl):
            cp.start()

        @pl.when(has_lookahead(s))
        def _():
            tail_copy(s, sl).start()

    def wait_fetch(s, sl):
        for cp in page_copies(s, sl):
            cp.wait()

        @pl.when(has_lookahead(s))
        def _():
            tail_copy(s, sl).wait()

    def relayout(s, sl):
        def scatter_tokens(x_t, chunk0, n_tok, keep=None):
            for c in range(2):
                xt = x_t[c * LANES:(c + 1) * LANES, :].T
                for i in range(n_tok // 8):
                    rows8 = xt[8 * i:8 * i + 8, :]
                    if keep is not None:
                        rows8 = jnp.where(keep, rows8, 0.0)
                    first = (8 * (i % 2)) * pitch + chunk0 + i // 2
                    xj[sl, c, pl.ds(first, 8, stride=pitch), :] = rows8

        for c in range(2):
            for j in range(CMP_STRIDE):
                xj[sl, c, j * pitch + n:(j + 1) * pitch, :] = jnp.zeros((pitch - n, LANES), F32)
        for p in range(pages):
            scatter_tokens(stg[sl, p], p * chunks_per_page, PAGE_SIZE)
        keep = (lax.broadcasted_iota(jnp.int32, (8, LANES), 0) * 0 + jnp.where(has_lookahead(s), 1, 0)) > 0
        scatter_tokens(stg[sl, pages], n, CMP_STRIDE, keep)

    def mlp(sl):
        aug = _cmp_key_columns(pl.program_id(1) * n, n)
        for c, w2_ref in ((0, w2k_ref), (1, w2v_ref)):
            lhs = jnp.concatenate(
                [xj[sl, c, j * pitch:(j + 1) * pitch, :].astype(BF16) for j in range(CMP_STRIDE)], axis=1)
            res = _dot(lhs, w1_ref[c])
            second_half_next = pltpu.roll(res[:, hid:], pitch - 1, axis=0)
            pre = (res[:, :hid] + second_half_next)[0:n] + pet[c][0:1, :]
            h = _gelu_tanh(pre).astype(BF16)
            out = _dot(h, w2_ref[...])
            if c == 0:
                o_ref[0, :, 0:2 * LANES] = (out + aug).astype(o_ref.dtype)
            else:
                o_ref[0, :, 2 * LANES:] = out.astype(o_ref.dtype)

    @pl.when(step == 0)
    def _():
        _pe_term(pe_ref, w1f_ref, pet)
        for sl in range(2):
            stg[sl, pages] = jnp.zeros((KV_COLS, PAGE_SIZE), F32)
        start_fetch(0, 0)
        if n_steps > 1:
            start_fetch(1, 1)
        wait_fetch(0, 0)
        relayout(0, 0)

    for par in range(2):
        @pl.when(step % 2 == par)
        def _(par=par):
            @pl.when(step + 1 < n_steps)
            def _():
                wait_fetch(step + 1, 1 - par)

            @pl.when(step + 2 < n_steps)
            def _():
                start_fetch(step + 2, par)

            relayout(step + 1, 1 - par)
            mlp(par)


def _compress_paged(page_table, cache_t, w1cat, pe8, w1f, w2k, w2v, pages):
    nb, ppb = page_table.shape
    n_groups = ppb // pages
    n = pages * PAGE_SIZE // CMP_STRIDE
    ncols = 2 * LANES + w2v.shape[1]
    once = dict(pipeline_mode=pl.Buffered(1))
    grid_spec = pltpu.PrefetchScalarGridSpec(
        num_scalar_prefetch=1,
        grid=(nb, n_groups),
        in_specs=[
            pl.BlockSpec(memory_space=pl.ANY),
            pl.BlockSpec(w1cat.shape, lambda i, j, pt: (0, 0, 0), **once),
            pl.BlockSpec(pe8.shape, lambda i, j, pt: (0, 0, 0), **once),
            pl.BlockSpec(w1f.shape, lambda i, j, pt: (0, 0, 0), **once),
            pl.BlockSpec(w2k.shape, lambda i, j, pt: (0, 0), **once),
            pl.BlockSpec(w2v.shape, lambda i, j, pt: (0, 0), **once),
        ],
        out_specs=pl.BlockSpec((1, n, ncols), lambda i, j, pt: (i, j, 0)),
        scratch_shapes=[
            pltpu.VMEM((2, pages + 1, KV_COLS, PAGE_SIZE), F32),
            pltpu.VMEM((2, 2, CMP_STRIDE * (n + 8), LANES), F32),
            pltpu.VMEM((2, 8, 2 * CMP_HID), F32),
            pltpu.SemaphoreType.DMA((2,)),
            pltpu.SemaphoreType.DMA((2,)),
        ],
    )
    return pl.pallas_call(
        functools.partial(_compress_paged_kernel, pages=pages, n_groups=n_groups, ppb=ppb,
                          n_steps=nb * n_groups),
        grid_spec=grid_spec,
        out_shape=jax.ShapeDtypeStruct((nb, n_groups * n, ncols), BF16),
        compiler_params=pltpu.CompilerParams(
            dimension_semantics=("arbitrary", "arbitrary"), vmem_limit_bytes=VMEM_LIMIT),
        name="compress_paged",
    )(page_table.reshape(-1), cache_t, w1cat, pe8, w1f, w2k, w2v)


def _compress_weights(w_cmp1, w_cmp2, pe_cmp):
    w1 = w_cmp1
    z = jnp.zeros_like(w1)
    w1big = jnp.stack([jnp.concatenate([w1, z], -1), jnp.concatenate([z, w1], -1)], axis=2)
    w1cat = jnp.concatenate([w1big[:, :CMP_STRIDE], w1big[:, CMP_STRIDE:]], axis=-1)
    w1cat = w1cat.reshape(2, CMP_STRIDE * 2 * HEAD_DIM, 4 * CMP_HID).astype(BF16)
    w1big = w1big.reshape(2, CMP_BLOCK * 2 * HEAD_DIM, 2 * CMP_HID).astype(BF16)
    pe8 = jnp.broadcast_to(pe_cmp.reshape(2, 1, CMP_BLOCK * HEAD_DIM), (2, 8, CMP_BLOCK * HEAD_DIM))
    w1f = w1.reshape(2, CMP_BLOCK * HEAD_DIM, CMP_HID)

    def w2_layout(w2):
        out = jnp.zeros((2, CMP_HID, 2, LANES), F32)
        for h in range(2):
            out = out.at[h, :, h, 0:HEAD_DIM].set(w2)
        return out.reshape(2 * CMP_HID, 2 * LANES).astype(BF16)

    return w1big, w1cat, (pe8, w1f, w2_layout(w_cmp2[0]), w2_layout(w_cmp2[1]))


SLC_CHUNK = 1024
WIN_SPAN = WINDOW + Q_TILE


def _nsa_prompt_kernel(q_ref, gt_ref, cmp_ref, ks_ref, kw_ref, cov_ref, e_ref, o_ref):
    i = pl.program_id(1)
    qs = i * Q_TILE
    q = q_ref[0]
    gates = gt_ref[0]
    qpos_t = qs + lax.broadcasted_iota(jnp.int32, (Q_TILE, 1), 0)
    qpos_gt = jnp.concatenate([qpos_t] * GROUP, axis=0)
    n_cmp_pad = cmp_ref.shape[1]
    cend = lax.broadcasted_iota(jnp.int32, (1, n_cmp_pad), 1) * CMP_STRIDE + (CMP_BLOCK - 1)
    jl = lax.broadcasted_iota(jnp.int32, (1, LANES), 1)
    cur = qpos_t // SLC_BLOCK
    forced = (jl == 0) | (jl == cur) | (jl == cur - 1)
    n_sel = ks_ref.shape[1] // SLC_BLOCK
    jrow = lax.broadcasted_iota(jnp.int32, (n_sel, Q_TILE), 0)

    kend = qs + Q_TILE
    woff = pl.multiple_of(jnp.maximum(kend - WIN_SPAN, 0), Q_TILE)
    d_w = qpos_t - (woff + lax.broadcasted_iota(jnp.int32, (1, WIN_SPAN), 1))
    bias_w = jnp.tile(jnp.where((d_w >= 0) & (d_w < WINDOW), 0.0, NEG), (GROUP, 1))

    qks, o_cmps, sels, a_wins = [], [], [], []
    for k in range(N_KV_HEADS):
        qk = jnp.concatenate(
            [q[:, (GROUP * k + g) * LANES:(GROUP * k + g + 1) * LANES] for g in range(GROUP)], axis=0)

        s = _dot_nt(qk, kw_ref[0, pl.ds(woff, WIN_SPAN), k * LANES:(k + 1) * LANES]) + bias_w
        p = jnp.exp(s - jnp.max(s, -1, keepdims=True)).astype(BF16)
        a_wins.append(_dot(p, kw_ref[0, pl.ds(woff, WIN_SPAN), (2 + k) * LANES:(3 + k) * LANES]))

        kc = cmp_ref[0, :, k * LANES:(k + 1) * LANES]
        vc = cmp_ref[0, :, (2 + k) * LANES:(3 + k) * LANES]
        valid = qpos_gt >= cend
        s = jnp.where(valid, _dot_nt(qk, kc), NEG)
        e = jnp.where(valid, jnp.exp(s - jnp.max(s, -1, keepdims=True)), 0.0)
        p = e / jnp.maximum(jnp.sum(e, -1, keepdims=True), 1e-30)
        o_cmp = _dot(p.astype(BF16), vc)

        psum = p[0:Q_TILE]
        for g in range(1, GROUP):
            psum = psum + p[g * Q_TILE:(g + 1) * Q_TILE]
        p_hi = psum.astype(BF16)
        p_lo = (psum - p_hi.astype(F32)).astype(BF16)
        imp = _dot(p_hi, cov_ref[...]) + _dot(p_lo, cov_ref[...])
        score = (jnp.where(jl * SLC_BLOCK <= qpos_t, imp, -1.0)
                 + jnp.where(forced, FORCE_BONUS, 0.0))
        sc_t = score.T[0:n_sel]
        cnt = jnp.zeros((n_sel, Q_TILE), jnp.int32)
        for jp in range(n_sel):
            row = sc_t[jp:jp + 1, :]
            tie = jnp.where(jrow > jp, 1, 0)
            cnt = cnt + jnp.where(row > sc_t, 1, jnp.where(row == sc_t, tie, 0))
        sel_t = jnp.where(cnt < TOP_K, 1.0, 0.0)
        sel_t = jnp.concatenate([sel_t, jnp.zeros((LANES - n_sel, Q_TILE), F32)], axis=0)
        qks.append(qk)
        o_cmps.append(o_cmp)
        sels.append(sel_t.T.astype(BF16))

    rows = GROUP * Q_TILE

    ciota = lax.broadcasted_iota(jnp.int32, (1, SLC_CHUNK), 1)

    def slc_step(c, carry):
        off = pl.multiple_of(c * SLC_CHUNK, SLC_CHUNK)
        causal = (c * SLC_CHUNK + ciota) <= qpos_t
        out = []
        for k in range(N_KV_HEADS):
            m, acc = carry[k]
            in_sel = _dot(sels[k], e_ref[c])
            bias = jnp.tile(jnp.where((in_sel > 0.5) & causal, 0.0, NEG), (GROUP, 1))
            s = _dot_nt(qks[k], ks_ref[0, pl.ds(off, SLC_CHUNK), k * LANES:(k + 1) * LANES]) + bias
            m_new = jnp.maximum(m, jnp.max(s, -1, keepdims=True))
            p = jnp.exp(s - m_new).astype(BF16)
            acc = jnp.exp(m - m_new) * acc + _dot(
                p, ks_ref[0, pl.ds(off, SLC_CHUNK), (2 + k) * LANES:(3 + k) * LANES])
            out.append((m_new, acc))
        return tuple(out)

    init = tuple((jnp.full((rows, 1), 0.1 * NEG, F32), jnp.zeros((rows, LANES), F32))
                 for _ in range(N_KV_HEADS))
    slc = lax.fori_loop(0, (kend + SLC_CHUNK - 1) // SLC_CHUNK, slc_step, init)

    for k in range(N_KV_HEADS):
        a_win = a_wins[k]
        a_slc = slc[k][1]
        o_cmp = o_cmps[k]
        heads = []
        for g in range(GROUP):
            h = GROUP * k + g
            rs = slice(g * Q_TILE, (g + 1) * Q_TILE)
            g_slc = gates[:, N_HEADS + h:N_HEADS + h + 1] / jnp.maximum(
                a_slc[rs][:, HEAD_DIM:HEAD_DIM + 1], 1e-30)
            g_win = gates[:, 2 * N_HEADS + h:2 * N_HEADS + h + 1] / jnp.maximum(
                a_win[rs][:, HEAD_DIM:HEAD_DIM + 1], 1e-30)
            heads.append(gates[:, h:h + 1] * o_cmp[rs] + g_slc * a_slc[rs] + g_win * a_win[rs])
        low = lax.broadcasted_iota(jnp.int32, (Q_TILE, LANES), 1) < HEAD_DIM
        for pair in range(GROUP // 2):
            both = jnp.where(low, heads[2 * pair], pltpu.roll(heads[2 * pair + 1], HEAD_DIM, axis=1))
            col = (GROUP // 2 * k + pair) * LANES
            o_ref[0, :, col:col + LANES] = both.astype(o_ref.dtype)


def _nsa_prompt(q_aug, gates, cmp, ks_aug, kw_aug, cover, expand):
    b, t, _ = q_aug.shape
    return pl.pallas_call(
        _nsa_prompt_kernel,
        grid=(b, t // Q_TILE),
        in_specs=[
            pl.BlockSpec((1, Q_TILE, N_HEADS * LANES), lambda i, j: (i, j, 0)),
            pl.BlockSpec((1, Q_TILE, LANES), lambda i, j: (i, j, 0)),
            pl.BlockSpec((1,) + cmp.shape[1:], lambda i, j: (i, 0, 0)),
            pl.BlockSpec((1,) + ks_aug.shape[1:], lambda i, j: (i, 0, 0)),
            pl.BlockSpec((1,) + kw_aug.shape[1:], lambda i, j: (i, 0, 0)),
            pl.BlockSpec(cover.shape, lambda i, j: (0, 0)),
            pl.BlockSpec(expand.shape, lambda i, j: (0, 0, 0)),
        ],
        out_specs=pl.BlockSpec((1, Q_TILE, MIX_ATTN), lambda i, j: (i, j, 0)),
        out_shape=jax.ShapeDtypeStruct((b, t, MIX_ATTN), BF16),
        compiler_params=pltpu.CompilerParams(
            dimension_semantics=("arbitrary", "arbitrary"), vmem_limit_bytes=VMEM_LIMIT),
        name="nsa_prompt",
    )(q_aug, gates, cmp, ks_aug, kw_aug, cover, expand)


def _selection_constants(n_cmp_pad, n_cmp, n_sel, seq_len):
    ci = np.arange(n_cmp_pad)[:, None] * CMP_STRIDE
    sj = np.arange(LANES)[None, :] * SLC_BLOCK
    cover = ((ci + CMP_BLOCK > sj) & (ci < sj + SLC_BLOCK)
             & (np.arange(n_cmp_pad)[:, None] < n_cmp) & (np.arange(LANES)[None, :] < n_sel))
    m = np.arange(seq_len).reshape(seq_len // SLC_CHUNK, 1, SLC_CHUNK)
    expand = (m // SLC_BLOCK) == np.arange(LANES).reshape(1, LANES, 1)
    return jnp.asarray(cover, BF16), jnp.asarray(expand, BF16)


FF_CHUNK = 1024


def _tail_kernel(*refs, n_mix, has_pre):
    x_ref = refs[0]
    mix_refs = refs[1:1 + n_mix]
    pos = 1 + n_mix
    pre_ref = refs[pos] if has_pre else None
    pos += int(has_pre)
    p_ref = refs[pos]
    wmix_refs = refs[pos + 1:pos + 1 + n_mix]
    (gmlp_ref, wup_ref, wdown_ref, gple_ref, wple_ref, wpg_ref, gfin_ref,
     y_ref) = refs[pos + 1 + n_mix:]
    x = x_ref[...]
    for m_ref, w_ref in zip(mix_refs, wmix_refs):
        x = x + _dot(m_ref[...], w_ref[...])
    if has_pre:
        x = x + pre_ref[...]
    xn = _rms(x, gmlp_ref[...]).astype(BF16)
    acc = jnp.zeros_like(x)
    for c in range(D_FF // FF_CHUNK):
        h = _dot(xn, wup_ref[:, c * FF_CHUNK:(c + 1) * FF_CHUNK])
        h = jnp.square(jnp.maximum(h, 0.0)).astype(BF16)
        acc = acc + _dot(h, wdown_ref[c * FF_CHUNK:(c + 1) * FF_CHUNK, :])
    x = x + acc
    gate = _sigmoid(_dot(_rms(x, gple_ref[...]).astype(BF16), wpg_ref[...]))
    x = x + _dot(p_ref[...].astype(BF16), wple_ref[...]) * gate
    y_ref[...] = _rms(x, gfin_ref[...])


def _layer_tail(x2d, mixes, wmixes, pre, p2d, g_mlp, w_up, w_down, g_ple, w_ple, w_pg, g_final, tm):
    rows = x2d.shape[0]
    row_spec = lambda a: pl.BlockSpec((tm, a.shape[1]), lambda i: (i, 0))
    const = lambda a: pl.BlockSpec(a.shape, lambda i: (0, 0), pipeline_mode=pl.Buffered(1))
    vec = lambda a: a.reshape(1, -1)
    has_pre = pre is not None
    args = [x2d, *mixes] + ([pre] if has_pre else []) + [p2d]
    specs = [row_spec(a) for a in args]
    consts = [*wmixes, vec(g_mlp), w_up, w_down, vec(g_ple), w_ple, w_pg, vec(g_final)]
    return pl.pallas_call(
        functools.partial(_tail_kernel, n_mix=len(mixes), has_pre=has_pre),
        grid=(rows // tm,),
        in_specs=specs + [const(a) for a in consts],
        out_specs=pl.BlockSpec((tm, D_MODEL), lambda i: (i, 0)),
        out_shape=jax.ShapeDtypeStruct((rows, D_MODEL), F32),
        compiler_params=pltpu.CompilerParams(
            dimension_semantics=("arbitrary",), vmem_limit_bytes=VMEM_LIMIT),
        name="layer_tail",
    )(*args, *consts)


def _proj_weights(w_in):
    c_q = Q_COLS
    c_gt = Q_COLS + 3 * KV_COLS
    w_packed = jnp.concatenate([
        w_in[:, :c_q] * (HEAD_DIM ** -0.5), w_in[:, c_q:c_gt],
        jnp.pad(w_in[:, c_gt:c_gt + GATE_COLS], ((0, 0), (0, LANES - GATE_COLS))),
        w_in[:, c_gt + GATE_COLS:]], axis=1).astype(BF16)
    qfill = np.zeros((1, N_HEADS * LANES), np.float32)
    for h, slope in enumerate(_slopes()):
        qfill[0, h * LANES + HEAD_DIM] = SLC_BLOCK * slope
        qfill[0, h * LANES + HEAD_DIM + 1] = slope
    return w_packed, jnp.asarray(qfill)


TQ_PAD = 8
SEL_LANES = 384
BIG = 1e9


def _masked_softmax(s, valid):
    s = jnp.where(valid, s, NEG)
    e = jnp.where(valid, jnp.exp(s - jnp.max(s, -1, keepdims=True)), 0.0)
    return e / jnp.maximum(jnp.sum(e, -1, keepdims=True), 1e-30)


def _rows_gt(x, k, width):
    return jnp.concatenate(
        [x[:, (GROUP * k + g) * LANES:(GROUP * k + g) * LANES + width] for g in range(GROUP)], axis=0)


def _sample_select_kernel(q_ref, cmp_ref, cov_ref, ocmp_ref, idx_ref, *, past_len, n_sel):
    rows = GROUP * TQ_PAD
    t_gt = lax.broadcasted_iota(jnp.int32, (rows, 1), 0) % TQ_PAD
    n_cmp_pad = cmp_ref.shape[1]
    cend = lax.broadcasted_iota(jnp.int32, (1, n_cmp_pad), 1) * CMP_STRIDE + (CMP_BLOCK - 1)
    qpos_t = past_len + lax.broadcasted_iota(jnp.int32, (TQ_PAD, 1), 0)
    jl = lax.broadcasted_iota(jnp.int32, (1, SEL_LANES), 1)
    cur = qpos_t // SLC_BLOCK
    forced = (jl == 0) | (jl == cur) | (jl == cur - 1)
    scores = []
    n_seq = q_ref.shape[0]
    for bb in range(n_seq):
        q = q_ref[bb].astype(F32)
        for k in range(N_KV_HEADS):
            qk = _rows_gt(q, k, LANES).astype(BF16)
            kc = cmp_ref[bb, :, k * LANES:(k + 1) * LANES]
            vc = cmp_ref[bb, :, (2 + k) * LANES:(3 + k) * LANES]
            p = _masked_softmax(_dot_nt(qk, kc), (past_len + t_gt) >= cend)
            ocmp_ref[bb, k] = _dot(p.astype(BF16), vc)
            psum = p[0:TQ_PAD]
            for g in range(1, GROUP):
                psum = psum + p[g * TQ_PAD:(g + 1) * TQ_PAD]
            p_hi = psum.astype(BF16)
            p_lo = (psum - p_hi.astype(F32)).astype(BF16)
            imp = _dot(p_hi, cov_ref[...]) + _dot(p_lo, cov_ref[...])
            score = (jnp.where(jl * SLC_BLOCK <= qpos_t, imp, -1.0)
                     + jnp.where(forced, FORCE_BONUS, 0.0))
            scores.append(jnp.where(jl < n_sel, score, -BIG))
    sc = jnp.concatenate(scores, axis=0)
    lane = lax.broadcasted_iota(jnp.int32, sc.shape, 1).astype(F32)
    out_lane = lax.broadcasted_iota(jnp.int32, (sc.shape[0], LANES), 1)
    picked = jnp.zeros((sc.shape[0], LANES), F32)
    for s in range(TOP_K):
        m = jnp.max(sc, -1, keepdims=True)
        am = jnp.min(jnp.where(sc == m, lane, BIG), -1, keepdims=True)
        picked = jnp.where(out_lane == s, am, picked)
        sc = jnp.where(lane == am, -2.0 * BIG, sc)
    per_seq = N_KV_HEADS * TQ_PAD
    for bb in range(n_seq):
        idx_ref[bb] = picked[bb * per_seq:(bb + 1) * per_seq].astype(jnp.int32)


SELECT_SEQS = 4


def _sample_select(q_s, cmp_s, cover_s, past_len, n_sel):
    db = q_s.shape[0]
    ns = SELECT_SEQS
    return pl.pallas_call(
        functools.partial(_sample_select_kernel, past_len=past_len, n_sel=n_sel),
        grid=(db // ns,),
        in_specs=[
            pl.BlockSpec((ns,) + q_s.shape[1:], lambda i: (i, 0, 0)),
            pl.BlockSpec((ns,) + cmp_s.shape[1:], lambda i: (i, 0, 0)),
            pl.BlockSpec(cover_s.shape, lambda i: (0, 0)),
        ],
        out_specs=[
            pl.BlockSpec((ns, N_KV_HEADS, GROUP * TQ_PAD, LANES), lambda i: (i, 0, 0, 0)),
            pl.BlockSpec((ns, N_KV_HEADS * TQ_PAD, LANES), lambda i: (i, 0, 0)),
        ],
        out_shape=[
            jax.ShapeDtypeStruct((db, N_KV_HEADS, GROUP * TQ_PAD, LANES), F32),
            jax.ShapeDtypeStruct((db, N_KV_HEADS * TQ_PAD, LANES), jnp.int32),
        ],
        compiler_params=pltpu.CompilerParams(
            dimension_semantics=("arbitrary",), vmem_limit_bytes=VMEM_LIMIT),
        name="sample_select",
    )(q_s, cmp_s, cover_s)


def _joint_softmax(s_a, valid_a, s_b, valid_b):
    s_a = jnp.where(valid_a, s_a, NEG)
    s_b = jnp.where(valid_b, s_b, NEG)
    m = jnp.maximum(jnp.max(s_a, -1, keepdims=True), jnp.max(s_b, -1, keepdims=True))
    e_a = jnp.where(valid_a, jnp.exp(s_a - m), 0.0)
    e_b = jnp.where(valid_b, jnp.exp(s_b - m), 0.0)
    inv = 1.0 / jnp.maximum(jnp.sum(e_a, -1, keepdims=True) + jnp.sum(e_b, -1, keepdims=True), 1e-30)
    return e_a * inv, e_b * inv


def _sample_attn_kernel(pt_ref, idx_ref, q_ref, gt_ref, ocmp_ref, ksn_ref, kwn_ref, slc_ref, win_ref,
                        wout_ref, o_ref, kbuf, vbuf, sem, *, past_len, tq, ppb):
    b = pl.program_id(0)
    n_past_blocks = past_len // SLC_BLOCK
    bpp = PAGE_SIZE // SLC_BLOCK
    rows = GROUP * TQ_PAD

    slot = b % 2

    def sel_index(k, t, s, bb=b):
        return idx_ref[((bb * N_KV_HEADS + k) * TQ_PAD + t) * TOP_K + s]

    def block_copies(bb, sl):
        cps = []
        for k in range(N_KV_HEADS):
            for t in range(tq):
                for s in range(TOP_K):
                    blk = jnp.minimum(sel_index(k, t, s, bb), n_past_blocks - 1)
                    page = pt_ref[bb * ppb + blk // bpp]
                    dst = pl.ds(s * PAGE_SIZE, PAGE_SIZE)
                    cps.append(pltpu.make_async_copy(
                        slc_ref.at[page, pl.ds(k * HEAD_DIM, HEAD_DIM), :], kbuf.at[sl, k, t, :, dst],
                        sem.at[sl]))
                    cps.append(pltpu.make_async_copy(
                        slc_ref.at[page, pl.ds((N_KV_HEADS + k) * HEAD_DIM, HEAD_DIM), :],
                        vbuf.at[sl, k, t, :, dst], sem.at[sl]))
        return cps

    @pl.when(b == 0)
    def _():
        for cp in block_copies(b, slot):
            cp.start()

    @pl.when(b + 1 < pl.num_programs(0))
    def _():
        for cp in block_copies(b + 1, 1 - slot):
            cp.start()

    q = q_ref[0].astype(F32)
    gates = gt_ref[0]
    t_gt = lax.broadcasted_iota(jnp.int32, (rows, 1), 0) % TQ_PAD
    g_gt = lax.broadcasted_iota(jnp.int32, (rows, 1), 0) // TQ_PAD
    qpos = past_len + t_gt
    lane_pg = lax.broadcasted_iota(jnp.int32, (1, PAGE_SIZE), 1)
    new_pos = past_len + lax.broadcasted_iota(jnp.int32, (1, TQ_PAD), 1)
    win_len = win_ref.shape[2]
    win_pos = past_len - win_len + lax.broadcasted_iota(jnp.int32, (1, win_len), 1)
    slopes = _slopes()

    for cp in block_copies(b, slot):
        cp.wait()

    acc = jnp.zeros((TQ_PAD, D_MODEL), F32)
    for k in range(N_KV_HEADS):
        qk = _rows_gt(q, k, HEAD_DIM).astype(BF16)
        slope = jnp.zeros((rows, 1), F32)
        for g in range(GROUP):
            slope = jnp.where(g_gt == g, float(slopes[GROUP * k + g]), slope)
        ks_new = ksn_ref[0, :, k * HEAD_DIM:(k + 1) * HEAD_DIM].astype(BF16)
        vs_new = ksn_ref[0, :, (N_KV_HEADS + k) * HEAD_DIM:(N_KV_HEADS + k + 1) * HEAD_DIM].astype(BF16)
        kw_new = kwn_ref[0, :, k * HEAD_DIM:(k + 1) * HEAD_DIM].astype(BF16)
        vw_new = kwn_ref[0, :, (N_KV_HEADS + k) * HEAD_DIM:(N_KV_HEADS + k + 1) * HEAD_DIM].astype(BF16)

        kpos, chosen, owner, s_parts = [], [], [], []
        has_new = jnp.zeros((rows, 1), jnp.int32)
        for t in range(tq):
            n_new = 0
            for s in range(TOP_K):
                blk = sel_index(k, t, s)
                is_past = blk < n_past_blocks
                page_blk = jnp.minimum(blk, n_past_blocks - 1)
                kpos.append((page_blk // bpp) * PAGE_SIZE + lane_pg)
                half = jnp.where(is_past, page_blk % bpp, -1)
                chosen.append(jnp.where((lane_pg // SLC_BLOCK) == half, 1, 0))
                owner.append(jnp.full((1, PAGE_SIZE), t, jnp.int32))
                n_new = n_new + jnp.where(is_past, 0, 1)
            has_new = jnp.where(t_gt == t, n_new, has_new)
            s_parts.append(_dot(qk, kbuf[slot, k, t].astype(BF16)))
        kpos = jnp.concatenate(kpos, axis=1)
        chosen = jnp.concatenate(chosen, axis=1)
        owner = jnp.concatenate(owner, axis=1)
        valid = (t_gt == owner) & (chosen > 0) & (kpos <= qpos)
        valid_new = (new_pos <= qpos) & (has_new > 0) & (t_gt < tq)
        s_past = jnp.concatenate(s_parts, axis=1) - slope * (qpos - kpos).astype(F32)
        s_new = _dot_nt(qk, ks_new) - slope * (qpos - new_pos).astype(F32)
        p_past, p_new = _joint_softmax(s_past, valid, s_new, valid_new)
        p_past = p_past.astype(BF16)
        o_slc = _dot(p_new.astype(BF16), vs_new)
        seg = TOP_K * PAGE_SIZE
        for t in range(tq):
            o_slc = o_slc + _dot_nt(p_past[:, t * seg:(t + 1) * seg], vbuf[slot, k, t].astype(BF16))

        d_w = qpos - win_pos
        d_n = qpos - new_pos
        s_w = _dot(qk, win_ref[0, k * HEAD_DIM:(k + 1) * HEAD_DIM, :].astype(BF16)) - slope * d_w.astype(F32)
        s_n = _dot_nt(qk, kw_new) - slope * d_n.astype(F32)
        p_w, p_n = _joint_softmax(s_w, (d_w >= 0) & (d_w < WINDOW) & (win_pos >= 0),
                                  s_n, (d_n >= 0) & (d_n < WINDOW))
        v_w = win_ref[0, (N_KV_HEADS + k) * HEAD_DIM:(N_KV_HEADS + k + 1) * HEAD_DIM, :].astype(BF16)
        o_win = _dot_nt(p_w.astype(BF16), v_w) + _dot(p_n.astype(BF16), vw_new)

        def gate(branch):
            return jnp.concatenate(
                [gates[:, branch * N_HEADS + GROUP * k + g:branch * N_HEADS + GROUP * k + g + 1]
                 for g in range(GROUP)], axis=0)

        o = (gate(0) * ocmp_ref[0, k][:, 0:HEAD_DIM] + gate(1) * o_slc + gate(2) * o_win).astype(BF16)
        for g in range(GROUP):
            h = GROUP * k + g
            acc = acc + _dot(o[g * TQ_PAD:(g + 1) * TQ_PAD], wout_ref[h * HEAD_DIM:(h + 1) * HEAD_DIM, :])
    o_ref[0] = acc


def _sample_attn(page_table, idx, q_s, gates_s, ocmp, ks_new, kw_new, slc_t, win_t, w_out_attn,
                 past_len, tq):
    db, ppb = page_table.shape
    blk3 = lambda a: pl.BlockSpec((1,) + a.shape[1:], lambda i, pt, ix: (i, 0, 0))
    grid_spec = pltpu.PrefetchScalarGridSpec(
        num_scalar_prefetch=2,
        grid=(db,),
        in_specs=[
            blk3(q_s), blk3(gates_s),
            pl.BlockSpec((1,) + ocmp.shape[1:], lambda i, pt, ix: (i, 0, 0, 0)),
            blk3(ks_new), blk3(kw_new),
            pl.BlockSpec(memory_space=pl.ANY),
            blk3(win_t),
            pl.BlockSpec(w_out_attn.shape, lambda i, pt, ix: (0, 0)),
        ],
        out_specs=pl.BlockSpec((1, TQ_PAD, D_MODEL), lambda i, pt, ix: (i, 0, 0)),
        scratch_shapes=[
            pltpu.VMEM((2, N_KV_HEADS, tq, HEAD_DIM, TOP_K * PAGE_SIZE), F32),
            pltpu.VMEM((2, N_KV_HEADS, tq, HEAD_DIM, TOP_K * PAGE_SIZE), F32),
            pltpu.SemaphoreType.DMA((2,)),
        ],
    )
    return pl.pallas_call(
        functools.partial(_sample_attn_kernel, past_len=past_len, tq=tq, ppb=ppb),
        grid_spec=grid_spec,
        out_shape=jax.ShapeDtypeStruct((db, TQ_PAD, D_MODEL), F32),
        compiler_params=pltpu.CompilerParams(
            dimension_semantics=("arbitrary",), vmem_limit_bytes=VMEM_LIMIT),
        name="sample_attn",
    )(page_table.reshape(-1), idx, q_s, gates_s, ocmp, ks_new, kw_new, slc_t, win_t, w_out_attn)


def _pages_feature_major(cache):
    return jnp.transpose(cache, (0, 2, 3, 4, 1)).reshape(cache.shape[0], KV_COLS, cache.shape[1])


def kernel(x_prompt, x_sample, p_prompt, p_sample, cache_cmp_kv, cache_slc_kv, cache_win_kv, state_conv, page_table, g_attn, w_in, w_cmp1, w_cmp2, pe_cmp, conv_w, conv_b, ln_conv_g, ln_conv_b, w_out, g_mlp, w_up, w_down, g_ple, w_ple, w_ple_gate, g_final):
    b, t, _ = x_prompt.shape
    db, tq, _ = x_sample.shape
    win_buf = cache_win_kv.shape[2]
    kv5 = lambda a, nb, nt: a.reshape(1, nb, nt, 2, N_KV_HEADS, HEAD_DIM)

    w_out_attn = w_out[0][:MIX_ATTN].astype(BF16)
    w_out_conv = w_out[0][MIX_ATTN:].astype(BF16)
    tail_w = (g_mlp[0], w_up[0].astype(BF16), w_down[0].astype(BF16), g_ple[0],
              w_ple[0].astype(BF16), w_ple_gate[0].astype(BF16), g_final)
    g_row = g_attn[0].reshape(1, D_MODEL)

    w_packed, qfill = _proj_weights(w_in[0])
    po = _project(x_prompt.reshape(b * t, D_MODEL), g_row, w_packed, qfill, 512, t, prompt=True)
    u_p = po["u"].reshape(b, t, CONV_CH)
    conv_p = _conv_tail(jnp.zeros((b, HALO, CONV_CH), F32), u_p, conv_w[0], conv_b[0],
                        ln_conv_g[0], ln_conv_b[0], 512)
    ppb = t // PAGE_SIZE
    w1big, w1cat, cw = _compress_weights(w_cmp1[0], w_cmp2[0], pe_cmp[0])
    cmp_p = _compress(jnp.arange(b * ppb, dtype=jnp.int32).reshape(b, ppb),
                      po["kc"].reshape(b * ppb, PAGE_SIZE, KV_COLS), w1big, *cw, pages=ppb)
    n_chunk = t // CMP_STRIDE
    cover, expand = _selection_constants(n_chunk, n_chunk - 1, t // SLC_BLOCK, t)
    attn_p = _nsa_prompt(po["q_aug"].reshape(b, t, -1), po["gates"].reshape(b, t, LANES), cmp_p,
                         po["ks_aug"].reshape(b, t, -1), po["kw_aug"].reshape(b, t, -1), cover, expand)
    y_prompt = _layer_tail(x_prompt.reshape(b * t, D_MODEL),
                           [attn_p.reshape(b * t, -1), conv_p.reshape(b * t, CONV_CH)],
                           [w_out_attn, w_out_conv], None, p_prompt[0].reshape(b * t, PLE_DIM),
                           *tail_w, tm=512).reshape(b, t, D_MODEL)

    xs = jnp.pad(x_sample, ((0, 0), (0, TQ_PAD - tq), (0, 0))).reshape(db * TQ_PAD, D_MODEL)
    so = _project(xs, g_row, w_packed, qfill, db * TQ_PAD, TQ_PAD, prompt=False)
    rs = lambda a: a.reshape(db, TQ_PAD, -1)
    u_s = rs(so["u"])
    kc_s, ks_s, kw_s = rs(so["kc"])[:, :tq], rs(so["ks"])[:, :tq], rs(so["kw"])[:, :tq]
    state32 = jnp.pad(state_conv[0], ((0, 0), (HALO - (CONV_W - 1), 0), (0, 0)))
    conv_s = _conv_tail(state32, u_s, conv_w[0], conv_b[0], ln_conv_g[0], ln_conv_b[0], TQ_PAD)

    past_len = page_table.shape[1] * PAGE_SIZE
    assert (past_len + tq) // CMP_STRIDE == past_len // CMP_STRIDE and tq <= TQ_PAD
    n_chunk_s = past_len // CMP_STRIDE
    n_sel_s = past_len // SLC_BLOCK + 1
    cmp_s = _compress_paged(page_table, _pages_feature_major(cache_cmp_kv[0]), w1cat, *cw, pages=64)
    ci = np.arange(n_chunk_s)[:, None] * CMP_STRIDE
    sj = np.arange(SEL_LANES)[None, :] * SLC_BLOCK
    cover_s = jnp.asarray((ci + CMP_BLOCK > sj) & (ci < sj + SLC_BLOCK)
                          & (np.arange(n_chunk_s)[:, None] < n_chunk_s - 1)
                          & (np.arange(SEL_LANES)[None, :] < n_sel_s), BF16)
    q_s = rs(so["q_aug"])
    ocmp_s, idx_s = _sample_select(q_s, cmp_s, cover_s, past_len, n_sel_s)
    pre_s = _sample_attn(page_table, idx_s[:, :, :TOP_K].reshape(-1), q_s, rs(so["gates"]), ocmp_s,
                         rs(so["ks"]), rs(so["kw"]), _pages_feature_major(cache_slc_kv[0]),
                         _pages_feature_major(cache_win_kv[0]), w_out_attn,
                         past_len, tq)
    p_s = jnp.pad(p_sample[0], ((0, 0), (0, TQ_PAD - tq), (0, 0))).reshape(db * TQ_PAD, PLE_DIM)
    y_sample = _layer_tail(xs, [conv_s.reshape(db * TQ_PAD, CONV_CH)], [w_out_conv],
                           pre_s.reshape(db * TQ_PAD, D_MODEL), p_s, *tail_w,
                           tm=db * TQ_PAD).reshape(db, TQ_PAD, D_MODEL)[:, :tq]

    def token_major(a_t):
        nt = a_t.shape[2]
        return jnp.transpose(a_t.reshape(b, 2, N_KV_HEADS, HEAD_DIM, nt), (0, 4, 1, 2, 3))[None]

    new_win = jnp.concatenate([cache_win_kv[:, :, tq:], kv5(kw_s, db, tq)], 2)
    new_conv_s = jnp.concatenate([state_conv[0], u_s[:, :tq]], 1)[:, -(CONV_W - 1):]
    return (y_prompt, y_sample,
            token_major(po["kc_t"]), token_major(po["ks_t"]), token_major(po["kw_t"][:, :, t - win_buf:]),
            u_p[:, -(CONV_W - 1):][None],
            kv5(kc_s, db, tq), kv5(ks_s, db, tq), new_win, new_conv_s[None])
```

```python
import functools

import numpy as np
import jax
import jax.numpy as jnp
from jax import lax
from jax.experimental import pallas as pl
from jax.experimental.pallas import tpu as pltpu

D_MODEL = 1024
N_HEADS = 8
HEAD_DIM = 64
N_KV_HEADS = 2
GROUP = N_HEADS // N_KV_HEADS
MIX_ATTN = N_HEADS * HEAD_DIM
CONV_CH = D_MODEL - MIX_ATTN
CONV_W = 31
CMP_BLOCK = 32
CMP_STRIDE = 16
CMP_HID = 2 * HEAD_DIM
SLC_BLOCK = 64
TOP_K = 16
WINDOW = 512
PAGE_SIZE = 128
D_FF = 4 * D_MODEL
PLE_DIM = 256
Q_COLS = N_HEADS * HEAD_DIM
KV_COLS = 2 * N_KV_HEADS * HEAD_DIM
GATE_COLS = 3 * N_HEADS
EPS = 1e-6
NEG = -1e30
FORCE_BONUS = 1e4

LANES = 128
Q_TILE = 256
KEY_TILE = 256
VMEM_LIMIT = 56 * 1024 * 1024

F32 = jnp.float32
BF16 = jnp.bfloat16


def _dot(a, b):
    return jnp.dot(a, b, preferred_element_type=F32)


def _dot_nt(a, b):
    return lax.dot_general(a, b, (((1,), (1,)), ((), ())), preferred_element_type=F32)


def _sigmoid(x):
    return 1.0 / (1.0 + jnp.exp(-x))


def _rms(x, g):
    return x * lax.rsqrt(jnp.mean(x * x, -1, keepdims=True) + EPS) * g


def _slopes():
    return 2.0 ** (-8.0 * np.arange(1, N_HEADS + 1) / N_HEADS)


def _expand_heads(y, fill):
    low = lax.broadcasted_iota(jnp.int32, (y.shape[0], LANES), 1) < HEAD_DIM
    tiles = []
    for i in range(y.shape[1] // HEAD_DIM):
        pair = y[:, (i // 2) * LANES:(i // 2 + 1) * LANES]
        data = pair if i % 2 == 0 else pltpu.roll(pair, HEAD_DIM, axis=1)
        tiles.append(jnp.where(low, data, fill[i]))
    return tiles


def _store_tiles(o_ref, tiles):
    for i, t in enumerate(tiles):
        o_ref[:, i * LANES:(i + 1) * LANES] = t.astype(o_ref.dtype)


PROJ_OUTPUTS_PROMPT = ("q_aug", "kc", "kc_t", "ks_t", "ks_aug", "kw_t", "kw_aug", "gates", "u")
PROJ_OUTPUTS_SAMPLE = ("q_aug", "kc", "ks", "kw", "gates", "u")


def _proj_kernel(x_ref, g_ref, w_ref, qfill_ref, *out_refs, seq_len, prompt):
    x = x_ref[...]
    xn = _rms(x, g_ref[...]).astype(BF16)
    tm = x.shape[0]
    outs = dict(zip(PROJ_OUTPUTS_PROMPT if prompt else PROJ_OUTPUTS_SAMPLE, out_refs))
    col = [0]

    def matmul(width):
        y = _dot(xn, w_ref[:, col[0]:col[0] + width])
        col[0] += width
        return y

    _store_tiles(outs["q_aug"], _expand_heads(
        matmul(Q_COLS), [qfill_ref[:, h * LANES:(h + 1) * LANES] for h in range(N_HEADS)]))
    if prompt:
        pos = (pl.program_id(0) * tm + lax.broadcasted_iota(jnp.int32, (tm, LANES), 0)) % seq_len
        lane = lax.broadcasted_iota(jnp.int32, (tm, LANES), 1)
        blk_lane = SEL_LANE0 + (pos // SLC_BLOCK) % (SLC_CHUNK // SLC_BLOCK)
        k_fill = (jnp.where(lane == HEAD_DIM, (pos // SLC_BLOCK).astype(F32), 0.0)
                  + jnp.where(lane == HEAD_DIM + 1, (pos % SLC_BLOCK).astype(F32), 0.0)
                  + jnp.where(lane == blk_lane, 1.0, 0.0))
        v_fill = jnp.where(lane == HEAD_DIM, 1.0, 0.0)
        y = matmul(KV_COLS)
        outs["kc"][...] = y
        outs["kc_t"][0] = y.T
        for name in ("ks", "kw"):
            y = matmul(KV_COLS)
            outs[name + "_t"][0] = y.T
            _store_tiles(outs[name + "_aug"], _expand_heads(y, [k_fill, k_fill, v_fill, v_fill]))
    else:
        for name in ("kc", "ks", "kw"):
            outs[name][...] = matmul(KV_COLS)
    outs["gates"][...] = _sigmoid(matmul(LANES))
    a = matmul(CONV_CH)
    outs["u"][...] = a * _sigmoid(matmul(CONV_CH))


def _project(x2d, g, w_packed, qfill, tm, seq_len, prompt):
    rows = x2d.shape[0]
    nw = w_packed.shape[1]
    tiles = seq_len // tm if prompt else 1
    widths = {"q_aug": (N_HEADS * LANES, BF16), "kc": (KV_COLS, F32), "ks": (KV_COLS, F32),
              "kw": (KV_COLS, F32), "ks_aug": (4 * LANES, BF16), "kw_aug": (4 * LANES, BF16),
              "gates": (LANES, F32), "u": (CONV_CH, F32)}
    names = PROJ_OUTPUTS_PROMPT if prompt else PROJ_OUTPUTS_SAMPLE
    out_shape, out_specs = [], []
    for name in names:
        if name.endswith("_t"):
            out_shape.append(jax.ShapeDtypeStruct((rows // seq_len, KV_COLS, seq_len), F32))
            out_specs.append(pl.BlockSpec((1, KV_COLS, tm), lambda i: (i // tiles, 0, i % tiles)))
        else:
            width, dt = widths[name]
            out_shape.append(jax.ShapeDtypeStruct((rows, width), dt))
            out_specs.append(pl.BlockSpec((tm, width), lambda i: (i, 0)))
    outs = pl.pallas_call(
        functools.partial(_proj_kernel, seq_len=seq_len, prompt=prompt),
        grid=(rows // tm,),
        in_specs=[
            pl.BlockSpec((tm, D_MODEL), lambda i: (i, 0)),
            pl.BlockSpec((1, D_MODEL), lambda i: (0, 0)),
            pl.BlockSpec((D_MODEL, nw), lambda i: (0, 0), pipeline_mode=pl.Buffered(1)),
            pl.BlockSpec(qfill.shape, lambda i: (0, 0)),
        ],
        out_specs=out_specs,
        out_shape=out_shape,
        compiler_params=pltpu.CompilerParams(
            dimension_semantics=("arbitrary",), vmem_limit_bytes=VMEM_LIMIT),
        name="proj",
    )(x2d, g, w_packed, qfill)
    return dict(zip(names, outs))


HALO = 32


def _conv_ln_swish(u, w_ref, cb_ref, lg_ref, lb_ref, ext_ref, sh_ref):
    tt = u.shape[0]
    ext_ref[HALO:HALO + tt, :] = u
    for r in range(1, 8):
        sh_ref[r] = ext_ref[pl.ds(r, tt + HALO - 8), :]
    w = w_ref[...]
    y = jnp.zeros((tt, CONV_CH), F32) + cb_ref[...]
    for r in range(8):
        for q in range(5 if r == 0 else 4):
            k = 8 * q + r - 2
            if 0 <= k < CONV_W:
                src = ext_ref[8 * q:8 * q + tt, :] if r == 0 else sh_ref[r, 8 * q:8 * q + tt, :]
                y = y + w[k:k + 1, :] * src
    mu = jnp.mean(y, -1, keepdims=True)
    yc = y - mu
    yn = yc * lax.rsqrt(jnp.mean(yc * yc, -1, keepdims=True) + EPS) * lg_ref[...] + lb_ref[...]
    ext_ref[0:HALO, :] = ext_ref[tt:tt + HALO, :]
    return yn * _sigmoid(yn)


def _conv_kernel(st_ref, u_ref, w_ref, cb_ref, lg_ref, lb_ref, o_ref, ext_ref, sh_ref):
    @pl.when(pl.program_id(1) == 0)
    def _():
        ext_ref[0:HALO, :] = st_ref[0]

    o_ref[0] = _conv_ln_swish(u_ref[0], w_ref, cb_ref, lg_ref, lb_ref, ext_ref, sh_ref).astype(o_ref.dtype)


def _conv_params(conv_w, conv_b, ln_g, ln_b):
    row = lambda a: a.reshape(1, CONV_CH)
    return jnp.pad(conv_w, ((0, 1), (0, 0))), row(conv_b), row(ln_g), row(ln_b)


def _conv_scratch(tt):
    return [pltpu.VMEM((HALO + tt, CONV_CH), F32), pltpu.VMEM((8, HALO + tt - 8, CONV_CH), F32)]


def _conv_tail(state32, u, conv_params, tt):
    b, t, _ = u.shape
    return pl.pallas_call(
        _conv_kernel,
        grid=(b, t // tt),
        in_specs=[
            pl.BlockSpec((1, HALO, CONV_CH), lambda i, j: (i, 0, 0)),
            pl.BlockSpec((1, tt, CONV_CH), lambda i, j: (i, j, 0)),
            pl.BlockSpec((HALO, CONV_CH), lambda i, j: (0, 0)),
            pl.BlockSpec((1, CONV_CH), lambda i, j: (0, 0)),
            pl.BlockSpec((1, CONV_CH), lambda i, j: (0, 0)),
            pl.BlockSpec((1, CONV_CH), lambda i, j: (0, 0)),
        ],
        out_specs=pl.BlockSpec((1, tt, CONV_CH), lambda i, j: (i, j, 0)),
        out_shape=jax.ShapeDtypeStruct((b, t, CONV_CH), BF16),
        scratch_shapes=_conv_scratch(tt),
        compiler_params=pltpu.CompilerParams(
            dimension_semantics=("arbitrary", "arbitrary"), vmem_limit_bytes=VMEM_LIMIT),
        name="conv_tail",
    )(state32, u, *conv_params)


def _gelu_tanh(x):
    return 0.5 * x * (1.0 + jnp.tanh(np.sqrt(2.0 / np.pi) * (x + 0.044715 * (x * x * x))))


def _pe_term(pe_ref, w1f_ref, pet):
    for c in range(2):
        t = lax.dot_general(pe_ref[c], w1f_ref[c], (((1,), (0,)), ((), ())),
                            precision=lax.Precision.HIGHEST, preferred_element_type=F32)
        pet[c] = jnp.concatenate([t, t], axis=1)


def _cmp_key_columns(n0, n):
    nidx = n0 + lax.broadcasted_iota(jnp.int32, (n, 2 * LANES), 0)
    cend = nidx * CMP_STRIDE + (CMP_BLOCK - 1)
    lane = lax.broadcasted_iota(jnp.int32, (n, 2 * LANES), 1) % LANES
    return (jnp.where(lane == HEAD_DIM, (cend // SLC_BLOCK).astype(F32), 0.0)
            + jnp.where(lane == HEAD_DIM + 1, (cend % SLC_BLOCK).astype(F32), 0.0))


def _compress_kernel(pt_ref, cache_ref, w1_ref, pe_ref, w1f_ref, w2k_ref, w2v_ref, o_ref,
                     xbuf, pet, sem, tsem, *, pages, n_groups, ppb):
    b = pl.program_id(0)
    gi = pl.program_id(1)
    rows = pages * PAGE_SIZE
    n = rows // CMP_STRIDE

    @pl.when((b == 0) & (gi == 0))
    def _():
        _pe_term(pe_ref, w1f_ref, pet)

    has_next = gi + 1 < n_groups

    def page_copies(p):
        page = pt_ref[b * ppb + gi * pages + p]
        return [pltpu.make_async_copy(cache_ref.at[page, :, pl.ds(c * LANES, LANES)],
                                      xbuf.at[c, pl.ds(p * PAGE_SIZE, PAGE_SIZE)], sem) for c in range(2)]

    def tail_copies():
        page = pt_ref[b * ppb + (gi + 1) * pages]
        return [pltpu.make_async_copy(cache_ref.at[page, pl.ds(0, CMP_STRIDE), pl.ds(c * LANES, LANES)],
                                      xbuf.at[c, pl.ds(rows, CMP_STRIDE)], tsem) for c in range(2)]

    for p in range(pages):
        for cp in page_copies(p):
            cp.start()

    @pl.when(has_next)
    def _():
        for cp in tail_copies():
            cp.start()

    for p in range(pages):
        for cp in page_copies(p):
            cp.wait()

    @pl.when(has_next)
    def _():
        for cp in tail_copies():
            cp.wait()

    @pl.when(jnp.logical_not(has_next))
    def _():
        for c in range(2):
            xbuf[c, rows:rows + CMP_STRIDE, :] = jnp.zeros((CMP_STRIDE, LANES), F32)

    aug = _cmp_key_columns(gi * n, n)
    for c, w2_ref in ((0, w2k_ref), (1, w2v_ref)):
        lhs = jnp.concatenate(
            [xbuf[c, pl.ds(j, n, stride=CMP_STRIDE), :].astype(BF16)
             for j in range(CMP_BLOCK)], axis=1)
        pre = _dot(lhs, w1_ref[c]) + pet[c][0:1, :]
        h = _gelu_tanh(pre).astype(BF16)
        out = _dot(h, w2_ref[...])
        if c == 0:
            o_ref[0, :, 0:2 * LANES] = (out + aug).astype(o_ref.dtype)
        else:
            o_ref[0, :, 2 * LANES:] = out.astype(o_ref.dtype)


def _compress(page_table, cache, w1big, pe8, w1f, w2k, w2v, pages):
    nb, ppb = page_table.shape
    n_groups = ppb // pages
    n = pages * PAGE_SIZE // CMP_STRIDE
    ncols = 2 * LANES + w2v.shape[1]
    once = dict(pipeline_mode=pl.Buffered(1))
    scratch = [
        pltpu.VMEM((2, pages * PAGE_SIZE + CMP_STRIDE, LANES), F32),
        pltpu.VMEM((2, 8, 2 * CMP_HID), F32),
        pltpu.SemaphoreType.DMA(()),
        pltpu.SemaphoreType.DMA(()),
    ]
    grid_spec = pltpu.PrefetchScalarGridSpec(
        num_scalar_prefetch=1,
        grid=(nb, n_groups),
        in_specs=[
            pl.BlockSpec(memory_space=pl.ANY),
            pl.BlockSpec(w1big.shape, lambda i, j, pt: (0, 0, 0), **once),
            pl.BlockSpec(pe8.shape, lambda i, j, pt: (0, 0, 0), **once),
            pl.BlockSpec(w1f.shape, lambda i, j, pt: (0, 0, 0), **once),
            pl.BlockSpec(w2k.shape, lambda i, j, pt: (0, 0), **once),
            pl.BlockSpec(w2v.shape, lambda i, j, pt: (0, 0), **once),
        ],
        out_specs=pl.BlockSpec((1, n, ncols), lambda i, j, pt: (i, j, 0)),
        scratch_shapes=scratch,
    )
    return pl.pallas_call(
        functools.partial(_compress_kernel, pages=pages, n_groups=n_groups, ppb=ppb),
        grid_spec=grid_spec,
        out_shape=jax.ShapeDtypeStruct((nb, n_groups * n, ncols), BF16),
        compiler_params=pltpu.CompilerParams(
            dimension_semantics=("arbitrary", "arbitrary"), vmem_limit_bytes=VMEM_LIMIT),
        name="compress",
    )(page_table.reshape(-1), cache, w1big, pe8, w1f, w2k, w2v)


def _compress_paged_kernel(pt_ref, cache_ref, w1_ref, pe_ref, w1f_ref, w2k_ref, w2v_ref, o_ref,
                           stg, xj, pet, sem, tsem, *, pages, n_groups, ppb, n_steps):
    step = pl.program_id(0) * n_groups + pl.program_id(1)
    chunks_per_page = PAGE_SIZE // CMP_STRIDE
    n = pages * chunks_per_page
    pitch = n + 8
    hid = 2 * CMP_HID

    def page_copies(s, sl):
        first = (s // n_groups) * ppb + (s % n_groups) * pages
        return [pltpu.make_async_copy(cache_ref.at[pt_ref[first + p]], stg.at[sl, p], sem.at[sl])
                for p in range(pages)]

    def tail_copy(s, sl):
        nxt = (s // n_groups) * ppb + (s % n_groups + 1) * pages
        return pltpu.make_async_copy(cache_ref.at[pt_ref[nxt]], stg.at[sl, pages], tsem.at[sl])

    def has_lookahead(s):
        return s % n_groups + 1 < n_groups

    def start_fetch(s, sl):
        for cp in page_copies(s, sl):
            cp.start()

        @pl.when(has_lookahead(s))
        def _():
            tail_copy(s, sl).start()

    def wait_fetch(s, sl):
        for cp in page_copies(s, sl):
            cp.wait()

        @pl.when(has_lookahead(s))
        def _():
            tail_copy(s, sl).wait()

    def relayout(s, sl):
        def scatter_tokens(x_t, chunk0, n_tok, keep=None):
            for c in range(2):
                xt = x_t[c * LANES:(c + 1) * LANES, :].T
                for i in range(n_tok // 8):
                    rows8 = xt[8 * i:8 * i + 8, :]
                    if keep is not None:
                        rows8 = jnp.where(keep, rows8, 0.0)
                    first = (8 * (i % 2)) * pitch + chunk0 + i // 2
                    xj[sl, c, pl.ds(first, 8, stride=pitch), :] = rows8

        for c in range(2):
            for j in range(CMP_STRIDE):
                xj[sl, c, j * pitch + n:(j + 1) * pitch, :] = jnp.zeros((pitch - n, LANES), F32)
        for p in range(pages):
            scatter_tokens(stg[sl, p], p * chunks_per_page, PAGE_SIZE)
        keep = (lax.broadcasted_iota(jnp.int32, (8, LANES), 0) * 0 + jnp.where(has_lookahead(s), 1, 0)) > 0
        scatter_tokens(stg[sl, pages], n, CMP_STRIDE, keep)

    def mlp(sl):
        aug = _cmp_key_columns(pl.program_id(1) * n, n)
        for c, w2_ref in ((0, w2k_ref), (1, w2v_ref)):
            lhs = jnp.concatenate(
                [xj[sl, c, j * pitch:(j + 1) * pitch, :].astype(BF16) for j in range(CMP_STRIDE)], axis=1)
            res = _dot(lhs, w1_ref[c])
            second_half_next = pltpu.roll(res[:, hid:], pitch - 1, axis=0)
            pre = (res[:, :hid] + second_half_next)[0:n] + pet[c][0:1, :]
            h = _gelu_tanh(pre).astype(BF16)
            out = _dot(h, w2_ref[...])
            if c == 0:
                o_ref[0, :, 0:2 * LANES] = (out + aug).astype(o_ref.dtype)
            else:
                o_ref[0, :, 2 * LANES:] = out.astype(o_ref.dtype)

    @pl.when(step == 0)
    def _():
        _pe_term(pe_ref, w1f_ref, pet)
        for sl in range(2):
            stg[sl, pages] = jnp.zeros((KV_COLS, PAGE_SIZE), F32)
        start_fetch(0, 0)
        if n_steps > 1:
            start_fetch(1, 1)
        wait_fetch(0, 0)
        relayout(0, 0)

    for par in range(2):
        @pl.when(step % 2 == par)
        def _(par=par):
            @pl.when(step + 1 < n_steps)
            def _():
                wait_fetch(step + 1, 1 - par)

            @pl.when(step + 2 < n_steps)
            def _():
                start_fetch(step + 2, par)

            relayout(step + 1, 1 - par)
            mlp(par)


def _compress_paged(page_table, cache_t, w1cat, pe8, w1f, w2k, w2v, pages):
    nb, ppb = page_table.shape
    n_groups = ppb // pages
    n = pages * PAGE_SIZE // CMP_STRIDE
    ncols = 2 * LANES + w2v.shape[1]
    once = dict(pipeline_mode=pl.Buffered(1))
    grid_spec = pltpu.PrefetchScalarGridSpec(
        num_scalar_prefetch=1,
        grid=(nb, n_groups),
        in_specs=[
            pl.BlockSpec(memory_space=pl.ANY),
            pl.BlockSpec(w1cat.shape, lambda i, j, pt: (0, 0, 0), **once),
            pl.BlockSpec(pe8.shape, lambda i, j, pt: (0, 0, 0), **once),
            pl.BlockSpec(w1f.shape, lambda i, j, pt: (0, 0, 0), **once),
            pl.BlockSpec(w2k.shape, lambda i, j, pt: (0, 0), **once),
            pl.BlockSpec(w2v.shape, lambda i, j, pt: (0, 0), **once),
        ],
        out_specs=pl.BlockSpec((1, n, ncols), lambda i, j, pt: (i, j, 0)),
        scratch_shapes=[
            pltpu.VMEM((2, pages + 1, KV_COLS, PAGE_SIZE), F32),
            pltpu.VMEM((2, 2, CMP_STRIDE * (n + 8), LANES), F32),
            pltpu.VMEM((2, 8, 2 * CMP_HID), F32),
            pltpu.SemaphoreType.DMA((2,)),
            pltpu.SemaphoreType.DMA((2,)),
        ],
    )
    return pl.pallas_call(
        functools.partial(_compress_paged_kernel, pages=pages, n_groups=n_groups, ppb=ppb,
                          n_steps=nb * n_groups),
        grid_spec=grid_spec,
        out_shape=jax.ShapeDtypeStruct((nb, n_groups * n, ncols), BF16),
        compiler_params=pltpu.CompilerParams(
            dimension_semantics=("arbitrary", "arbitrary"), vmem_limit_bytes=VMEM_LIMIT),
        name="compress_paged",
    )(page_table.reshape(-1), cache_t, w1cat, pe8, w1f, w2k, w2v)


def _compress_weights(w_cmp1, w_cmp2, pe_cmp):
    w1 = w_cmp1
    z = jnp.zeros_like(w1)
    w1big = jnp.stack([jnp.concatenate([w1, z], -1), jnp.concatenate([z, w1], -1)], axis=2)
    w1cat = jnp.concatenate([w1big[:, :CMP_STRIDE], w1big[:, CMP_STRIDE:]], axis=-1)
    w1cat = w1cat.reshape(2, CMP_STRIDE * 2 * HEAD_DIM, 4 * CMP_HID).astype(BF16)
    w1big = w1big.reshape(2, CMP_BLOCK * 2 * HEAD_DIM, 2 * CMP_HID).astype(BF16)
    pe8 = jnp.broadcast_to(pe_cmp.reshape(2, 1, CMP_BLOCK * HEAD_DIM), (2, 8, CMP_BLOCK * HEAD_DIM))
    w1f = w1.reshape(2, CMP_BLOCK * HEAD_DIM, CMP_HID)

    def w2_layout(w2):
        out = jnp.zeros((2, CMP_HID, 2, LANES), F32)
        for h in range(2):
            out = out.at[h, :, h, 0:HEAD_DIM].set(w2)
        return out.reshape(2 * CMP_HID, 2 * LANES).astype(BF16)

    return w1big, w1cat, (pe8, w1f, w2_layout(w_cmp2[0]), w2_layout(w_cmp2[1]))


SLC_CHUNK = 1024
SEL_LANE0 = HEAD_DIM + 2
WIN_SPAN = WINDOW + Q_TILE


def _nsa_prompt_kernel(q_ref, gt_ref, cmp_ref, ks_ref, kw_ref, cov_ref, o_ref):
    i = pl.program_id(1)
    qs = i * Q_TILE
    q = q_ref[0]
    gates = gt_ref[0]
    qpos_t = qs + lax.broadcasted_iota(jnp.int32, (Q_TILE, 1), 0)
    qpos_gt = jnp.concatenate([qpos_t] * GROUP, axis=0)
    n_cmp_pad = cmp_ref.shape[1]
    cend = lax.broadcasted_iota(jnp.int32, (1, n_cmp_pad), 1) * CMP_STRIDE + (CMP_BLOCK - 1)
    jl = lax.broadcasted_iota(jnp.int32, (1, LANES), 1)
    cur = qpos_t // SLC_BLOCK
    forced = (jl == 0) | (jl == cur) | (jl == cur - 1)
    n_sel = ks_ref.shape[1] // SLC_BLOCK
    jrow = lax.broadcasted_iota(jnp.int32, (n_sel, Q_TILE), 0)

    kend = qs + Q_TILE
    woff = pl.multiple_of(jnp.maximum(kend - WIN_SPAN, 0), Q_TILE)
    d_w = qpos_t - (woff + lax.broadcasted_iota(jnp.int32, (1, WIN_SPAN), 1))
    bias_w = jnp.tile(jnp.where((d_w >= 0) & (d_w < WINDOW), 0.0, NEG), (GROUP, 1))

    qks, o_cmps, sel_negs, a_wins = [], [], [], []
    for k in range(N_KV_HEADS):
        qk = jnp.concatenate(
            [q[:, (GROUP * k + g) * LANES:(GROUP * k + g + 1) * LANES] for g in range(GROUP)], axis=0)

        s = _dot_nt(qk, kw_ref[0, pl.ds(woff, WIN_SPAN), k * LANES:(k + 1) * LANES]) + bias_w
        p = jnp.exp(s - jnp.max(s, -1, keepdims=True)).astype(BF16)
        a_wins.append(_dot(p, kw_ref[0, pl.ds(woff, WIN_SPAN), (2 + k) * LANES:(3 + k) * LANES]))

        kc = cmp_ref[0, :, k * LANES:(k + 1) * LANES]
        vc = cmp_ref[0, :, (2 + k) * LANES:(3 + k) * LANES]
        valid = qpos_gt >= cend
        s = jnp.where(valid, _dot_nt(qk, kc), NEG)
        e = jnp.where(valid, jnp.exp(s - jnp.max(s, -1, keepdims=True)), 0.0)
        p = e / jnp.maximum(jnp.sum(e, -1, keepdims=True), 1e-30)
        o_cmp = _dot(p.astype(BF16), vc)

        psum = p[0:Q_TILE]
        for g in range(1, GROUP):
            psum = psum + p[g * Q_TILE:(g + 1) * Q_TILE]
        p_hi = psum.astype(BF16)
        p_lo = (psum - p_hi.astype(F32)).astype(BF16)
        imp = _dot(p_hi, cov_ref[...]) + _dot(p_lo, cov_ref[...])
        score = (jnp.where(jl * SLC_BLOCK <= qpos_t, imp, -1.0)
                 + jnp.where(forced, FORCE_BONUS, 0.0))
        sc_t = score.T[0:n_sel]
        cnt = jnp.zeros((n_sel, Q_TILE), jnp.int32)
        for jp in range(n_sel):
            row = sc_t[jp:jp + 1, :]
            tie = jnp.where(jrow > jp, 1, 0)
            cnt = cnt + jnp.where(row > sc_t, 1, jnp.where(row == sc_t, tie, 0))
        sel_t = jnp.where(cnt < TOP_K, 1.0, 0.0)
        sel_t = jnp.concatenate([sel_t, jnp.zeros((LANES - n_sel, Q_TILE), F32)], axis=0)
        qks.append(qk)
        o_cmps.append(o_cmp)
        sel_negs.append(jnp.where(sel_t.T > 0.5, 0.0, NEG))

    rows = GROUP * Q_TILE

    blocks_per_chunk = SLC_CHUNK // SLC_BLOCK
    in_sel_lanes = (jl >= SEL_LANE0) & (jl < SEL_LANE0 + blocks_per_chunk)
    q_f32 = [[q[:, (GROUP * k + g) * LANES:(GROUP * k + g + 1) * LANES].astype(F32) for g in range(GROUP)]
             for k in range(N_KV_HEADS)]

    def slc_chunk(c, carry, bias):
        off = pl.multiple_of(c * SLC_CHUNK, SLC_CHUNK)
        out = []
        for k in range(N_KV_HEADS):
            m, acc = carry[k]
            mask_cols = jnp.where(
                in_sel_lanes, pltpu.roll(sel_negs[k], SEL_LANE0 - c * blocks_per_chunk, axis=1), 0.0)
            qx = jnp.concatenate([(q_f32[k][g] + mask_cols).astype(BF16) for g in range(GROUP)], axis=0)
            s = _dot_nt(qx, ks_ref[0, pl.ds(off, SLC_CHUNK), k * LANES:(k + 1) * LANES])
            if bias is not None:
                s = s + bias
            m_new = jnp.maximum(m, jnp.max(s, -1, keepdims=True))
            p = jnp.exp(s - m_new).astype(BF16)
            acc = jnp.exp(m - m_new) * acc + _dot(
                p, ks_ref[0, pl.ds(off, SLC_CHUNK), (2 + k) * LANES:(3 + k) * LANES])
            out.append((m_new, acc))
        return tuple(out)

    init = tuple((jnp.full((rows, 1), 0.1 * NEG, F32), jnp.zeros((rows, LANES), F32))
                 for _ in range(N_KV_HEADS))
    last = (kend - 1) // SLC_CHUNK
    slc = lax.fori_loop(0, last, lambda c, carry: slc_chunk(c, carry, None), init)
    kpos_last = last * SLC_CHUNK + lax.broadcasted_iota(jnp.int32, (1, SLC_CHUNK), 1)
    slc = slc_chunk(last, slc, jnp.tile(jnp.where(kpos_last <= qpos_t, 0.0, NEG), (GROUP, 1)))

    for k in range(N_KV_HEADS):
        a_win = a_wins[k]
        a_slc = slc[k][1]
        o_cmp = o_cmps[k]
        heads = []
        for g in range(GROUP):
            h = GROUP * k + g
            rs = slice(g * Q_TILE, (g + 1) * Q_TILE)
            g_slc = gates[:, N_HEADS + h:N_HEADS + h + 1] / jnp.maximum(
                a_slc[rs][:, HEAD_DIM:HEAD_DIM + 1], 1e-30)
            g_win = gates[:, 2 * N_HEADS + h:2 * N_HEADS + h + 1] / jnp.maximum(
                a_win[rs][:, HEAD_DIM:HEAD_DIM + 1], 1e-30)
            heads.append(gates[:, h:h + 1] * o_cmp[rs] + g_slc * a_slc[rs] + g_win * a_win[rs])
        low = lax.broadcasted_iota(jnp.int32, (Q_TILE, LANES), 1) < HEAD_DIM
        for pair in range(GROUP // 2):
            both = jnp.where(low, heads[2 * pair], pltpu.roll(heads[2 * pair + 1], HEAD_DIM, axis=1))
            col = (GROUP // 2 * k + pair) * LANES
            o_ref[0, :, col:col + LANES] = both.astype(o_ref.dtype)


def _nsa_prompt(q_aug, gates, cmp, ks_aug, kw_aug, cover):
    b, t, _ = q_aug.shape
    return pl.pallas_call(
        _nsa_prompt_kernel,
        grid=(b, t // Q_TILE),
        in_specs=[
            pl.BlockSpec((1, Q_TILE, N_HEADS * LANES), lambda i, j: (i, j, 0)),
            pl.BlockSpec((1, Q_TILE, LANES), lambda i, j: (i, j, 0)),
            pl.BlockSpec((1,) + cmp.shape[1:], lambda i, j: (i, 0, 0)),
            pl.BlockSpec((1,) + ks_aug.shape[1:], lambda i, j: (i, 0, 0)),
            pl.BlockSpec((1,) + kw_aug.shape[1:], lambda i, j: (i, 0, 0)),
            pl.BlockSpec(cover.shape, lambda i, j: (0, 0)),
        ],
        out_specs=pl.BlockSpec((1, Q_TILE, MIX_ATTN), lambda i, j: (i, j, 0)),
        out_shape=jax.ShapeDtypeStruct((b, t, MIX_ATTN), BF16),
        compiler_params=pltpu.CompilerParams(
            dimension_semantics=("arbitrary", "arbitrary"), vmem_limit_bytes=VMEM_LIMIT),
        name="nsa_prompt",
    )(q_aug, gates, cmp, ks_aug, kw_aug, cover)


def _selection_constants(n_cmp_pad, n_cmp, n_sel):
    ci = np.arange(n_cmp_pad)[:, None] * CMP_STRIDE
    sj = np.arange(LANES)[None, :] * SLC_BLOCK
    cover = ((ci + CMP_BLOCK > sj) & (ci < sj + SLC_BLOCK)
             & (np.arange(n_cmp_pad)[:, None] < n_cmp) & (np.arange(LANES)[None, :] < n_sel))
    return jnp.asarray(cover, BF16)


FF_CHUNK = 1024


def _tail_kernel(*refs, n_mix, has_pre):
    x_ref = refs[0]
    mix_refs = refs[1:1 + n_mix]
    pos = 1 + n_mix
    pre_ref = refs[pos] if has_pre else None
    pos += int(has_pre)
    p_ref = refs[pos]
    wmix_refs = refs[pos + 1:pos + 1 + n_mix]
    (gmlp_ref, wup_ref, wdown_ref, gple_ref, wple_ref, wpg_ref, gfin_ref,
     y_ref) = refs[pos + 1 + n_mix:]
    x = x_ref[...]
    for m_ref, w_ref in zip(mix_refs, wmix_refs):
        x = x + _dot(m_ref[...], w_ref[...])
    if has_pre:
        x = x + pre_ref[...]
    xn = _rms(x, gmlp_ref[...]).astype(BF16)
    acc = jnp.zeros_like(x)
    for c in range(D_FF // FF_CHUNK):
        h = _dot(xn, wup_ref[:, c * FF_CHUNK:(c + 1) * FF_CHUNK])
        h = jnp.square(jnp.maximum(h, 0.0)).astype(BF16)
        acc = acc + _dot(h, wdown_ref[c * FF_CHUNK:(c + 1) * FF_CHUNK, :])
    x = x + acc
    gate = _sigmoid(_dot(_rms(x, gple_ref[...]).astype(BF16), wpg_ref[...]))
    x = x + _dot(p_ref[...].astype(BF16), wple_ref[...]) * gate
    y_ref[...] = _rms(x, gfin_ref[...])


def _layer_tail(x2d, mixes, wmixes, pre, p2d, g_mlp, w_up, w_down, g_ple, w_ple, w_pg, g_final, tm):
    rows = x2d.shape[0]
    row_spec = lambda a: pl.BlockSpec((tm, a.shape[1]), lambda i: (i, 0))
    const = lambda a: pl.BlockSpec(a.shape, lambda i: (0, 0), pipeline_mode=pl.Buffered(1))
    vec = lambda a: a.reshape(1, -1)
    has_pre = pre is not None
    args = [x2d, *mixes] + ([pre] if has_pre else []) + [p2d]
    specs = [row_spec(a) for a in args]
    consts = [*wmixes, vec(g_mlp), w_up, w_down, vec(g_ple), w_ple, w_pg, vec(g_final)]
    return pl.pallas_call(
        functools.partial(_tail_kernel, n_mix=len(mixes), has_pre=has_pre),
        grid=(rows // tm,),
        in_specs=specs + [const(a) for a in consts],
        out_specs=pl.BlockSpec((tm, D_MODEL), lambda i: (i, 0)),
        out_shape=jax.ShapeDtypeStruct((rows, D_MODEL), F32),
        compiler_params=pltpu.CompilerParams(
            dimension_semantics=("arbitrary",), vmem_limit_bytes=VMEM_LIMIT),
        name="layer_tail",
    )(*args, *consts)


def _proj_weights(w_in):
    c_q = Q_COLS
    c_gt = Q_COLS + 3 * KV_COLS
    w_packed = jnp.concatenate([
        w_in[:, :c_q] * (HEAD_DIM ** -0.5), w_in[:, c_q:c_gt],
        jnp.pad(w_in[:, c_gt:c_gt + GATE_COLS], ((0, 0), (0, LANES - GATE_COLS))),
        w_in[:, c_gt + GATE_COLS:]], axis=1).astype(BF16)
    qfill = np.zeros((1, N_HEADS * LANES), np.float32)
    for h, slope in enumerate(_slopes()):
        qfill[0, h * LANES + HEAD_DIM] = SLC_BLOCK * slope
        qfill[0, h * LANES + HEAD_DIM + 1] = slope
    return w_packed, jnp.asarray(qfill)


TQ_PAD = 8
SEL_LANES = 384
BIG = 1e9


def _masked_softmax(s, valid):
    s = jnp.where(valid, s, NEG)
    e = jnp.where(valid, jnp.exp(s - jnp.max(s, -1, keepdims=True)), 0.0)
    return e / jnp.maximum(jnp.sum(e, -1, keepdims=True), 1e-30)


def _rows_gt(x, k, width):
    return jnp.concatenate(
        [x[:, (GROUP * k + g) * LANES:(GROUP * k + g) * LANES + width] for g in range(GROUP)], axis=0)


def _sample_select_kernel(q_ref, cmp_ref, cov_ref, ocmp_ref, idx_ref, *, past_len, n_sel):
    rows = GROUP * TQ_PAD
    t_gt = lax.broadcasted_iota(jnp.int32, (rows, 1), 0) % TQ_PAD
    n_cmp_pad = cmp_ref.shape[1]
    cend = lax.broadcasted_iota(jnp.int32, (1, n_cmp_pad), 1) * CMP_STRIDE + (CMP_BLOCK - 1)
    qpos_t = past_len + lax.broadcasted_iota(jnp.int32, (TQ_PAD, 1), 0)
    jl = lax.broadcasted_iota(jnp.int32, (1, SEL_LANES), 1)
    cur = qpos_t // SLC_BLOCK
    forced = (jl == 0) | (jl == cur) | (jl == cur - 1)
    scores = []
    n_seq = q_ref.shape[0]
    for bb in range(n_seq):
        q = q_ref[bb].astype(F32)
        for k in range(N_KV_HEADS):
            qk = _rows_gt(q, k, LANES).astype(BF16)
            kc = cmp_ref[bb, :, k * LANES:(k + 1) * LANES]
            vc = cmp_ref[bb, :, (2 + k) * LANES:(3 + k) * LANES]
            p = _masked_softmax(_dot_nt(qk, kc), (past_len + t_gt) >= cend)
            ocmp_ref[bb, k] = _dot(p.astype(BF16), vc)
            psum = p[0:TQ_PAD]
            for g in range(1, GROUP):
                psum = psum + p[g * TQ_PAD:(g + 1) * TQ_PAD]
            p_hi = psum.astype(BF16)
            p_lo = (psum - p_hi.astype(F32)).astype(BF16)
            imp = _dot(p_hi, cov_ref[...]) + _dot(p_lo, cov_ref[...])
            score = (jnp.where(jl * SLC_BLOCK <= qpos_t, imp, -1.0)
                     + jnp.where(forced, FORCE_BONUS, 0.0))
            scores.append(jnp.where(jl < n_sel, score, -BIG))
    sc = jnp.concatenate(scores, axis=0)
    lane = lax.broadcasted_iota(jnp.int32, sc.shape, 1).astype(F32)
    out_lane = lax.broadcasted_iota(jnp.int32, (sc.shape[0], LANES), 1)
    picked = jnp.zeros((sc.shape[0], LANES), F32)
    for s in range(TOP_K):
        m = jnp.max(sc, -1, keepdims=True)
        am = jnp.min(jnp.where(sc == m, lane, BIG), -1, keepdims=True)
        picked = jnp.where(out_lane == s, am, picked)
        sc = jnp.where(lane == am, -2.0 * BIG, sc)
    per_seq = N_KV_HEADS * TQ_PAD
    for bb in range(n_seq):
        idx_ref[bb] = picked[bb * per_seq:(bb + 1) * per_seq].astype(jnp.int32)


SELECT_SEQS = 4


def _sample_select(q_s, cmp_s, cover_s, past_len, n_sel):
    db = q_s.shape[0]
    ns = SELECT_SEQS
    return pl.pallas_call(
        functools.partial(_sample_select_kernel, past_len=past_len, n_sel=n_sel),
        grid=(db // ns,),
        in_specs=[
            pl.BlockSpec((ns,) + q_s.shape[1:], lambda i: (i, 0, 0)),
            pl.BlockSpec((ns,) + cmp_s.shape[1:], lambda i: (i, 0, 0)),
            pl.BlockSpec(cover_s.shape, lambda i: (0, 0)),
        ],
        out_specs=[
            pl.BlockSpec((ns, N_KV_HEADS, GROUP * TQ_PAD, LANES), lambda i: (i, 0, 0, 0)),
            pl.BlockSpec((ns, N_KV_HEADS * TQ_PAD, LANES), lambda i: (i, 0, 0)),
        ],
        out_shape=[
            jax.ShapeDtypeStruct((db, N_KV_HEADS, GROUP * TQ_PAD, LANES), F32),
            jax.ShapeDtypeStruct((db, N_KV_HEADS * TQ_PAD, LANES), jnp.int32),
        ],
        compiler_params=pltpu.CompilerParams(
            dimension_semantics=("arbitrary",), vmem_limit_bytes=VMEM_LIMIT),
        name="sample_select",
    )(q_s, cmp_s, cover_s)


def _joint_softmax(s_a, valid_a, s_b, valid_b):
    s_a = jnp.where(valid_a, s_a, NEG)
    s_b = jnp.where(valid_b, s_b, NEG)
    m = jnp.maximum(jnp.max(s_a, -1, keepdims=True), jnp.max(s_b, -1, keepdims=True))
    e_a = jnp.where(valid_a, jnp.exp(s_a - m), 0.0)
    e_b = jnp.where(valid_b, jnp.exp(s_b - m), 0.0)
    inv = 1.0 / jnp.maximum(jnp.sum(e_a, -1, keepdims=True) + jnp.sum(e_b, -1, keepdims=True), 1e-30)
    return e_a * inv, e_b * inv


def _sample_attn_kernel(pt_ref, idx_ref, q_ref, gt_ref, ocmp_ref, ksn_ref, kwn_ref, slc_ref, win_ref,
                        wout_ref, o_ref, kbuf, vbuf, sem, *, past_len, tq, ppb):
    b = pl.program_id(0)
    n_past_blocks = past_len // SLC_BLOCK
    bpp = PAGE_SIZE // SLC_BLOCK
    rows = GROUP * TQ_PAD

    slot = b % 2

    def sel_index(k, t, s, bb=b):
        return idx_ref[((bb * N_KV_HEADS + k) * TQ_PAD + t) * TOP_K + s]

    def block_copies(bb, sl):
        cps = []
        for k in range(N_KV_HEADS):
            for t in range(tq):
                for s in range(TOP_K):
                    blk = jnp.minimum(sel_index(k, t, s, bb), n_past_blocks - 1)
                    page = pt_ref[bb * ppb + blk // bpp]
                    dst = pl.ds(s * PAGE_SIZE, PAGE_SIZE)
                    cps.append(pltpu.make_async_copy(
                        slc_ref.at[page, pl.ds(k * HEAD_DIM, HEAD_DIM), :], kbuf.at[sl, k, t, :, dst],
                        sem.at[sl]))
                    cps.append(pltpu.make_async_copy(
                        slc_ref.at[page, pl.ds((N_KV_HEADS + k) * HEAD_DIM, HEAD_DIM), :],
                        vbuf.at[sl, k, t, :, dst], sem.at[sl]))
        return cps

    @pl.when(b == 0)
    def _():
        for cp in block_copies(b, slot):
            cp.start()

    @pl.when(b + 1 < pl.num_programs(0))
    def _():
        for cp in block_copies(b + 1, 1 - slot):
            cp.start()

    q = q_ref[0].astype(F32)
    gates = gt_ref[0]
    t_gt = lax.broadcasted_iota(jnp.int32, (rows, 1), 0) % TQ_PAD
    g_gt = lax.broadcasted_iota(jnp.int32, (rows, 1), 0) // TQ_PAD
    qpos = past_len + t_gt
    lane_pg = lax.broadcasted_iota(jnp.int32, (1, PAGE_SIZE), 1)
    new_pos = past_len + lax.broadcasted_iota(jnp.int32, (1, TQ_PAD), 1)
    win_len = win_ref.shape[2]
    win_pos = past_len - win_len + lax.broadcasted_iota(jnp.int32, (1, win_len), 1)
    slopes = _slopes()

    for cp in block_copies(b, slot):
        cp.wait()

    acc = jnp.zeros((TQ_PAD, D_MODEL), F32)
    for k in range(N_KV_HEADS):
        qk = _rows_gt(q, k, HEAD_DIM).astype(BF16)
        slope = jnp.zeros((rows, 1), F32)
        for g in range(GROUP):
            slope = jnp.where(g_gt == g, float(slopes[GROUP * k + g]), slope)
        ks_new = ksn_ref[0, :, k * HEAD_DIM:(k + 1) * HEAD_DIM].astype(BF16)
        vs_new = ksn_ref[0, :, (N_KV_HEADS + k) * HEAD_DIM:(N_KV_HEADS + k + 1) * HEAD_DIM].astype(BF16)
        kw_new = kwn_ref[0, :, k * HEAD_DIM:(k + 1) * HEAD_DIM].astype(BF16)
        vw_new = kwn_ref[0, :, (N_KV_HEADS + k) * HEAD_DIM:(N_KV_HEADS + k + 1) * HEAD_DIM].astype(BF16)

        kpos, chosen, owner, s_parts = [], [], [], []
        has_new = jnp.zeros((rows, 1), jnp.int32)
        for t in range(tq):
            n_new = 0
            for s in range(TOP_K):
                blk = sel_index(k, t, s)
                is_past = blk < n_past_blocks
                page_blk = jnp.minimum(blk, n_past_blocks - 1)
                kpos.append((page_blk // bpp) * PAGE_SIZE + lane_pg)
                half = jnp.where(is_past, page_blk % bpp, -1)
                chosen.append(jnp.where((lane_pg // SLC_BLOCK) == half, 1, 0))
                owner.append(jnp.full((1, PAGE_SIZE), t, jnp.int32))
                n_new = n_new + jnp.where(is_past, 0, 1)
            has_new = jnp.where(t_gt == t, n_new, has_new)
            s_parts.append(_dot(qk, kbuf[slot, k, t].astype(BF16)))
        kpos = jnp.concatenate(kpos, axis=1)
        chosen = jnp.concatenate(chosen, axis=1)
        owner = jnp.concatenate(owner, axis=1)
        valid = (t_gt == owner) & (chosen > 0) & (kpos <= qpos)
        valid_new = (new_pos <= qpos) & (has_new > 0) & (t_gt < tq)
        s_past = jnp.concatenate(s_parts, axis=1) - slope * (qpos - kpos).astype(F32)
        s_new = _dot_nt(qk, ks_new) - slope * (qpos - new_pos).astype(F32)
        p_past, p_new = _joint_softmax(s_past, valid, s_new, valid_new)
        p_past = p_past.astype(BF16)
        o_slc = _dot(p_new.astype(BF16), vs_new)
        seg = TOP_K * PAGE_SIZE
        for t in range(tq):
            o_slc = o_slc + _dot_nt(p_past[:, t * seg:(t + 1) * seg], vbuf[slot, k, t].astype(BF16))

        d_w = qpos - win_pos
        d_n = qpos - new_pos
        s_w = _dot(qk, win_ref[0, k * HEAD_DIM:(k + 1) * HEAD_DIM, :].astype(BF16)) - slope * d_w.astype(F32)
        s_n = _dot_nt(qk, kw_new) - slope * d_n.astype(F32)
        p_w, p_n = _joint_softmax(s_w, (d_w >= 0) & (d_w < WINDOW) & (win_pos >= 0),
                                  s_n, (d_n >= 0) & (d_n < WINDOW))
        v_w = win_ref[0, (N_KV_HEADS + k) * HEAD_DIM:(N_KV_HEADS + k + 1) * HEAD_DIM, :].astype(BF16)
        o_win = _dot_nt(p_w.astype(BF16), v_w) + _dot(p_n.astype(BF16), vw_new)

        def gate(branch):
            return jnp.concatenate(
                [gates[:, branch * N_HEADS + GROUP * k + g:branch * N_HEADS + GROUP * k + g + 1]
                 for g in range(GROUP)], axis=0)

        o = (gate(0) * ocmp_ref[0, k][:, 0:HEAD_DIM] + gate(1) * o_slc + gate(2) * o_win).astype(BF16)
        for g in range(GROUP):
            h = GROUP * k + g
            acc = acc + _dot(o[g * TQ_PAD:(g + 1) * TQ_PAD], wout_ref[h * HEAD_DIM:(h + 1) * HEAD_DIM, :])
    o_ref[0] = acc


def _sample_attn(page_table, idx, q_s, gates_s, ocmp, ks_new, kw_new, slc_t, win_t, w_out_attn,
                 past_len, tq):
    db, ppb = page_table.shape
    blk3 = lambda a: pl.BlockSpec((1,) + a.shape[1:], lambda i, pt, ix: (i, 0, 0))
    grid_spec = pltpu.PrefetchScalarGridSpec(
        num_scalar_prefetch=2,
        grid=(db,),
        in_specs=[
            blk3(q_s), blk3(gates_s),
            pl.BlockSpec((1,) + ocmp.shape[1:], lambda i, pt, ix: (i, 0, 0, 0)),
            blk3(ks_new), blk3(kw_new),
            pl.BlockSpec(memory_space=pl.ANY),
            blk3(win_t),
            pl.BlockSpec(w_out_attn.shape, lambda i, pt, ix: (0, 0)),
        ],
        out_specs=pl.BlockSpec((1, TQ_PAD, D_MODEL), lambda i, pt, ix: (i, 0, 0)),
        scratch_shapes=[
            pltpu.VMEM((2, N_KV_HEADS, tq, HEAD_DIM, TOP_K * PAGE_SIZE), F32),
            pltpu.VMEM((2, N_KV_HEADS, tq, HEAD_DIM, TOP_K * PAGE_SIZE), F32),
            pltpu.SemaphoreType.DMA((2,)),
        ],
    )
    return pl.pallas_call(
        functools.partial(_sample_attn_kernel, past_len=past_len, tq=tq, ppb=ppb),
        grid_spec=grid_spec,
        out_shape=jax.ShapeDtypeStruct((db, TQ_PAD, D_MODEL), F32),
        compiler_params=pltpu.CompilerParams(
            dimension_semantics=("arbitrary",), vmem_limit_bytes=VMEM_LIMIT),
        name="sample_attn",
    )(page_table.reshape(-1), idx, q_s, gates_s, ocmp, ks_new, kw_new, slc_t, win_t, w_out_attn)


def _pages_feature_major(cache):
    return jnp.transpose(cache, (0, 2, 3, 4, 1)).reshape(cache.shape[0], KV_COLS, cache.shape[1])


def kernel(x_prompt, x_sample, p_prompt, p_sample, cache_cmp_kv, cache_slc_kv, cache_win_kv, state_conv, page_table, g_attn, w_in, w_cmp1, w_cmp2, pe_cmp, conv_w, conv_b, ln_conv_g, ln_conv_b, w_out, g_mlp, w_up, w_down, g_ple, w_ple, w_ple_gate, g_final):
    b, t, _ = x_prompt.shape
    db, tq, _ = x_sample.shape
    win_buf = cache_win_kv.shape[2]
    kv5 = lambda a, nb, nt: a.reshape(1, nb, nt, 2, N_KV_HEADS, HEAD_DIM)

    w_out_attn = w_out[0][:MIX_ATTN].astype(BF16)
    w_out_conv = w_out[0][MIX_ATTN:].astype(BF16)
    tail_w = (g_mlp[0], w_up[0].astype(BF16), w_down[0].astype(BF16), g_ple[0],
              w_ple[0].astype(BF16), w_ple_gate[0].astype(BF16), g_final)
    g_row = g_attn[0].reshape(1, D_MODEL)

    w_packed, qfill = _proj_weights(w_in[0])
    conv_params = _conv_params(conv_w[0], conv_b[0], ln_conv_g[0], ln_conv_b[0])
    po = _project(x_prompt.reshape(b * t, D_MODEL), g_row, w_packed, qfill, 512, t, prompt=True)
    u_p = po["u"].reshape(b, t, CONV_CH)
    conv_p = _conv_tail(jnp.zeros((b, HALO, CONV_CH), F32), u_p, conv_params, 512)
    ppb = t // PAGE_SIZE
    w1big, w1cat, cw = _compress_weights(w_cmp1[0], w_cmp2[0], pe_cmp[0])
    cmp_p = _compress(jnp.arange(b * ppb, dtype=jnp.int32).reshape(b, ppb),
                      po["kc"].reshape(b * ppb, PAGE_SIZE, KV_COLS), w1big, *cw, pages=ppb)
    n_chunk = t // CMP_STRIDE
    cover = _selection_constants(n_chunk, n_chunk - 1, t // SLC_BLOCK)
    attn_p = _nsa_prompt(po["q_aug"].reshape(b, t, -1), po["gates"].reshape(b, t, LANES), cmp_p,
                         po["ks_aug"].reshape(b, t, -1), po["kw_aug"].reshape(b, t, -1), cover)
    y_prompt = _layer_tail(x_prompt.reshape(b * t, D_MODEL),
                           [attn_p.reshape(b * t, -1), conv_p.reshape(b * t, CONV_CH)],
                           [w_out_attn, w_out_conv], None, p_prompt[0].reshape(b * t, PLE_DIM),
                           *tail_w, tm=512).reshape(b, t, D_MODEL)

    xs = jnp.pad(x_sample, ((0, 0), (0, TQ_PAD - tq), (0, 0))).reshape(db * TQ_PAD, D_MODEL)
    so = _project(xs, g_row, w_packed, qfill, db * TQ_PAD, TQ_PAD, prompt=False)
    rs = lambda a: a.reshape(db, TQ_PAD, -1)
    u_s = rs(so["u"])
    kc_s, ks_s, kw_s = rs(so["kc"])[:, :tq], rs(so["ks"])[:, :tq], rs(so["kw"])[:, :tq]
    state32 = jnp.pad(state_conv[0], ((0, 0), (HALO - (CONV_W - 1), 0), (0, 0)))
    conv_s = _conv_tail(state32, u_s, conv_params, TQ_PAD)

    past_len = page_table.shape[1] * PAGE_SIZE
    assert (past_len + tq) // CMP_STRIDE == past_len // CMP_STRIDE and tq <= TQ_PAD
    n_chunk_s = past_len // CMP_STRIDE
    n_sel_s = past_len // SLC_BLOCK + 1
    cmp_s = _compress_paged(page_table, _pages_feature_major(cache_cmp_kv[0]), w1cat, *cw, pages=64)
    ci = np.arange(n_chunk_s)[:, None] * CMP_STRIDE
    sj = np.arange(SEL_LANES)[None, :] * SLC_BLOCK
    cover_s = jnp.asarray((ci + CMP_BLOCK > sj) & (ci < sj + SLC_BLOCK)
                          & (np.arange(n_chunk_s)[:, None] < n_chunk_s - 1)
                          & (np.arange(SEL_LANES)[None, :] < n_sel_s), BF16)
    q_s = rs(so["q_aug"])
    ocmp_s, idx_s = _sample_select(q_s, cmp_s, cover_s, past_len, n_sel_s)
    pre_s = _sample_attn(page_table, idx_s[:, :, :TOP_K].reshape(-1), q_s, rs(so["gates"]), ocmp_s,
                         rs(so["ks"]), rs(so["kw"]), _pages_feature_major(cache_slc_kv[0]),
                         _pages_feature_major(cache_win_kv[0]), w_out_attn,
                         past_len, tq)
    p_s = jnp.pad(p_sample[0], ((0, 0), (0, TQ_PAD - tq), (0, 0))).reshape(db * TQ_PAD, PLE_DIM)
    y_sample = _layer_tail(xs, [conv_s.reshape(db * TQ_PAD, CONV_CH)], [w_out_conv],
                           pre_s.reshape(db * TQ_PAD, D_MODEL), p_s, *tail_w,
                           tm=db * TQ_PAD).reshape(db, TQ_PAD, D_MODEL)[:, :tq]

    def token_major(a_t):
        nt = a_t.shape[2]
        return jnp.transpose(a_t.reshape(b, 2, N_KV_HEADS, HEAD_DIM, nt), (0, 4, 1, 2, 3))[None]

    new_win = jnp.concatenate([cache_win_kv[:, :, tq:], kv5(kw_s, db, tq)], 2)
    new_conv_s = jnp.concatenate([state_conv[0], u_s[:, :tq]], 1)[:, -(CONV_W - 1):]
    return (y_prompt, y_sample,
            token_major(po["kc_t"]), token_major(po["ks_t"]), token_major(po["kw_t"][:, :, t - win_buf:]),
            u_p[:, -(CONV_W - 1):][None],
            kv5(kc_s, db, tq), kv5(ks_s, db, tq), new_win, new_conv_s[None])
```

```python
import functools

import numpy as np
import jax
import jax.numpy as jnp
from jax import lax
from jax.experimental import pallas as pl
from jax.experimental.pallas import tpu as pltpu

D_MODEL = 1024
N_HEADS = 8
HEAD_DIM = 64
N_KV_HEADS = 2
GROUP = N_HEADS // N_KV_HEADS
MIX_ATTN = N_HEADS * HEAD_DIM
CONV_CH = D_MODEL - MIX_ATTN
CONV_W = 31
CMP_BLOCK = 32
CMP_STRIDE = 16
CMP_HID = 2 * HEAD_DIM
SLC_BLOCK = 64
TOP_K = 16
WINDOW = 512
PAGE_SIZE = 128
D_FF = 4 * D_MODEL
PLE_DIM = 256
Q_COLS = N_HEADS * HEAD_DIM
KV_COLS = 2 * N_KV_HEADS * HEAD_DIM
GATE_COLS = 3 * N_HEADS
EPS = 1e-6
NEG = -1e30
FORCE_BONUS = 1e4

LANES = 128
Q_TILE = 256
VMEM_LIMIT = 56 * 1024 * 1024

F32 = jnp.float32
BF16 = jnp.bfloat16


def _dot(a, b):
    return jnp.dot(a, b, preferred_element_type=F32)


def _dot_nt(a, b):
    return lax.dot_general(a, b, (((1,), (1,)), ((), ())), preferred_element_type=F32)


def _sigmoid(x):
    return 1.0 / (1.0 + jnp.exp(-x))


def _rms(x, g):
    return x * lax.rsqrt(jnp.mean(x * x, -1, keepdims=True) + EPS) * g


def _slopes():
    return 2.0 ** (-8.0 * np.arange(1, N_HEADS + 1) / N_HEADS)


def _expand_heads(y, fill):
    low = lax.broadcasted_iota(jnp.int32, (y.shape[0], LANES), 1) < HEAD_DIM
    tiles = []
    for i in range(y.shape[1] // HEAD_DIM):
        pair = y[:, (i // 2) * LANES:(i // 2 + 1) * LANES]
        data = pair if i % 2 == 0 else pltpu.roll(pair, HEAD_DIM, axis=1)
        tiles.append(jnp.where(low, data, fill[i]))
    return tiles


def _store_tiles(o_ref, tiles):
    for i, t in enumerate(tiles):
        o_ref[:, i * LANES:(i + 1) * LANES] = t.astype(o_ref.dtype)


PROJ_OUTPUTS_PROMPT = ("q_aug", "kc", "kc_t", "ks_t", "ks_aug", "kw_t", "kw_aug", "gates", "u")
PROJ_OUTPUTS_SAMPLE = ("q_aug", "kc", "ks", "kw", "gates", "u")


def _proj_kernel(x_ref, g_ref, w_ref, qfill_ref, *out_refs, seq_len, prompt):
    x = x_ref[...]
    xn = _rms(x, g_ref[...]).astype(BF16)
    tm = x.shape[0]
    outs = dict(zip(PROJ_OUTPUTS_PROMPT if prompt else PROJ_OUTPUTS_SAMPLE, out_refs))
    col = [0]

    def matmul(width):
        y = _dot(xn, w_ref[:, col[0]:col[0] + width])
        col[0] += width
        return y

    _store_tiles(outs["q_aug"], _expand_heads(
        matmul(Q_COLS), [qfill_ref[:, h * LANES:(h + 1) * LANES] for h in range(N_HEADS)]))
    if prompt:
        pos = (pl.program_id(0) * tm + lax.broadcasted_iota(jnp.int32, (tm, LANES), 0)) % seq_len
        lane = lax.broadcasted_iota(jnp.int32, (tm, LANES), 1)
        blk_lane = SEL_LANE0 + (pos // SLC_BLOCK) % (SLC_CHUNK // SLC_BLOCK)
        k_fill = (jnp.where(lane == HEAD_DIM, (pos // SLC_BLOCK).astype(F32), 0.0)
                  + jnp.where(lane == HEAD_DIM + 1, (pos % SLC_BLOCK).astype(F32), 0.0)
                  + jnp.where(lane == blk_lane, 1.0, 0.0))
        v_fill = jnp.where(lane == HEAD_DIM, 1.0, 0.0)
        y = matmul(KV_COLS)
        outs["kc"][...] = y
        outs["kc_t"][0] = y.T
        for name in ("ks", "kw"):
            y = matmul(KV_COLS)
            outs[name + "_t"][0] = y.T
            _store_tiles(outs[name + "_aug"], _expand_heads(y, [k_fill, k_fill, v_fill, v_fill]))
    else:
        for name in ("kc", "ks", "kw"):
            outs[name][...] = matmul(KV_COLS)
    outs["gates"][...] = _sigmoid(matmul(LANES))
    a = matmul(CONV_CH)
    outs["u"][...] = a * _sigmoid(matmul(CONV_CH))


def _project(x2d, g, w_packed, qfill, tm, seq_len, prompt):
    rows = x2d.shape[0]
    nw = w_packed.shape[1]
    tiles = seq_len // tm if prompt else 1
    widths = {"q_aug": (N_HEADS * LANES, BF16), "kc": (KV_COLS, F32), "ks": (KV_COLS, F32),
              "kw": (KV_COLS, F32), "ks_aug": (4 * LANES, BF16), "kw_aug": (4 * LANES, BF16),
              "gates": (LANES, F32), "u": (CONV_CH, F32)}
    names = PROJ_OUTPUTS_PROMPT if prompt else PROJ_OUTPUTS_SAMPLE
    out_shape, out_specs = [], []
    for name in names:
        if name.endswith("_t"):
            out_shape.append(jax.ShapeDtypeStruct((rows // seq_len, KV_COLS, seq_len), F32))
            out_specs.append(pl.BlockSpec((1, KV_COLS, tm), lambda i: (i // tiles, 0, i % tiles)))
        else:
            width, dt = widths[name]
            out_shape.append(jax.ShapeDtypeStruct((rows, width), dt))
            out_specs.append(pl.BlockSpec((tm, width), lambda i: (i, 0)))
    outs = pl.pallas_call(
        functools.partial(_proj_kernel, seq_len=seq_len, prompt=prompt),
        grid=(rows // tm,),
        in_specs=[
            pl.BlockSpec((tm, D_MODEL), lambda i: (i, 0)),
            pl.BlockSpec((1, D_MODEL), lambda i: (0, 0)),
            pl.BlockSpec((D_MODEL, nw), lambda i: (0, 0), pipeline_mode=pl.Buffered(1)),
            pl.BlockSpec(qfill.shape, lambda i: (0, 0)),
        ],
        out_specs=out_specs,
        out_shape=out_shape,
        compiler_params=pltpu.CompilerParams(
            dimension_semantics=("arbitrary",), vmem_limit_bytes=VMEM_LIMIT),
        name="proj",
    )(x2d, g, w_packed, qfill)
    return dict(zip(names, outs))


HALO = 32


def _conv_ln_swish(u, w_ref, cb_ref, lg_ref, lb_ref, ext_ref, sh_ref):
    tt = u.shape[0]
    ext_ref[HALO:HALO + tt, :] = u
    for r in range(1, 8):
        sh_ref[r] = ext_ref[pl.ds(r, tt + HALO - 8), :]
    w = w_ref[...]
    y = jnp.zeros((tt, CONV_CH), F32) + cb_ref[...]
    for r in range(8):
        for q in range(5 if r == 0 else 4):
            k = 8 * q + r - 2
            if 0 <= k < CONV_W:
                src = ext_ref[8 * q:8 * q + tt, :] if r == 0 else sh_ref[r, 8 * q:8 * q + tt, :]
                y = y + w[k:k + 1, :] * src
    mu = jnp.mean(y, -1, keepdims=True)
    yc = y - mu
    yn = yc * lax.rsqrt(jnp.mean(yc * yc, -1, keepdims=True) + EPS) * lg_ref[...] + lb_ref[...]
    ext_ref[0:HALO, :] = ext_ref[tt:tt + HALO, :]
    return yn * _sigmoid(yn)


def _conv_kernel(st_ref, u_ref, w_ref, cb_ref, lg_ref, lb_ref, o_ref, ext_ref, sh_ref):
    @pl.when(pl.program_id(1) == 0)
    def _():
        ext_ref[0:HALO, :] = st_ref[0]

    o_ref[0] = _conv_ln_swish(u_ref[0], w_ref, cb_ref, lg_ref, lb_ref, ext_ref, sh_ref).astype(o_ref.dtype)


def _conv_params(conv_w, conv_b, ln_g, ln_b):
    row = lambda a: a.reshape(1, CONV_CH)
    return jnp.pad(conv_w, ((0, 1), (0, 0))), row(conv_b), row(ln_g), row(ln_b)


def _conv_scratch(tt):
    return [pltpu.VMEM((HALO + tt, CONV_CH), F32), pltpu.VMEM((8, HALO + tt - 8, CONV_CH), F32)]


def _conv_tail(state32, u, conv_params, tt):
    b, t, _ = u.shape
    return pl.pallas_call(
        _conv_kernel,
        grid=(b, t // tt),
        in_specs=[
            pl.BlockSpec((1, HALO, CONV_CH), lambda i, j: (i, 0, 0)),
            pl.BlockSpec((1, tt, CONV_CH), lambda i, j: (i, j, 0)),
            pl.BlockSpec((HALO, CONV_CH), lambda i, j: (0, 0)),
            pl.BlockSpec((1, CONV_CH), lambda i, j: (0, 0)),
            pl.BlockSpec((1, CONV_CH), lambda i, j: (0, 0)),
            pl.BlockSpec((1, CONV_CH), lambda i, j: (0, 0)),
        ],
        out_specs=pl.BlockSpec((1, tt, CONV_CH), lambda i, j: (i, j, 0)),
        out_shape=jax.ShapeDtypeStruct((b, t, CONV_CH), BF16),
        scratch_shapes=_conv_scratch(tt),
        compiler_params=pltpu.CompilerParams(
            dimension_semantics=("arbitrary", "arbitrary"), vmem_limit_bytes=VMEM_LIMIT),
        name="conv_tail",
    )(state32, u, *conv_params)


def _gelu_tanh(x):
    return 0.5 * x * (1.0 + jnp.tanh(np.sqrt(2.0 / np.pi) * (x + 0.044715 * (x * x * x))))


def _pe_term(pe_ref, w1f_ref, pet):
    for c in range(2):
        t = lax.dot_general(pe_ref[c], w1f_ref[c], (((1,), (0,)), ((), ())),
                            precision=lax.Precision.HIGHEST, preferred_element_type=F32)
        pet[c] = jnp.concatenate([t, t], axis=1)


def _cmp_key_columns(n0, n):
    nidx = n0 + lax.broadcasted_iota(jnp.int32, (n, 2 * LANES), 0)
    cend = nidx * CMP_STRIDE + (CMP_BLOCK - 1)
    lane = lax.broadcasted_iota(jnp.int32, (n, 2 * LANES), 1) % LANES
    return (jnp.where(lane == HEAD_DIM, (cend // SLC_BLOCK).astype(F32), 0.0)
            + jnp.where(lane == HEAD_DIM + 1, (cend % SLC_BLOCK).astype(F32), 0.0))


def _compress_kernel(pt_ref, cache_ref, w1_ref, pe_ref, w1f_ref, w2k_ref, w2v_ref, o_ref,
                     xbuf, pet, sem, tsem, *, pages, n_groups, ppb):
    b = pl.program_id(0)
    gi = pl.program_id(1)
    rows = pages * PAGE_SIZE
    n = rows // CMP_STRIDE

    @pl.when((b == 0) & (gi == 0))
    def _():
        _pe_term(pe_ref, w1f_ref, pet)

    has_next = gi + 1 < n_groups

    def page_copies(p):
        page = pt_ref[b * ppb + gi * pages + p]
        return [pltpu.make_async_copy(cache_ref.at[page, :, pl.ds(c * LANES, LANES)],
                                      xbuf.at[c, pl.ds(p * PAGE_SIZE, PAGE_SIZE)], sem) for c in range(2)]

    def tail_copies():
        page = pt_ref[b * ppb + (gi + 1) * pages]
        return [pltpu.make_async_copy(cache_ref.at[page, pl.ds(0, CMP_STRIDE), pl.ds(c * LANES, LANES)],
                                      xbuf.at[c, pl.ds(rows, CMP_STRIDE)], tsem) for c in range(2)]

    for p in range(pages):
        for cp in page_copies(p):
            cp.start()

    @pl.when(has_next)
    def _():
        for cp in tail_copies():
            cp.start()

    for p in range(pages):
        for cp in page_copies(p):
            cp.wait()

    @pl.when(has_next)
    def _():
        for cp in tail_copies():
            cp.wait()

    @pl.when(jnp.logical_not(has_next))
    def _():
        for c in range(2):
            xbuf[c, rows:rows + CMP_STRIDE, :] = jnp.zeros((CMP_STRIDE, LANES), F32)

    aug = _cmp_key_columns(gi * n, n)
    for c, w2_ref in ((0, w2k_ref), (1, w2v_ref)):
        lhs = jnp.concatenate(
            [xbuf[c, pl.ds(j, n, stride=CMP_STRIDE), :].astype(BF16)
             for j in range(CMP_BLOCK)], axis=1)
        pre = _dot(lhs, w1_ref[c]) + pet[c][0:1, :]
        h = _gelu_tanh(pre).astype(BF16)
        out = _dot(h, w2_ref[...])
        if c == 0:
            o_ref[0, :, 0:2 * LANES] = (out + aug).astype(o_ref.dtype)
        else:
            o_ref[0, :, 2 * LANES:] = out.astype(o_ref.dtype)


def _compress(page_table, cache, w1big, pe8, w1f, w2k, w2v, pages):
    nb, ppb = page_table.shape
    n_groups = ppb // pages
    n = pages * PAGE_SIZE // CMP_STRIDE
    ncols = 2 * LANES + w2v.shape[1]
    once = dict(pipeline_mode=pl.Buffered(1))
    scratch = [
        pltpu.VMEM((2, pages * PAGE_SIZE + CMP_STRIDE, LANES), F32),
        pltpu.VMEM((2, 8, 2 * CMP_HID), F32),
        pltpu.SemaphoreType.DMA(()),
        pltpu.SemaphoreType.DMA(()),
    ]
    grid_spec = pltpu.PrefetchScalarGridSpec(
        num_scalar_prefetch=1,
        grid=(nb, n_groups),
        in_specs=[
            pl.BlockSpec(memory_space=pl.ANY),
            pl.BlockSpec(w1big.shape, lambda i, j, pt: (0, 0, 0), **once),
            pl.BlockSpec(pe8.shape, lambda i, j, pt: (0, 0, 0), **once),
            pl.BlockSpec(w1f.shape, lambda i, j, pt: (0, 0, 0), **once),
            pl.BlockSpec(w2k.shape, lambda i, j, pt: (0, 0), **once),
            pl.BlockSpec(w2v.shape, lambda i, j, pt: (0, 0), **once),
        ],
        out_specs=pl.BlockSpec((1, n, ncols), lambda i, j, pt: (i, j, 0)),
        scratch_shapes=scratch,
    )
    return pl.pallas_call(
        functools.partial(_compress_kernel, pages=pages, n_groups=n_groups, ppb=ppb),
        grid_spec=grid_spec,
        out_shape=jax.ShapeDtypeStruct((nb, n_groups * n, ncols), BF16),
        compiler_params=pltpu.CompilerParams(
            dimension_semantics=("arbitrary", "arbitrary"), vmem_limit_bytes=VMEM_LIMIT),
        name="compress",
    )(page_table.reshape(-1), cache, w1big, pe8, w1f, w2k, w2v)


def _compress_paged_kernel(pt_ref, cache_ref, w1_ref, pe_ref, w1f_ref, w2k_ref, w2v_ref, o_ref,
                           stg, xj, pet, sem, tsem, *, pages, n_groups, ppb, n_steps):
    step = pl.program_id(0) * n_groups + pl.program_id(1)
    chunks_per_page = PAGE_SIZE // CMP_STRIDE
    n = pages * chunks_per_page
    pitch = n + 8
    hid = 2 * CMP_HID

    def page_copies(s, sl):
        first = (s // n_groups) * ppb + (s % n_groups) * pages
        return [pltpu.make_async_copy(cache_ref.at[pt_ref[first + p]], stg.at[sl, p], sem.at[sl])
                for p in range(pages)]

    def tail_copy(s, sl):
        nxt = (s // n_groups) * ppb + (s % n_groups + 1) * pages
        return pltpu.make_async_copy(cache_ref.at[pt_ref[nxt]], stg.at[sl, pages], tsem.at[sl])

    def has_lookahead(s):
        return s % n_groups + 1 < n_groups

    def start_fetch(s, sl):
        for cp in page_copies(s, sl):
            cp.start()

        @pl.when(has_lookahead(s))
        def _():
            tail_copy(s, sl).start()

    def wait_fetch(s, sl):
        for cp in page_copies(s, sl):
            cp.wait()

        @pl.when(has_lookahead(s))
        def _():
            tail_copy(s, sl).wait()

    def relayout(s, sl):
        def scatter_tokens(x_t, chunk0, n_tok, keep=None):
            for c in range(2):
                xt = x_t[c * LANES:(c + 1) * LANES, :].T
                for i in range(n_tok // 8):
                    rows8 = xt[8 * i:8 * i + 8, :]
                    if keep is not None:
                        rows8 = jnp.where(keep, rows8, 0.0)
                    first = (8 * (i % 2)) * pitch + chunk0 + i // 2
                    xj[sl, c, pl.ds(first, 8, stride=pitch), :] = rows8

        for c in range(2):
            for j in range(CMP_STRIDE):
                xj[sl, c, j * pitch + n:(j + 1) * pitch, :] = jnp.zeros((pitch - n, LANES), F32)
        for p in range(pages):
            scatter_tokens(stg[sl, p], p * chunks_per_page, PAGE_SIZE)
        keep = (lax.broadcasted_iota(jnp.int32, (8, LANES), 0) * 0 + jnp.where(has_lookahead(s), 1, 0)) > 0
        scatter_tokens(stg[sl, pages], n, CMP_STRIDE, keep)

    def mlp(sl):
        aug = _cmp_key_columns(pl.program_id(1) * n, n)
        for c, w2_ref in ((0, w2k_ref), (1, w2v_ref)):
            lhs = jnp.concatenate(
                [xj[sl, c, j * pitch:(j + 1) * pitch, :].astype(BF16) for j in range(CMP_STRIDE)], axis=1)
            res = _dot(lhs, w1_ref[c])
            second_half_next = pltpu.roll(res[:, hid:], pitch - 1, axis=0)
            pre = (res[:, :hid] + second_half_next)[0:n] + pet[c][0:1, :]
            h = _gelu_tanh(pre).astype(BF16)
            out = _dot(h, w2_ref[...])
            if c == 0:
                o_ref[0, :, 0:2 * LANES] = (out + aug).astype(o_ref.dtype)
            else:
                o_ref[0, :, 2 * LANES:] = out.astype(o_ref.dtype)

    @pl.when(step == 0)
    def _():
        _pe_term(pe_ref, w1f_ref, pet)
        for sl in range(2):
            stg[sl, pages] = jnp.zeros((KV_COLS, PAGE_SIZE), F32)
        start_fetch(0, 0)
        if n_steps > 1:
            start_fetch(1, 1)
        wait_fetch(0, 0)
        relayout(0, 0)

    for par in range(2):
        @pl.when(step % 2 == par)
        def _(par=par):
            @pl.when(step + 1 < n_steps)
            def _():
                wait_fetch(step + 1, 1 - par)

            @pl.when(step + 2 < n_steps)
            def _():
                start_fetch(step + 2, par)

            relayout(step + 1, 1 - par)
            mlp(par)


def _compress_paged(page_table, cache_t, w1cat, pe8, w1f, w2k, w2v, pages):
    nb, ppb = page_table.shape
    n_groups = ppb // pages
    n = pages * PAGE_SIZE // CMP_STRIDE
    ncols = 2 * LANES + w2v.shape[1]
    once = dict(pipeline_mode=pl.Buffered(1))
    grid_spec = pltpu.PrefetchScalarGridSpec(
        num_scalar_prefetch=1,
        grid=(nb, n_groups),
        in_specs=[
            pl.BlockSpec(memory_space=pl.ANY),
            pl.BlockSpec(w1cat.shape, lambda i, j, pt: (0, 0, 0), **once),
            pl.BlockSpec(pe8.shape, lambda i, j, pt: (0, 0, 0), **once),
            pl.BlockSpec(w1f.shape, lambda i, j, pt: (0, 0, 0), **once),
            pl.BlockSpec(w2k.shape, lambda i, j, pt: (0, 0), **once),
            pl.BlockSpec(w2v.shape, lambda i, j, pt: (0, 0), **once),
        ],
        out_specs=pl.BlockSpec((1, n, ncols), lambda i, j, pt: (i, j, 0)),
        scratch_shapes=[
            pltpu.VMEM((2, pages + 1, KV_COLS, PAGE_SIZE), F32),
            pltpu.VMEM((2, 2, CMP_STRIDE * (n + 8), LANES), F32),
            pltpu.VMEM((2, 8, 2 * CMP_HID), F32),
            pltpu.SemaphoreType.DMA((2,)),
            pltpu.SemaphoreType.DMA((2,)),
        ],
    )
    return pl.pallas_call(
        functools.partial(_compress_paged_kernel, pages=pages, n_groups=n_groups, ppb=ppb,
                          n_steps=nb * n_groups),
        grid_spec=grid_spec,
        out_shape=jax.ShapeDtypeStruct((nb, n_groups * n, ncols), BF16),
        compiler_params=pltpu.CompilerParams(
            dimension_semantics=("arbitrary", "arbitrary"), vmem_limit_bytes=VMEM_LIMIT),
        name="compress_paged",
    )(page_table.reshape(-1), cache_t, w1cat, pe8, w1f, w2k, w2v)


def _compress_weights(w_cmp1, w_cmp2, pe_cmp):
    w1 = w_cmp1
    z = jnp.zeros_like(w1)
    w1big = jnp.stack([jnp.concatenate([w1, z], -1), jnp.concatenate([z, w1], -1)], axis=2)
    w1cat = jnp.concatenate([w1big[:, :CMP_STRIDE], w1big[:, CMP_STRIDE:]], axis=-1)
    w1cat = w1cat.reshape(2, CMP_STRIDE * 2 * HEAD_DIM, 4 * CMP_HID).astype(BF16)
    w1big = w1big.reshape(2, CMP_BLOCK * 2 * HEAD_DIM, 2 * CMP_HID).astype(BF16)
    pe8 = jnp.broadcast_to(pe_cmp.reshape(2, 1, CMP_BLOCK * HEAD_DIM), (2, 8, CMP_BLOCK * HEAD_DIM))
    w1f = w1.reshape(2, CMP_BLOCK * HEAD_DIM, CMP_HID)

    def w2_layout(w2):
        out = jnp.zeros((2, CMP_HID, 2, LANES), F32)
        for h in range(2):
            out = out.at[h, :, h, 0:HEAD_DIM].set(w2)
        return out.reshape(2 * CMP_HID, 2 * LANES).astype(BF16)

    return w1big, w1cat, (pe8, w1f, w2_layout(w_cmp2[0]), w2_layout(w_cmp2[1]))


SLC_CHUNK = 1024
SEL_LANE0 = HEAD_DIM + 2
WIN_SPAN = WINDOW + Q_TILE


def _nsa_prompt_kernel(q_ref, gt_ref, cmp_ref, ks_ref, kw_ref, cov_ref, o_ref):
    i = pl.program_id(1)
    qs = i * Q_TILE
    q = q_ref[0]
    gates = gt_ref[0]
    qpos_t = qs + lax.broadcasted_iota(jnp.int32, (Q_TILE, 1), 0)
    qpos_gt = jnp.concatenate([qpos_t] * GROUP, axis=0)
    n_cmp_pad = cmp_ref.shape[1]
    cend = lax.broadcasted_iota(jnp.int32, (1, n_cmp_pad), 1) * CMP_STRIDE + (CMP_BLOCK - 1)
    jl = lax.broadcasted_iota(jnp.int32, (1, LANES), 1)
    cur = qpos_t // SLC_BLOCK
    forced = (jl == 0) | (jl == cur) | (jl == cur - 1)
    n_sel = ks_ref.shape[1] // SLC_BLOCK
    jrow = lax.broadcasted_iota(jnp.int32, (n_sel, Q_TILE), 0)

    kend = qs + Q_TILE
    woff = pl.multiple_of(jnp.maximum(kend - WIN_SPAN, 0), Q_TILE)
    d_w = qpos_t - (woff + lax.broadcasted_iota(jnp.int32, (1, WIN_SPAN), 1))
    bias_w = jnp.tile(jnp.where((d_w >= 0) & (d_w < WINDOW), 0.0, NEG), (GROUP, 1))

    qks, o_cmps, sel_negs, a_wins = [], [], [], []
    for k in range(N_KV_HEADS):
        qk = jnp.concatenate(
            [q[:, (GROUP * k + g) * LANES:(GROUP * k + g + 1) * LANES] for g in range(GROUP)], axis=0)

        s = _dot_nt(qk, kw_ref[0, pl.ds(woff, WIN_SPAN), k * LANES:(k + 1) * LANES]) + bias_w
        p = jnp.exp(s - jnp.max(s, -1, keepdims=True)).astype(BF16)
        a_wins.append(_dot(p, kw_ref[0, pl.ds(woff, WIN_SPAN), (2 + k) * LANES:(3 + k) * LANES]))

        kc = cmp_ref[0, :, k * LANES:(k + 1) * LANES]
        vc = cmp_ref[0, :, (2 + k) * LANES:(3 + k) * LANES]
        valid = qpos_gt >= cend
        s = jnp.where(valid, _dot_nt(qk, kc), NEG)
        e = jnp.where(valid, jnp.exp(s - jnp.max(s, -1, keepdims=True)), 0.0)
        p = e / jnp.maximum(jnp.sum(e, -1, keepdims=True), 1e-30)
        o_cmp = _dot(p.astype(BF16), vc)

        psum = p[0:Q_TILE]
        for g in range(1, GROUP):
            psum = psum + p[g * Q_TILE:(g + 1) * Q_TILE]
        p_hi = psum.astype(BF16)
        p_lo = (psum - p_hi.astype(F32)).astype(BF16)
        imp = _dot(p_hi, cov_ref[...]) + _dot(p_lo, cov_ref[...])
        score = (jnp.where(jl * SLC_BLOCK <= qpos_t, imp, -1.0)
                 + jnp.where(forced, FORCE_BONUS, 0.0))
        sc_t = score.T[0:n_sel]
        cnt = jnp.zeros((n_sel, Q_TILE), jnp.int32)
        for jp in range(n_sel):
            row = sc_t[jp:jp + 1, :]
            tie = jnp.where(jrow > jp, 1, 0)
            cnt = cnt + jnp.where(row > sc_t, 1, jnp.where(row == sc_t, tie, 0))
        sel_t = jnp.where(cnt < TOP_K, 1.0, 0.0)
        sel_t = jnp.concatenate([sel_t, jnp.zeros((LANES - n_sel, Q_TILE), F32)], axis=0)
        qks.append(qk)
        o_cmps.append(o_cmp)
        sel_negs.append(jnp.where(sel_t.T > 0.5, 0.0, NEG))

    rows = GROUP * Q_TILE

    blocks_per_chunk = SLC_CHUNK // SLC_BLOCK
    in_sel_lanes = (jl >= SEL_LANE0) & (jl < SEL_LANE0 + blocks_per_chunk)
    q_f32 = [[q[:, (GROUP * k + g) * LANES:(GROUP * k + g + 1) * LANES].astype(F32) for g in range(GROUP)]
             for k in range(N_KV_HEADS)]

    def slc_chunk(c, carry, bias):
        off = pl.multiple_of(c * SLC_CHUNK, SLC_CHUNK)
        out = []
        for k in range(N_KV_HEADS):
            m, acc = carry[k]
            mask_cols = jnp.where(
                in_sel_lanes, pltpu.roll(sel_negs[k], SEL_LANE0 - c * blocks_per_chunk, axis=1), 0.0)
            qx = jnp.concatenate([(q_f32[k][g] + mask_cols).astype(BF16) for g in range(GROUP)], axis=0)
            s = _dot_nt(qx, ks_ref[0, pl.ds(off, SLC_CHUNK), k * LANES:(k + 1) * LANES])
            if bias is not None:
                s = s + bias
            m_new = jnp.maximum(m, jnp.max(s, -1, keepdims=True))
            p = jnp.exp(s - m_new).astype(BF16)
            acc = jnp.exp(m - m_new) * acc + _dot(
                p, ks_ref[0, pl.ds(off, SLC_CHUNK), (2 + k) * LANES:(3 + k) * LANES])
            out.append((m_new, acc))
        return tuple(out)

    init = tuple((jnp.full((rows, 1), 0.1 * NEG, F32), jnp.zeros((rows, LANES), F32))
                 for _ in range(N_KV_HEADS))
    last = (kend - 1) // SLC_CHUNK
    slc = lax.fori_loop(0, last, lambda c, carry: slc_chunk(c, carry, None), init)
    kpos_last = last * SLC_CHUNK + lax.broadcasted_iota(jnp.int32, (1, SLC_CHUNK), 1)
    slc = slc_chunk(last, slc, jnp.tile(jnp.where(kpos_last <= qpos_t, 0.0, NEG), (GROUP, 1)))

    for k in range(N_KV_HEADS):
        a_win = a_wins[k]
        a_slc = slc[k][1]
        o_cmp = o_cmps[k]
        heads = []
        for g in range(GROUP):
            h = GROUP * k + g
            rs = slice(g * Q_TILE, (g + 1) * Q_TILE)
            g_slc = gates[:, N_HEADS + h:N_HEADS + h + 1] / jnp.maximum(
                a_slc[rs][:, HEAD_DIM:HEAD_DIM + 1], 1e-30)
            g_win = gates[:, 2 * N_HEADS + h:2 * N_HEADS + h + 1] / jnp.maximum(
                a_win[rs][:, HEAD_DIM:HEAD_DIM + 1], 1e-30)
            heads.append(gates[:, h:h + 1] * o_cmp[rs] + g_slc * a_slc[rs] + g_win * a_win[rs])
        low = lax.broadcasted_iota(jnp.int32, (Q_TILE, LANES), 1) < HEAD_DIM
        for pair in range(GROUP // 2):
            both = jnp.where(low, heads[2 * pair], pltpu.roll(heads[2 * pair + 1], HEAD_DIM, axis=1))
            col = (GROUP // 2 * k + pair) * LANES
            o_ref[0, :, col:col + LANES] = both.astype(o_ref.dtype)


def _nsa_prompt(q_aug, gates, cmp, ks_aug, kw_aug, cover):
    b, t, _ = q_aug.shape
    return pl.pallas_call(
        _nsa_prompt_kernel,
        grid=(b, t // Q_TILE),
        in_specs=[
            pl.BlockSpec((1, Q_TILE, N_HEADS * LANES), lambda i, j: (i, j, 0)),
            pl.BlockSpec((1, Q_TILE, LANES), lambda i, j: (i, j, 0)),
            pl.BlockSpec((1,) + cmp.shape[1:], lambda i, j: (i, 0, 0)),
            pl.BlockSpec((1,) + ks_aug.shape[1:], lambda i, j: (i, 0, 0)),
            pl.BlockSpec((1,) + kw_aug.shape[1:], lambda i, j: (i, 0, 0)),
            pl.BlockSpec(cover.shape, lambda i, j: (0, 0)),
        ],
        out_specs=pl.BlockSpec((1, Q_TILE, MIX_ATTN), lambda i, j: (i, j, 0)),
        out_shape=jax.ShapeDtypeStruct((b, t, MIX_ATTN), BF16),
        compiler_params=pltpu.CompilerParams(
            dimension_semantics=("arbitrary", "arbitrary"), vmem_limit_bytes=VMEM_LIMIT),
        name="nsa_prompt",
    )(q_aug, gates, cmp, ks_aug, kw_aug, cover)


def _selection_constants(n_cmp_pad, n_cmp, n_sel):
    ci = np.arange(n_cmp_pad)[:, None] * CMP_STRIDE
    sj = np.arange(LANES)[None, :] * SLC_BLOCK
    cover = ((ci + CMP_BLOCK > sj) & (ci < sj + SLC_BLOCK)
             & (np.arange(n_cmp_pad)[:, None] < n_cmp) & (np.arange(LANES)[None, :] < n_sel))
    return jnp.asarray(cover, BF16)


FF_CHUNK = 1024


def _tail_kernel(*refs, n_mix, has_pre):
    x_ref = refs[0]
    mix_refs = refs[1:1 + n_mix]
    pos = 1 + n_mix
    pre_ref = refs[pos] if has_pre else None
    pos += int(has_pre)
    p_ref = refs[pos]
    wmix_refs = refs[pos + 1:pos + 1 + n_mix]
    (gmlp_ref, wup_ref, wdown_ref, gple_ref, wple_ref, wpg_ref, gfin_ref,
     y_ref) = refs[pos + 1 + n_mix:]
    x = x_ref[...]
    for m_ref, w_ref in zip(mix_refs, wmix_refs):
        x = x + _dot(m_ref[...], w_ref[...])
    if has_pre:
        x = x + pre_ref[...]
    xn = _rms(x, gmlp_ref[...]).astype(BF16)
    acc = jnp.zeros_like(x)
    for c in range(D_FF // FF_CHUNK):
        h = _dot(xn, wup_ref[:, c * FF_CHUNK:(c + 1) * FF_CHUNK])
        h = jnp.square(jnp.maximum(h, 0.0)).astype(BF16)
        acc = acc + _dot(h, wdown_ref[c * FF_CHUNK:(c + 1) * FF_CHUNK, :])
    x = x + acc
    gate = _sigmoid(_dot(_rms(x, gple_ref[...]).astype(BF16), wpg_ref[...]))
    x = x + _dot(p_ref[...].astype(BF16), wple_ref[...]) * gate
    y_ref[...] = _rms(x, gfin_ref[...])


def _layer_tail(x2d, mixes, wmixes, pre, p2d, g_mlp, w_up, w_down, g_ple, w_ple, w_pg, g_final, tm):
    rows = x2d.shape[0]
    row_spec = lambda a: pl.BlockSpec((tm, a.shape[1]), lambda i: (i, 0))
    const = lambda a: pl.BlockSpec(a.shape, lambda i: (0, 0), pipeline_mode=pl.Buffered(1))
    vec = lambda a: a.reshape(1, -1)
    has_pre = pre is not None
    args = [x2d, *mixes] + ([pre] if has_pre else []) + [p2d]
    specs = [row_spec(a) for a in args]
    consts = [*wmixes, vec(g_mlp), w_up, w_down, vec(g_ple), w_ple, w_pg, vec(g_final)]
    return pl.pallas_call(
        functools.partial(_tail_kernel, n_mix=len(mixes), has_pre=has_pre),
        grid=(rows // tm,),
        in_specs=specs + [const(a) for a in consts],
        out_specs=pl.BlockSpec((tm, D_MODEL), lambda i: (i, 0)),
        out_shape=jax.ShapeDtypeStruct((rows, D_MODEL), F32),
        compiler_params=pltpu.CompilerParams(
            dimension_semantics=("arbitrary",), vmem_limit_bytes=VMEM_LIMIT),
        name="layer_tail",
    )(*args, *consts)


def _proj_weights(w_in):
    c_q = Q_COLS
    c_gt = Q_COLS + 3 * KV_COLS
    w_packed = jnp.concatenate([
        w_in[:, :c_q] * (HEAD_DIM ** -0.5), w_in[:, c_q:c_gt],
        jnp.pad(w_in[:, c_gt:c_gt + GATE_COLS], ((0, 0), (0, LANES - GATE_COLS))),
        w_in[:, c_gt + GATE_COLS:]], axis=1).astype(BF16)
    qfill = np.zeros((1, N_HEADS * LANES), np.float32)
    for h, slope in enumerate(_slopes()):
        qfill[0, h * LANES + HEAD_DIM] = SLC_BLOCK * slope
        qfill[0, h * LANES + HEAD_DIM + 1] = slope
    return w_packed, jnp.asarray(qfill)


TQ_PAD = 8
SEL_LANES = 384
BIG = 1e9


def _masked_softmax(s, valid):
    s = jnp.where(valid, s, NEG)
    e = jnp.where(valid, jnp.exp(s - jnp.max(s, -1, keepdims=True)), 0.0)
    return e / jnp.maximum(jnp.sum(e, -1, keepdims=True), 1e-30)


def _rows_gt(x, k, width):
    return jnp.concatenate(
        [x[:, (GROUP * k + g) * LANES:(GROUP * k + g) * LANES + width] for g in range(GROUP)], axis=0)


def _sample_select_kernel(q_ref, cmp_ref, cov_ref, ocmp_ref, idx_ref, *, past_len, n_sel):
    rows = GROUP * TQ_PAD
    t_gt = lax.broadcasted_iota(jnp.int32, (rows, 1), 0) % TQ_PAD
    n_cmp_pad = cmp_ref.shape[1]
    cend = lax.broadcasted_iota(jnp.int32, (1, n_cmp_pad), 1) * CMP_STRIDE + (CMP_BLOCK - 1)
    qpos_t = past_len + lax.broadcasted_iota(jnp.int32, (TQ_PAD, 1), 0)
    jl = lax.broadcasted_iota(jnp.int32, (1, SEL_LANES), 1)
    cur = qpos_t // SLC_BLOCK
    forced = (jl == 0) | (jl == cur) | (jl == cur - 1)
    scores = []
    n_seq = q_ref.shape[0]
    for bb in range(n_seq):
        q = q_ref[bb].astype(F32)
        for k in range(N_KV_HEADS):
            qk = _rows_gt(q, k, LANES).astype(BF16)
            kc = cmp_ref[bb, :, k * LANES:(k + 1) * LANES]
            vc = cmp_ref[bb, :, (2 + k) * LANES:(3 + k) * LANES]
            p = _masked_softmax(_dot_nt(qk, kc), (past_len + t_gt) >= cend)
            ocmp_ref[bb, k] = _dot(p.astype(BF16), vc)
            psum = p[0:TQ_PAD]
            for g in range(1, GROUP):
                psum = psum + p[g * TQ_PAD:(g + 1) * TQ_PAD]
            p_hi = psum.astype(BF16)
            p_lo = (psum - p_hi.astype(F32)).astype(BF16)
            imp = _dot(p_hi, cov_ref[...]) + _dot(p_lo, cov_ref[...])
            score = (jnp.where(jl * SLC_BLOCK <= qpos_t, imp, -1.0)
                     + jnp.where(forced, FORCE_BONUS, 0.0))
            scores.append(jnp.where(jl < n_sel, score, -BIG))
    sc = jnp.concatenate(scores, axis=0)
    lane = lax.broadcasted_iota(jnp.int32, sc.shape, 1).astype(F32)
    out_lane = lax.broadcasted_iota(jnp.int32, (sc.shape[0], LANES), 1)
    picked = jnp.zeros((sc.shape[0], LANES), F32)
    for s in range(TOP_K):
        m = jnp.max(sc, -1, keepdims=True)
        am = jnp.min(jnp.where(sc == m, lane, BIG), -1, keepdims=True)
        picked = jnp.where(out_lane == s, am, picked)
        sc = jnp.where(lane == am, -2.0 * BIG, sc)
    per_seq = N_KV_HEADS * TQ_PAD
    for bb in range(n_seq):
        idx_ref[bb] = picked[bb * per_seq:(bb + 1) * per_seq].astype(jnp.int32)


SELECT_SEQS = 8


def _sample_select(q_s, cmp_s, cover_s, past_len, n_sel):
    db = q_s.shape[0]
    ns = SELECT_SEQS
    return pl.pallas_call(
        functools.partial(_sample_select_kernel, past_len=past_len, n_sel=n_sel),
        grid=(db // ns,),
        in_specs=[
            pl.BlockSpec((ns,) + q_s.shape[1:], lambda i: (i, 0, 0)),
            pl.BlockSpec((ns,) + cmp_s.shape[1:], lambda i: (i, 0, 0)),
            pl.BlockSpec(cover_s.shape, lambda i: (0, 0)),
        ],
        out_specs=[
            pl.BlockSpec((ns, N_KV_HEADS, GROUP * TQ_PAD, LANES), lambda i: (i, 0, 0, 0)),
            pl.BlockSpec((ns, N_KV_HEADS * TQ_PAD, LANES), lambda i: (i, 0, 0)),
        ],
        out_shape=[
            jax.ShapeDtypeStruct((db, N_KV_HEADS, GROUP * TQ_PAD, LANES), F32),
            jax.ShapeDtypeStruct((db, N_KV_HEADS * TQ_PAD, LANES), jnp.int32),
        ],
        compiler_params=pltpu.CompilerParams(
            dimension_semantics=("arbitrary",), vmem_limit_bytes=VMEM_LIMIT),
        name="sample_select",
    )(q_s, cmp_s, cover_s)


def _joint_softmax(s_a, valid_a, s_b, valid_b):
    s_a = jnp.where(valid_a, s_a, NEG)
    s_b = jnp.where(valid_b, s_b, NEG)
    m = jnp.maximum(jnp.max(s_a, -1, keepdims=True), jnp.max(s_b, -1, keepdims=True))
    e_a = jnp.where(valid_a, jnp.exp(s_a - m), 0.0)
    e_b = jnp.where(valid_b, jnp.exp(s_b - m), 0.0)
    inv = 1.0 / jnp.maximum(jnp.sum(e_a, -1, keepdims=True) + jnp.sum(e_b, -1, keepdims=True), 1e-30)
    return e_a * inv, e_b * inv


def _sample_attn_kernel(pt_ref, idx_ref, q_ref, gt_ref, ocmp_ref, ksn_ref, kwn_ref, slc_ref, win_ref,
                        wout_ref, o_ref, kbuf, vbuf, sem, *, past_len, tq, ppb):
    b = pl.program_id(0)
    n_past_blocks = past_len // SLC_BLOCK
    bpp = PAGE_SIZE // SLC_BLOCK
    rows = GROUP * TQ_PAD

    slot = b % 2

    def sel_index(k, t, s, bb=b):
        return idx_ref[((bb * N_KV_HEADS + k) * TQ_PAD + t) * TOP_K + s]

    def block_copies(bb, sl):
        cps = []
        for k in range(N_KV_HEADS):
            for t in range(tq):
                for s in range(TOP_K):
                    blk = jnp.minimum(sel_index(k, t, s, bb), n_past_blocks - 1)
                    page = pt_ref[bb * ppb + blk // bpp]
                    dst = pl.ds(s * PAGE_SIZE, PAGE_SIZE)
                    cps.append(pltpu.make_async_copy(
                        slc_ref.at[page, pl.ds(k * HEAD_DIM, HEAD_DIM), :], kbuf.at[sl, k, t, :, dst],
                        sem.at[sl]))
                    cps.append(pltpu.make_async_copy(
                        slc_ref.at[page, pl.ds((N_KV_HEADS + k) * HEAD_DIM, HEAD_DIM), :],
                        vbuf.at[sl, k, t, :, dst], sem.at[sl]))
        return cps

    @pl.when(b == 0)
    def _():
        for cp in block_copies(b, slot):
            cp.start()

    @pl.when(b + 1 < pl.num_programs(0))
    def _():
        for cp in block_copies(b + 1, 1 - slot):
            cp.start()

    q = q_ref[0].astype(F32)
    gates = gt_ref[0]
    t_gt = lax.broadcasted_iota(jnp.int32, (rows, 1), 0) % TQ_PAD
    g_gt = lax.broadcasted_iota(jnp.int32, (rows, 1), 0) // TQ_PAD
    qpos = past_len + t_gt
    lane_pg = lax.broadcasted_iota(jnp.int32, (1, PAGE_SIZE), 1)
    new_pos = past_len + lax.broadcasted_iota(jnp.int32, (1, TQ_PAD), 1)
    win_len = win_ref.shape[2]
    win_pos = past_len - win_len + lax.broadcasted_iota(jnp.int32, (1, win_len), 1)
    slopes = _slopes()

    for cp in block_copies(b, slot):
        cp.wait()

    acc = jnp.zeros((TQ_PAD, D_MODEL), F32)
    for k in range(N_KV_HEADS):
        qk = _rows_gt(q, k, HEAD_DIM).astype(BF16)
        slope = jnp.zeros((rows, 1), F32)
        for g in range(GROUP):
            slope = jnp.where(g_gt == g, float(slopes[GROUP * k + g]), slope)
        ks_new = ksn_ref[0, :, k * HEAD_DIM:(k + 1) * HEAD_DIM].astype(BF16)
        vs_new = ksn_ref[0, :, (N_KV_HEADS + k) * HEAD_DIM:(N_KV_HEADS + k + 1) * HEAD_DIM].astype(BF16)
        kw_new = kwn_ref[0, :, k * HEAD_DIM:(k + 1) * HEAD_DIM].astype(BF16)
        vw_new = kwn_ref[0, :, (N_KV_HEADS + k) * HEAD_DIM:(N_KV_HEADS + k + 1) * HEAD_DIM].astype(BF16)

        kpos, chosen, owner, s_parts = [], [], [], []
        has_new = jnp.zeros((rows, 1), jnp.int32)
        for t in range(tq):
            n_new = 0
            for s in range(TOP_K):
                blk = sel_index(k, t, s)
                is_past = blk < n_past_blocks
                page_blk = jnp.minimum(blk, n_past_blocks - 1)
                kpos.append((page_blk // bpp) * PAGE_SIZE + lane_pg)
                half = jnp.where(is_past, page_blk % bpp, -1)
                chosen.append(jnp.where((lane_pg // SLC_BLOCK) == half, 1, 0))
                owner.append(jnp.full((1, PAGE_SIZE), t, jnp.int32))
                n_new = n_new + jnp.where(is_past, 0, 1)
            has_new = jnp.where(t_gt == t, n_new, has_new)
            s_parts.append(_dot(qk, kbuf[slot, k, t].astype(BF16)))
        kpos = jnp.concatenate(kpos, axis=1)
        chosen = jnp.concatenate(chosen, axis=1)
        owner = jnp.concatenate(owner, axis=1)
        valid = (t_gt == owner) & (chosen > 0) & (kpos <= qpos)
        valid_new = (new_pos <= qpos) & (has_new > 0) & (t_gt < tq)
        s_past = jnp.concatenate(s_parts, axis=1) - slope * (qpos - kpos).astype(F32)
        s_new = _dot_nt(qk, ks_new) - slope * (qpos - new_pos).astype(F32)
        p_past, p_new = _joint_softmax(s_past, valid, s_new, valid_new)
        p_past = p_past.astype(BF16)
        o_slc = _dot(p_new.astype(BF16), vs_new)
        seg = TOP_K * PAGE_SIZE
        for t in range(tq):
            o_slc = o_slc + _dot_nt(p_past[:, t * seg:(t + 1) * seg], vbuf[slot, k, t].astype(BF16))

        d_w = qpos - win_pos
        d_n = qpos - new_pos
        s_w = _dot(qk, win_ref[0, k * HEAD_DIM:(k + 1) * HEAD_DIM, :].astype(BF16)) - slope * d_w.astype(F32)
        s_n = _dot_nt(qk, kw_new) - slope * d_n.astype(F32)
        p_w, p_n = _joint_softmax(s_w, (d_w >= 0) & (d_w < WINDOW) & (win_pos >= 0),
                                  s_n, (d_n >= 0) & (d_n < WINDOW))
        v_w = win_ref[0, (N_KV_HEADS + k) * HEAD_DIM:(N_KV_HEADS + k + 1) * HEAD_DIM, :].astype(BF16)
        o_win = _dot_nt(p_w.astype(BF16), v_w) + _dot(p_n.astype(BF16), vw_new)

        def gate(branch):
            return jnp.concatenate(
                [gates[:, branch * N_HEADS + GROUP * k + g:branch * N_HEADS + GROUP * k + g + 1]
                 for g in range(GROUP)], axis=0)

        o = (gate(0) * ocmp_ref[0, k][:, 0:HEAD_DIM] + gate(1) * o_slc + gate(2) * o_win).astype(BF16)
        for g in range(GROUP):
            h = GROUP * k + g
            acc = acc + _dot(o[g * TQ_PAD:(g + 1) * TQ_PAD], wout_ref[h * HEAD_DIM:(h + 1) * HEAD_DIM, :])
    o_ref[0] = acc


def _sample_attn(page_table, idx, q_s, gates_s, ocmp, ks_new, kw_new, slc_t, win_t, w_out_attn,
                 past_len, tq):
    db, ppb = page_table.shape
    blk3 = lambda a: pl.BlockSpec((1,) + a.shape[1:], lambda i, pt, ix: (i, 0, 0))
    grid_spec = pltpu.PrefetchScalarGridSpec(
        num_scalar_prefetch=2,
        grid=(db,),
        in_specs=[
            blk3(q_s), blk3(gates_s),
            pl.BlockSpec((1,) + ocmp.shape[1:], lambda i, pt, ix: (i, 0, 0, 0)),
            blk3(ks_new), blk3(kw_new),
            pl.BlockSpec(memory_space=pl.ANY),
            blk3(win_t),
            pl.BlockSpec(w_out_attn.shape, lambda i, pt, ix: (0, 0)),
        ],
        out_specs=pl.BlockSpec((1, TQ_PAD, D_MODEL), lambda i, pt, ix: (i, 0, 0)),
        scratch_shapes=[
            pltpu.VMEM((2, N_KV_HEADS, tq, HEAD_DIM, TOP_K * PAGE_SIZE), F32),
            pltpu.VMEM((2, N_KV_HEADS, tq, HEAD_DIM, TOP_K * PAGE_SIZE), F32),
            pltpu.SemaphoreType.DMA((2,)),
        ],
    )
    return pl.pallas_call(
        functools.partial(_sample_attn_kernel, past_len=past_len, tq=tq, ppb=ppb),
        grid_spec=grid_spec,
        out_shape=jax.ShapeDtypeStruct((db, TQ_PAD, D_MODEL), F32),
        compiler_params=pltpu.CompilerParams(
            dimension_semantics=("arbitrary",), vmem_limit_bytes=VMEM_LIMIT),
        name="sample_attn",
    )(page_table.reshape(-1), idx, q_s, gates_s, ocmp, ks_new, kw_new, slc_t, win_t, w_out_attn)


def _pages_feature_major(cache):
    return jnp.transpose(cache, (0, 2, 3, 4, 1)).reshape(cache.shape[0], KV_COLS, cache.shape[1])


def kernel(x_prompt, x_sample, p_prompt, p_sample, cache_cmp_kv, cache_slc_kv, cache_win_kv, state_conv, page_table, g_attn, w_in, w_cmp1, w_cmp2, pe_cmp, conv_w, conv_b, ln_conv_g, ln_conv_b, w_out, g_mlp, w_up, w_down, g_ple, w_ple, w_ple_gate, g_final):
    b, t, _ = x_prompt.shape
    db, tq, _ = x_sample.shape
    win_buf = cache_win_kv.shape[2]
    kv5 = lambda a, nb, nt: a.reshape(1, nb, nt, 2, N_KV_HEADS, HEAD_DIM)

    w_out_attn = w_out[0][:MIX_ATTN].astype(BF16)
    w_out_conv = w_out[0][MIX_ATTN:].astype(BF16)
    tail_w = (g_mlp[0], w_up[0].astype(BF16), w_down[0].astype(BF16), g_ple[0],
              w_ple[0].astype(BF16), w_ple_gate[0].astype(BF16), g_final)
    g_row = g_attn[0].reshape(1, D_MODEL)

    w_packed, qfill = _proj_weights(w_in[0])
    conv_params = _conv_params(conv_w[0], conv_b[0], ln_conv_g[0], ln_conv_b[0])
    po = _project(x_prompt.reshape(b * t, D_MODEL), g_row, w_packed, qfill, 512, t, prompt=True)
    u_p = po["u"].reshape(b, t, CONV_CH)
    conv_p = _conv_tail(jnp.zeros((b, HALO, CONV_CH), F32), u_p, conv_params, 512)
    ppb = t // PAGE_SIZE
    w1big, w1cat, cw = _compress_weights(w_cmp1[0], w_cmp2[0], pe_cmp[0])
    cmp_p = _compress(jnp.arange(b * ppb, dtype=jnp.int32).reshape(b, ppb),
                      po["kc"].reshape(b * ppb, PAGE_SIZE, KV_COLS), w1big, *cw, pages=ppb)
    n_chunk = t // CMP_STRIDE
    cover = _selection_constants(n_chunk, n_chunk - 1, t // SLC_BLOCK)
    attn_p = _nsa_prompt(po["q_aug"].reshape(b, t, -1), po["gates"].reshape(b, t, LANES), cmp_p,
                         po["ks_aug"].reshape(b, t, -1), po["kw_aug"].reshape(b, t, -1), cover)
    y_prompt = _layer_tail(x_prompt.reshape(b * t, D_MODEL),
                           [attn_p.reshape(b * t, -1), conv_p.reshape(b * t, CONV_CH)],
                           [w_out_attn, w_out_conv], None, p_prompt[0].reshape(b * t, PLE_DIM),
                           *tail_w, tm=512).reshape(b, t, D_MODEL)

    xs = jnp.pad(x_sample, ((0, 0), (0, TQ_PAD - tq), (0, 0))).reshape(db * TQ_PAD, D_MODEL)
    so = _project(xs, g_row, w_packed, qfill, db * TQ_PAD, TQ_PAD, prompt=False)
    rs = lambda a: a.reshape(db, TQ_PAD, -1)
    u_s = rs(so["u"])
    kc_s, ks_s, kw_s = rs(so["kc"])[:, :tq], rs(so["ks"])[:, :tq], rs(so["kw"])[:, :tq]
    state32 = jnp.pad(state_conv[0], ((0, 0), (HALO - (CONV_W - 1), 0), (0, 0)))
    conv_s = _conv_tail(state32, u_s, conv_params, TQ_PAD)

    past_len = page_table.shape[1] * PAGE_SIZE
    assert (past_len + tq) // CMP_STRIDE == past_len // CMP_STRIDE and tq <= TQ_PAD
    n_chunk_s = past_len // CMP_STRIDE
    n_sel_s = past_len // SLC_BLOCK + 1
    cmp_s = _compress_paged(page_table, _pages_feature_major(cache_cmp_kv[0]), w1cat, *cw, pages=64)
    ci = np.arange(n_chunk_s)[:, None] * CMP_STRIDE
    sj = np.arange(SEL_LANES)[None, :] * SLC_BLOCK
    cover_s = jnp.asarray((ci + CMP_BLOCK > sj) & (ci < sj + SLC_BLOCK)
                          & (np.arange(n_chunk_s)[:, None] < n_chunk_s - 1)
                          & (np.arange(SEL_LANES)[None, :] < n_sel_s), BF16)
    q_s = rs(so["q_aug"])
    ocmp_s, idx_s = _sample_select(q_s, cmp_s, cover_s, past_len, n_sel_s)
    pre_s = _sample_attn(page_table, idx_s[:, :, :TOP_K].reshape(-1), q_s, rs(so["gates"]), ocmp_s,
                         rs(so["ks"]), rs(so["kw"]), _pages_feature_major(cache_slc_kv[0]),
                         _pages_feature_major(cache_win_kv[0]), w_out_attn,
                         past_len, tq)
    p_s = jnp.pad(p_sample[0], ((0, 0), (0, TQ_PAD - tq), (0, 0))).reshape(db * TQ_PAD, PLE_DIM)
    y_sample = _layer_tail(xs, [conv_s.reshape(db * TQ_PAD, CONV_CH)], [w_out_conv],
                           pre_s.reshape(db * TQ_PAD, D_MODEL), p_s, *tail_w,
                           tm=db * TQ_PAD).reshape(db, TQ_PAD, D_MODEL)[:, :tq]

    def token_major(a_t):
        nt = a_t.shape[2]
        return jnp.transpose(a_t.reshape(b, 2, N_KV_HEADS, HEAD_DIM, nt), (0, 4, 1, 2, 3))[None]

    new_win = jnp.concatenate([cache_win_kv[:, :, tq:], kv5(kw_s, db, tq)], 2)
    new_conv_s = jnp.concatenate([state_conv[0], u_s[:, :tq]], 1)[:, -(CONV_W - 1):]
    return (y_prompt, y_sample,
            token_major(po["kc_t"]), token_major(po["ks_t"]), token_major(po["kw_t"][:, :, t - win_buf:]),
            u_p[:, -(CONV_W - 1):][None],
            kv5(kc_s, db, tq), kv5(ks_s, db, tq), new_win, new_conv_s[None])
```

```python
import functools

import numpy as np
import jax
import jax.numpy as jnp
from jax import lax
from jax.experimental import pallas as pl
from jax.experimental.pallas import tpu as pltpu

D_MODEL = 1024
N_HEADS = 8
HEAD_DIM = 64
N_KV_HEADS = 2
GROUP = N_HEADS // N_KV_HEADS
MIX_ATTN = N_HEADS * HEAD_DIM
CONV_CH = D_MODEL - MIX_ATTN
CONV_W = 31
CMP_BLOCK = 32
CMP_STRIDE = 16
CMP_HID = 2 * HEAD_DIM
SLC_BLOCK = 64
TOP_K = 16
WINDOW = 512
PAGE_SIZE = 128
D_FF = 4 * D_MODEL
PLE_DIM = 256
Q_COLS = N_HEADS * HEAD_DIM
KV_COLS = 2 * N_KV_HEADS * HEAD_DIM
GATE_COLS = 3 * N_HEADS
EPS = 1e-6
NEG = -1e30
FORCE_BONUS = 1e4

LANES = 128
Q_TILE = 256
VMEM_LIMIT = 56 * 1024 * 1024

F32 = jnp.float32
BF16 = jnp.bfloat16


def _dot(a, b):
    return jnp.dot(a, b, preferred_element_type=F32)


def _dot_nt(a, b):
    return lax.dot_general(a, b, (((1,), (1,)), ((), ())), preferred_element_type=F32)


def _sigmoid(x):
    return 1.0 / (1.0 + jnp.exp(-x))


def _rms(x, g):
    return x * lax.rsqrt(jnp.mean(x * x, -1, keepdims=True) + EPS) * g


def _slopes():
    return 2.0 ** (-8.0 * np.arange(1, N_HEADS + 1) / N_HEADS)


def _expand_heads(y, fill):
    low = lax.broadcasted_iota(jnp.int32, (y.shape[0], LANES), 1) < HEAD_DIM
    tiles = []
    for i in range(y.shape[1] // HEAD_DIM):
        pair = y[:, (i // 2) * LANES:(i // 2 + 1) * LANES]
        data = pair if i % 2 == 0 else pltpu.roll(pair, HEAD_DIM, axis=1)
        tiles.append(jnp.where(low, data, fill[i]))
    return tiles


def _store_tiles(o_ref, tiles):
    for i, t in enumerate(tiles):
        o_ref[:, i * LANES:(i + 1) * LANES] = t.astype(o_ref.dtype)


PROJ_OUTPUTS_PROMPT = ("q_aug", "kc_t", "ks_t", "ks_aug", "kw_t", "kw_aug", "gates", "u")
PROJ_OUTPUTS_SAMPLE = ("q_aug", "kc", "ks", "kw", "gates", "u")


def _proj_kernel(x_ref, g_ref, w_ref, qfill_ref, *out_refs, seq_len, prompt):
    x = x_ref[...]
    xn = _rms(x, g_ref[...]).astype(BF16)
    tm = x.shape[0]
    outs = dict(zip(PROJ_OUTPUTS_PROMPT if prompt else PROJ_OUTPUTS_SAMPLE, out_refs))
    col = [0]

    def matmul(width):
        y = _dot(xn, w_ref[:, col[0]:col[0] + width])
        col[0] += width
        return y

    _store_tiles(outs["q_aug"], _expand_heads(
        matmul(Q_COLS), [qfill_ref[:, h * LANES:(h + 1) * LANES] for h in range(N_HEADS)]))
    if prompt:
        pos = (pl.program_id(0) * tm + lax.broadcasted_iota(jnp.int32, (tm, LANES), 0)) % seq_len
        lane = lax.broadcasted_iota(jnp.int32, (tm, LANES), 1)
        blk_lane = SEL_LANE0 + (pos // SLC_BLOCK) % (SLC_CHUNK // SLC_BLOCK)
        k_fill = (jnp.where(lane == HEAD_DIM, (pos // SLC_BLOCK).astype(F32), 0.0)
                  + jnp.where(lane == HEAD_DIM + 1, (pos % SLC_BLOCK).astype(F32), 0.0)
                  + jnp.where(lane == blk_lane, 1.0, 0.0))
        v_fill = jnp.where(lane == HEAD_DIM, 1.0, 0.0)
        outs["kc_t"][0] = matmul(KV_COLS).T
        for name in ("ks", "kw"):
            y = matmul(KV_COLS)
            outs[name + "_t"][0] = y.T
            _store_tiles(outs[name + "_aug"], _expand_heads(y, [k_fill, k_fill, v_fill, v_fill]))
    else:
        for name in ("kc", "ks", "kw"):
            outs[name][...] = matmul(KV_COLS)
    outs["gates"][...] = _sigmoid(matmul(LANES))
    a = matmul(CONV_CH)
    outs["u"][...] = a * _sigmoid(matmul(CONV_CH))


def _project(x2d, g, w_packed, qfill, tm, seq_len, prompt):
    rows = x2d.shape[0]
    nw = w_packed.shape[1]
    tiles = seq_len // tm if prompt else 1
    widths = {"q_aug": (N_HEADS * LANES, BF16), "kc": (KV_COLS, F32), "ks": (KV_COLS, F32),
              "kw": (KV_COLS, F32), "ks_aug": (4 * LANES, BF16), "kw_aug": (4 * LANES, BF16),
              "gates": (LANES, F32), "u": (CONV_CH, F32)}
    names = PROJ_OUTPUTS_PROMPT if prompt else PROJ_OUTPUTS_SAMPLE
    out_shape, out_specs = [], []
    for name in names:
        if name.endswith("_t"):
            out_shape.append(jax.ShapeDtypeStruct((rows // seq_len, KV_COLS, seq_len), F32))
            out_specs.append(pl.BlockSpec((1, KV_COLS, tm), lambda i: (i // tiles, 0, i % tiles)))
        else:
            width, dt = widths[name]
            out_shape.append(jax.ShapeDtypeStruct((rows, width), dt))
            out_specs.append(pl.BlockSpec((tm, width), lambda i: (i, 0)))
    outs = pl.pallas_call(
        functools.partial(_proj_kernel, seq_len=seq_len, prompt=prompt),
        grid=(rows // tm,),
        in_specs=[
            pl.BlockSpec((tm, D_MODEL), lambda i: (i, 0)),
            pl.BlockSpec((1, D_MODEL), lambda i: (0, 0)),
            pl.BlockSpec((D_MODEL, nw), lambda i: (0, 0), pipeline_mode=pl.Buffered(1)),
            pl.BlockSpec(qfill.shape, lambda i: (0, 0)),
        ],
        out_specs=out_specs,
        out_shape=out_shape,
        compiler_params=pltpu.CompilerParams(
            dimension_semantics=("arbitrary",), vmem_limit_bytes=VMEM_LIMIT),
        name="proj",
    )(x2d, g, w_packed, qfill)
    return dict(zip(names, outs))


HALO = 32


def _conv_ln_swish(u, w_ref, cb_ref, lg_ref, lb_ref, ext_ref, sh_ref):
    tt = u.shape[0]
    ext_ref[HALO:HALO + tt, :] = u
    for r in range(1, 8):
        sh_ref[r] = ext_ref[pl.ds(r, tt + HALO - 8), :]
    w = w_ref[...]
    y = jnp.zeros((tt, CONV_CH), F32) + cb_ref[...]
    for r in range(8):
        for q in range(5 if r == 0 else 4):
            k = 8 * q + r - 2
            if 0 <= k < CONV_W:
                src = ext_ref[8 * q:8 * q + tt, :] if r == 0 else sh_ref[r, 8 * q:8 * q + tt, :]
                y = y + w[k:k + 1, :] * src
    mu = jnp.mean(y, -1, keepdims=True)
    yc = y - mu
    yn = yc * lax.rsqrt(jnp.mean(yc * yc, -1, keepdims=True) + EPS) * lg_ref[...] + lb_ref[...]
    ext_ref[0:HALO, :] = ext_ref[tt:tt + HALO, :]
    return yn * _sigmoid(yn)


def _conv_kernel(st_ref, u_ref, w_ref, cb_ref, lg_ref, lb_ref, o_ref, ext_ref, sh_ref):
    @pl.when(pl.program_id(1) == 0)
    def _():
        ext_ref[0:HALO, :] = st_ref[0]

    o_ref[0] = _conv_ln_swish(u_ref[0], w_ref, cb_ref, lg_ref, lb_ref, ext_ref, sh_ref).astype(o_ref.dtype)


def _conv_params(conv_w, conv_b, ln_g, ln_b):
    row = lambda a: a.reshape(1, CONV_CH)
    return jnp.pad(conv_w, ((0, 1), (0, 0))), row(conv_b), row(ln_g), row(ln_b)


def _conv_scratch(tt):
    return [pltpu.VMEM((HALO + tt, CONV_CH), F32), pltpu.VMEM((8, HALO + tt - 8, CONV_CH), F32)]


def _conv_tail(state32, u, conv_params, tt):
    b, t, _ = u.shape
    return pl.pallas_call(
        _conv_kernel,
        grid=(b, t // tt),
        in_specs=[
            pl.BlockSpec((1, HALO, CONV_CH), lambda i, j: (i, 0, 0)),
            pl.BlockSpec((1, tt, CONV_CH), lambda i, j: (i, j, 0)),
            pl.BlockSpec((HALO, CONV_CH), lambda i, j: (0, 0)),
            pl.BlockSpec((1, CONV_CH), lambda i, j: (0, 0)),
            pl.BlockSpec((1, CONV_CH), lambda i, j: (0, 0)),
            pl.BlockSpec((1, CONV_CH), lambda i, j: (0, 0)),
        ],
        out_specs=pl.BlockSpec((1, tt, CONV_CH), lambda i, j: (i, j, 0)),
        out_shape=jax.ShapeDtypeStruct((b, t, CONV_CH), BF16),
        scratch_shapes=_conv_scratch(tt),
        compiler_params=pltpu.CompilerParams(
            dimension_semantics=("arbitrary", "arbitrary"), vmem_limit_bytes=VMEM_LIMIT),
        name="conv_tail",
    )(state32, u, *conv_params)


def _gelu_tanh(x):
    return 0.5 * x * (1.0 + jnp.tanh(np.sqrt(2.0 / np.pi) * (x + 0.044715 * (x * x * x))))


def _pe_term(pe_ref, w1f_ref, pet):
    for c in range(2):
        t = lax.dot_general(pe_ref[c], w1f_ref[c], (((1,), (0,)), ((), ())),
                            precision=lax.Precision.HIGHEST, preferred_element_type=F32)
        pet[c] = jnp.concatenate([t, t], axis=1)


def _cmp_key_columns(n0, n):
    nidx = n0 + lax.broadcasted_iota(jnp.int32, (n, 2 * LANES), 0)
    cend = nidx * CMP_STRIDE + (CMP_BLOCK - 1)
    lane = lax.broadcasted_iota(jnp.int32, (n, 2 * LANES), 1) % LANES
    return (jnp.where(lane == HEAD_DIM, (cend // SLC_BLOCK).astype(F32), 0.0)
            + jnp.where(lane == HEAD_DIM + 1, (cend % SLC_BLOCK).astype(F32), 0.0))


def _compress_paged_kernel(pt_ref, cache_ref, w1_ref, pe_ref, w1f_ref, w2k_ref, w2v_ref, o_ref,
                           stg, xj, pet, sem, tsem, *, pages, n_groups, ppb, n_steps, paged):
    step = pl.program_id(0) * n_groups + pl.program_id(1)
    chunks_per_page = PAGE_SIZE // CMP_STRIDE
    n = pages * chunks_per_page
    pitch = n + 8
    hid = 2 * CMP_HID

    def page_src(seq, page):
        if paged:
            return cache_ref.at[pt_ref[seq * ppb + page]]
        return cache_ref.at[seq, :, pl.ds(pl.multiple_of(page * PAGE_SIZE, PAGE_SIZE), PAGE_SIZE)]

    def page_copies(s, sl):
        return [pltpu.make_async_copy(page_src(s // n_groups, (s % n_groups) * pages + p), stg.at[sl, p],
                                      sem.at[sl]) for p in range(pages)]

    def tail_copy(s, sl):
        return pltpu.make_async_copy(page_src(s // n_groups, (s % n_groups + 1) * pages), stg.at[sl, pages],
                                     tsem.at[sl])

    def has_lookahead(s):
        return s % n_groups + 1 < n_groups

    def start_fetch(s, sl):
        for cp in page_copies(s, sl):
            cp.start()

        @pl.when(has_lookahead(s))
        def _():
            tail_copy(s, sl).start()

    def wait_fetch(s, sl):
        for cp in page_copies(s, sl):
            cp.wait()

        @pl.when(has_lookahead(s))
        def _():
            tail_copy(s, sl).wait()

    def relayout(s, sl):
        def scatter_tokens(x_t, chunk0, n_tok, keep=None):
            for c in range(2):
                xt = x_t[c * LANES:(c + 1) * LANES, :].T
                for i in range(n_tok // 8):
                    rows8 = xt[8 * i:8 * i + 8, :]
                    if keep is not None:
                        rows8 = jnp.where(keep, rows8, 0.0)
                    first = (8 * (i % 2)) * pitch + chunk0 + i // 2
                    xj[sl, c, pl.ds(first, 8, stride=pitch), :] = rows8

        for c in range(2):
            for j in range(CMP_STRIDE):
                xj[sl, c, j * pitch + n:(j + 1) * pitch, :] = jnp.zeros((pitch - n, LANES), F32)
        for p in range(pages):
            scatter_tokens(stg[sl, p], p * chunks_per_page, PAGE_SIZE)
        keep = (lax.broadcasted_iota(jnp.int32, (8, LANES), 0) * 0 + jnp.where(has_lookahead(s), 1, 0)) > 0
        scatter_tokens(stg[sl, pages], n, CMP_STRIDE, keep)

    def mlp(sl):
        aug = _cmp_key_columns(pl.program_id(1) * n, n)
        for c, w2_ref in ((0, w2k_ref), (1, w2v_ref)):
            lhs = jnp.concatenate(
                [xj[sl, c, j * pitch:(j + 1) * pitch, :].astype(BF16) for j in range(CMP_STRIDE)], axis=1)
            res = _dot(lhs, w1_ref[c])
            second_half_next = pltpu.roll(res[:, hid:], pitch - 1, axis=0)
            pre = (res[:, :hid] + second_half_next)[0:n] + pet[c][0:1, :]
            h = _gelu_tanh(pre).astype(BF16)
            out = _dot(h, w2_ref[...])
            if c == 0:
                o_ref[0, :, 0:2 * LANES] = (out + aug).astype(o_ref.dtype)
            else:
                o_ref[0, :, 2 * LANES:] = out.astype(o_ref.dtype)

    @pl.when(step == 0)
    def _():
        _pe_term(pe_ref, w1f_ref, pet)
        for sl in range(2):
            stg[sl, pages] = jnp.zeros((KV_COLS, PAGE_SIZE), F32)
        start_fetch(0, 0)
        if n_steps > 1:
            start_fetch(1, 1)
        wait_fetch(0, 0)
        relayout(0, 0)

    for par in range(2):
        @pl.when(step % 2 == par)
        def _(par=par):
            @pl.when(step + 1 < n_steps)
            def _():
                wait_fetch(step + 1, 1 - par)

            @pl.when(step + 2 < n_steps)
            def _():
                start_fetch(step + 2, par)

            relayout(step + 1, 1 - par)
            mlp(par)


def _compress_paged(page_table, cache_t, w1cat, pe8, w1f, w2k, w2v, pages):
    paged = page_table is not None
    if paged:
        nb, ppb = page_table.shape
    else:
        nb, ppb = cache_t.shape[0], cache_t.shape[2] // PAGE_SIZE
        page_table = jnp.zeros((1, 1), jnp.int32)
    n_groups = ppb // pages
    n = pages * PAGE_SIZE // CMP_STRIDE
    ncols = 2 * LANES + w2v.shape[1]
    once = dict(pipeline_mode=pl.Buffered(1))
    grid_spec = pltpu.PrefetchScalarGridSpec(
        num_scalar_prefetch=1,
        grid=(nb, n_groups),
        in_specs=[
            pl.BlockSpec(memory_space=pl.ANY),
            pl.BlockSpec(w1cat.shape, lambda i, j, pt: (0, 0, 0), **once),
            pl.BlockSpec(pe8.shape, lambda i, j, pt: (0, 0, 0), **once),
            pl.BlockSpec(w1f.shape, lambda i, j, pt: (0, 0, 0), **once),
            pl.BlockSpec(w2k.shape, lambda i, j, pt: (0, 0), **once),
            pl.BlockSpec(w2v.shape, lambda i, j, pt: (0, 0), **once),
        ],
        out_specs=pl.BlockSpec((1, n, ncols), lambda i, j, pt: (i, j, 0)),
        scratch_shapes=[
            pltpu.VMEM((2, pages + 1, KV_COLS, PAGE_SIZE), F32),
            pltpu.VMEM((2, 2, CMP_STRIDE * (n + 8), LANES), F32),
            pltpu.VMEM((2, 8, 2 * CMP_HID), F32),
            pltpu.SemaphoreType.DMA((2,)),
            pltpu.SemaphoreType.DMA((2,)),
        ],
    )
    return pl.pallas_call(
        functools.partial(_compress_paged_kernel, pages=pages, n_groups=n_groups, ppb=ppb,
                          n_steps=nb * n_groups, paged=paged),
        grid_spec=grid_spec,
        out_shape=jax.ShapeDtypeStruct((nb, n_groups * n, ncols), BF16),
        compiler_params=pltpu.CompilerParams(
            dimension_semantics=("arbitrary", "arbitrary"), vmem_limit_bytes=VMEM_LIMIT),
        name="compress_paged",
    )(page_table.reshape(-1), cache_t, w1cat, pe8, w1f, w2k, w2v)


def _compress_weights(w_cmp1, w_cmp2, pe_cmp):
    w1 = w_cmp1
    z = jnp.zeros_like(w1)
    w1big = jnp.stack([jnp.concatenate([w1, z], -1), jnp.concatenate([z, w1], -1)], axis=2)
    w1cat = jnp.concatenate([w1big[:, :CMP_STRIDE], w1big[:, CMP_STRIDE:]], axis=-1)
    w1cat = w1cat.reshape(2, CMP_STRIDE * 2 * HEAD_DIM, 4 * CMP_HID).astype(BF16)
    pe8 = jnp.broadcast_to(pe_cmp.reshape(2, 1, CMP_BLOCK * HEAD_DIM), (2, 8, CMP_BLOCK * HEAD_DIM))
    w1f = w1.reshape(2, CMP_BLOCK * HEAD_DIM, CMP_HID)

    def w2_layout(w2):
        out = jnp.zeros((2, CMP_HID, 2, LANES), F32)
        for h in range(2):
            out = out.at[h, :, h, 0:HEAD_DIM].set(w2)
        return out.reshape(2 * CMP_HID, 2 * LANES).astype(BF16)

    return w1cat, (pe8, w1f, w2_layout(w_cmp2[0]), w2_layout(w_cmp2[1]))


SLC_CHUNK = 1024
SEL_LANE0 = HEAD_DIM + 2
WIN_SPAN = WINDOW + Q_TILE


def _nsa_prompt_kernel(q_ref, gt_ref, cmp_ref, ks_ref, kw_ref, cov_ref, o_ref):
    i = pl.program_id(1)
    qs = i * Q_TILE
    q = q_ref[0]
    gates = gt_ref[0]
    qpos_t = qs + lax.broadcasted_iota(jnp.int32, (Q_TILE, 1), 0)
    qpos_gt = jnp.concatenate([qpos_t] * GROUP, axis=0)
    n_cmp_pad = cmp_ref.shape[1]
    cend = lax.broadcasted_iota(jnp.int32, (1, n_cmp_pad), 1) * CMP_STRIDE + (CMP_BLOCK - 1)
    jl = lax.broadcasted_iota(jnp.int32, (1, LANES), 1)
    cur = qpos_t // SLC_BLOCK
    forced = (jl == 0) | (jl == cur) | (jl == cur - 1)
    n_sel = ks_ref.shape[1] // SLC_BLOCK
    jrow = lax.broadcasted_iota(jnp.int32, (n_sel, Q_TILE), 0)

    kend = qs + Q_TILE
    woff = pl.multiple_of(jnp.maximum(kend - WIN_SPAN, 0), Q_TILE)
    d_w = qpos_t - (woff + lax.broadcasted_iota(jnp.int32, (1, WIN_SPAN), 1))
    bias_w = jnp.tile(jnp.where((d_w >= 0) & (d_w < WINDOW), 0.0, NEG), (GROUP, 1))

    qks, o_cmps, sel_negs, a_wins = [], [], [], []
    for k in range(N_KV_HEADS):
        qk = jnp.concatenate(
            [q[:, (GROUP * k + g) * LANES:(GROUP * k + g + 1) * LANES] for g in range(GROUP)], axis=0)

        s = _dot_nt(qk, kw_ref[0, pl.ds(woff, WIN_SPAN), k * LANES:(k + 1) * LANES]) + bias_w
        p = jnp.exp(s - jnp.max(s, -1, keepdims=True)).astype(BF16)
        a_wins.append(_dot(p, kw_ref[0, pl.ds(woff, WIN_SPAN), (2 + k) * LANES:(3 + k) * LANES]))

        kc = cmp_ref[0, :, k * LANES:(k + 1) * LANES]
        vc = cmp_ref[0, :, (2 + k) * LANES:(3 + k) * LANES]
        valid = qpos_gt >= cend
        s = jnp.where(valid, _dot_nt(qk, kc), NEG)
        e = jnp.where(valid, jnp.exp(s - jnp.max(s, -1, keepdims=True)), 0.0)
        p = e / jnp.maximum(jnp.sum(e, -1, keepdims=True), 1e-30)
        o_cmp = _dot(p.astype(BF16), vc)

        psum = p[0:Q_TILE]
        for g in range(1, GROUP):
            psum = psum + p[g * Q_TILE:(g + 1) * Q_TILE]
        p_hi = psum.astype(BF16)
        p_lo = (psum - p_hi.astype(F32)).astype(BF16)
        imp = _dot(p_hi, cov_ref[...]) + _dot(p_lo, cov_ref[...])
        score = (jnp.where(jl * SLC_BLOCK <= qpos_t, imp, -1.0)
                 + jnp.where(forced, FORCE_BONUS, 0.0))
        sc_t = score.T[0:n_sel]
        cnt = jnp.zeros((n_sel, Q_TILE), jnp.int32)
        for jp in range(n_sel):
            row = sc_t[jp:jp + 1, :]
            tie = jnp.where(jrow > jp, 1, 0)
            cnt = cnt + jnp.where(row > sc_t, 1, jnp.where(row == sc_t, tie, 0))
        sel_t = jnp.where(cnt < TOP_K, 1.0, 0.0)
        sel_t = jnp.concatenate([sel_t, jnp.zeros((LANES - n_sel, Q_TILE), F32)], axis=0)
        qks.append(qk)
        o_cmps.append(o_cmp)
        sel_negs.append(jnp.where(sel_t.T > 0.5, 0.0, NEG))

    rows = GROUP * Q_TILE

    blocks_per_chunk = SLC_CHUNK // SLC_BLOCK
    in_sel_lanes = (jl >= SEL_LANE0) & (jl < SEL_LANE0 + blocks_per_chunk)
    q_f32 = [[q[:, (GROUP * k + g) * LANES:(GROUP * k + g + 1) * LANES].astype(F32) for g in range(GROUP)]
             for k in range(N_KV_HEADS)]

    def slc_chunk(c, carry, bias):
        off = pl.multiple_of(c * SLC_CHUNK, SLC_CHUNK)
        out = []
        for k in range(N_KV_HEADS):
            m, acc = carry[k]
            mask_cols = jnp.where(
                in_sel_lanes, pltpu.roll(sel_negs[k], SEL_LANE0 - c * blocks_per_chunk, axis=1), 0.0)
            qx = jnp.concatenate([(q_f32[k][g] + mask_cols).astype(BF16) for g in range(GROUP)], axis=0)
            s = _dot_nt(qx, ks_ref[0, pl.ds(off, SLC_CHUNK), k * LANES:(k + 1) * LANES])
            if bias is not None:
                s = s + bias
            m_new = jnp.maximum(m, jnp.max(s, -1, keepdims=True))
            p = jnp.exp(s - m_new).astype(BF16)
            acc = jnp.exp(m - m_new) * acc + _dot(
                p, ks_ref[0, pl.ds(off, SLC_CHUNK), (2 + k) * LANES:(3 + k) * LANES])
            out.append((m_new, acc))
        return tuple(out)

    init = tuple((jnp.full((rows, 1), 0.1 * NEG, F32), jnp.zeros((rows, LANES), F32))
                 for _ in range(N_KV_HEADS))
    last = (kend - 1) // SLC_CHUNK
    slc = lax.fori_loop(0, last, lambda c, carry: slc_chunk(c, carry, None), init)
    kpos_last = last * SLC_CHUNK + lax.broadcasted_iota(jnp.int32, (1, SLC_CHUNK), 1)
    slc = slc_chunk(last, slc, jnp.tile(jnp.where(kpos_last <= qpos_t, 0.0, NEG), (GROUP, 1)))

    for k in range(N_KV_HEADS):
        a_win = a_wins[k]
        a_slc = slc[k][1]
        o_cmp = o_cmps[k]
        heads = []
        for g in range(GROUP):
            h = GROUP * k + g
            rs = slice(g * Q_TILE, (g + 1) * Q_TILE)
            g_slc = gates[:, N_HEADS + h:N_HEADS + h + 1] / jnp.maximum(
                a_slc[rs][:, HEAD_DIM:HEAD_DIM + 1], 1e-30)
            g_win = gates[:, 2 * N_HEADS + h:2 * N_HEADS + h + 1] / jnp.maximum(
                a_win[rs][:, HEAD_DIM:HEAD_DIM + 1], 1e-30)
            heads.append(gates[:, h:h + 1] * o_cmp[rs] + g_slc * a_slc[rs] + g_win * a_win[rs])
        low = lax.broadcasted_iota(jnp.int32, (Q_TILE, LANES), 1) < HEAD_DIM
        for pair in range(GROUP // 2):
            both = jnp.where(low, heads[2 * pair], pltpu.roll(heads[2 * pair + 1], HEAD_DIM, axis=1))
            col = (GROUP // 2 * k + pair) * LANES
            o_ref[0, :, col:col + LANES] = both.astype(o_ref.dtype)


def _nsa_prompt(q_aug, gates, cmp, ks_aug, kw_aug, cover):
    b, t, _ = q_aug.shape
    return pl.pallas_call(
        _nsa_prompt_kernel,
        grid=(b, t // Q_TILE),
        in_specs=[
            pl.BlockSpec((1, Q_TILE, N_HEADS * LANES), lambda i, j: (i, j, 0)),
            pl.BlockSpec((1, Q_TILE, LANES), lambda i, j: (i, j, 0)),
            pl.BlockSpec((1,) + cmp.shape[1:], lambda i, j: (i, 0, 0)),
            pl.BlockSpec((1,) + ks_aug.shape[1:], lambda i, j: (i, 0, 0)),
            pl.BlockSpec((1,) + kw_aug.shape[1:], lambda i, j: (i, 0, 0)),
            pl.BlockSpec(cover.shape, lambda i, j: (0, 0)),
        ],
        out_specs=pl.BlockSpec((1, Q_TILE, MIX_ATTN), lambda i, j: (i, j, 0)),
        out_shape=jax.ShapeDtypeStruct((b, t, MIX_ATTN), BF16),
        compiler_params=pltpu.CompilerParams(
            dimension_semantics=("arbitrary", "arbitrary"), vmem_limit_bytes=VMEM_LIMIT),
        name="nsa_prompt",
    )(q_aug, gates, cmp, ks_aug, kw_aug, cover)


def _selection_constants(n_cmp_pad, n_cmp, n_sel):
    ci = np.arange(n_cmp_pad)[:, None] * CMP_STRIDE
    sj = np.arange(LANES)[None, :] * SLC_BLOCK
    cover = ((ci + CMP_BLOCK > sj) & (ci < sj + SLC_BLOCK)
             & (np.arange(n_cmp_pad)[:, None] < n_cmp) & (np.arange(LANES)[None, :] < n_sel))
    return jnp.asarray(cover, BF16)


FF_CHUNK = 1024


def _tail_kernel(*refs, n_mix, has_pre):
    x_ref = refs[0]
    mix_refs = refs[1:1 + n_mix]
    pos = 1 + n_mix
    pre_ref = refs[pos] if has_pre else None
    pos += int(has_pre)
    p_ref = refs[pos]
    wmix_refs = refs[pos + 1:pos + 1 + n_mix]
    (gmlp_ref, wup_ref, wdown_ref, gple_ref, wple_ref, wpg_ref, gfin_ref,
     y_ref) = refs[pos + 1 + n_mix:]
    x = x_ref[...]
    for m_ref, w_ref in zip(mix_refs, wmix_refs):
        x = x + _dot(m_ref[...], w_ref[...])
    if has_pre:
        x = x + pre_ref[...]
    xn = _rms(x, gmlp_ref[...]).astype(BF16)
    acc = jnp.zeros_like(x)
    for c in range(D_FF // FF_CHUNK):
        h = _dot(xn, wup_ref[:, c * FF_CHUNK:(c + 1) * FF_CHUNK])
        h = jnp.square(jnp.maximum(h, 0.0)).astype(BF16)
        acc = acc + _dot(h, wdown_ref[c * FF_CHUNK:(c + 1) * FF_CHUNK, :])
    x = x + acc
    gate = _sigmoid(_dot(_rms(x, gple_ref[...]).astype(BF16), wpg_ref[...]))
    x = x + _dot(p_ref[...].astype(BF16), wple_ref[...]) * gate
    y_ref[...] = _rms(x, gfin_ref[...])


def _layer_tail(x2d, mixes, wmixes, pre, p2d, g_mlp, w_up, w_down, g_ple, w_ple, w_pg, g_final, tm):
    rows = x2d.shape[0]
    row_spec = lambda a: pl.BlockSpec((tm, a.shape[1]), lambda i: (i, 0))
    const = lambda a: pl.BlockSpec(a.shape, lambda i: (0, 0), pipeline_mode=pl.Buffered(1))
    vec = lambda a: a.reshape(1, -1)
    has_pre = pre is not None
    args = [x2d, *mixes] + ([pre] if has_pre else []) + [p2d]
    specs = [row_spec(a) for a in args]
    consts = [*wmixes, vec(g_mlp), w_up, w_down, vec(g_ple), w_ple, w_pg, vec(g_final)]
    return pl.pallas_call(
        functools.partial(_tail_kernel, n_mix=len(mixes), has_pre=has_pre),
        grid=(rows // tm,),
        in_specs=specs + [const(a) for a in consts],
        out_specs=pl.BlockSpec((tm, D_MODEL), lambda i: (i, 0)),
        out_shape=jax.ShapeDtypeStruct((rows, D_MODEL), F32),
        compiler_params=pltpu.CompilerParams(
            dimension_semantics=("arbitrary",), vmem_limit_bytes=VMEM_LIMIT),
        name="layer_tail",
    )(*args, *consts)


def _proj_weights(w_in):
    c_q = Q_COLS
    c_gt = Q_COLS + 3 * KV_COLS
    w_packed = jnp.concatenate([
        w_in[:, :c_q] * (HEAD_DIM ** -0.5), w_in[:, c_q:c_gt],
        jnp.pad(w_in[:, c_gt:c_gt + GATE_COLS], ((0, 0), (0, LANES - GATE_COLS))),
        w_in[:, c_gt + GATE_COLS:]], axis=1).astype(BF16)
    qfill = np.zeros((1, N_HEADS * LANES), np.float32)
    for h, slope in enumerate(_slopes()):
        qfill[0, h * LANES + HEAD_DIM] = SLC_BLOCK * slope
        qfill[0, h * LANES + HEAD_DIM + 1] = slope
    return w_packed, jnp.asarray(qfill)


TQ_PAD = 8
SEL_LANES = 384
BIG = 1e9


def _masked_softmax(s, valid):
    s = jnp.where(valid, s, NEG)
    e = jnp.where(valid, jnp.exp(s - jnp.max(s, -1, keepdims=True)), 0.0)
    return e / jnp.maximum(jnp.sum(e, -1, keepdims=True), 1e-30)


def _rows_gt(x, k, width):
    return jnp.concatenate(
        [x[:, (GROUP * k + g) * LANES:(GROUP * k + g) * LANES + width] for g in range(GROUP)], axis=0)


def _sample_select_kernel(q_ref, cmp_ref, cov_ref, ocmp_ref, idx_ref, *, past_len, n_sel):
    rows = GROUP * TQ_PAD
    t_gt = lax.broadcasted_iota(jnp.int32, (rows, 1), 0) % TQ_PAD
    n_cmp_pad = cmp_ref.shape[1]
    cend = lax.broadcasted_iota(jnp.int32, (1, n_cmp_pad), 1) * CMP_STRIDE + (CMP_BLOCK - 1)
    qpos_t = past_len + lax.broadcasted_iota(jnp.int32, (TQ_PAD, 1), 0)
    jl = lax.broadcasted_iota(jnp.int32, (1, SEL_LANES), 1)
    cur = qpos_t // SLC_BLOCK
    forced = (jl == 0) | (jl == cur) | (jl == cur - 1)
    scores = []
    n_seq = q_ref.shape[0]
    for bb in range(n_seq):
        q = q_ref[bb].astype(F32)
        for k in range(N_KV_HEADS):
            qk = _rows_gt(q, k, LANES).astype(BF16)
            kc = cmp_ref[bb, :, k * LANES:(k + 1) * LANES]
            vc = cmp_ref[bb, :, (2 + k) * LANES:(3 + k) * LANES]
            p = _masked_softmax(_dot_nt(qk, kc), (past_len + t_gt) >= cend)
            ocmp_ref[bb, k] = _dot(p.astype(BF16), vc)
            psum = p[0:TQ_PAD]
            for g in range(1, GROUP):
                psum = psum + p[g * TQ_PAD:(g + 1) * TQ_PAD]
            p_hi = psum.astype(BF16)
            p_lo = (psum - p_hi.astype(F32)).astype(BF16)
            imp = _dot(p_hi, cov_ref[...]) + _dot(p_lo, cov_ref[...])
            score = (jnp.where(jl * SLC_BLOCK <= qpos_t, imp, -1.0)
                     + jnp.where(forced, FORCE_BONUS, 0.0))
            scores.append(jnp.where(jl < n_sel, score, -BIG))
    sc = jnp.concatenate(scores, axis=0)
    lane = lax.broadcasted_iota(jnp.int32, sc.shape, 1).astype(F32)
    out_lane = lax.broadcasted_iota(jnp.int32, (sc.shape[0], LANES), 1)
    picked = jnp.zeros((sc.shape[0], LANES), F32)
    for s in range(TOP_K):
        m = jnp.max(sc, -1, keepdims=True)
        am = jnp.min(jnp.where(sc == m, lane, BIG), -1, keepdims=True)
        picked = jnp.where(out_lane == s, am, picked)
        sc = jnp.where(lane == am, -2.0 * BIG, sc)
    per_seq = N_KV_HEADS * TQ_PAD
    for bb in range(n_seq):
        idx_ref[bb] = picked[bb * per_seq:(bb + 1) * per_seq].astype(jnp.int32)


SELECT_SEQS = 8


def _sample_select(q_s, cmp_s, cover_s, past_len, n_sel):
    db = q_s.shape[0]
    ns = SELECT_SEQS
    return pl.pallas_call(
        functools.partial(_sample_select_kernel, past_len=past_len, n_sel=n_sel),
        grid=(db // ns,),
        in_specs=[
            pl.BlockSpec((ns,) + q_s.shape[1:], lambda i: (i, 0, 0)),
            pl.BlockSpec((ns,) + cmp_s.shape[1:], lambda i: (i, 0, 0)),
            pl.BlockSpec(cover_s.shape, lambda i: (0, 0)),
        ],
        out_specs=[
            pl.BlockSpec((ns, N_KV_HEADS, GROUP * TQ_PAD, LANES), lambda i: (i, 0, 0, 0)),
            pl.BlockSpec((ns, N_KV_HEADS * TQ_PAD, LANES), lambda i: (i, 0, 0)),
        ],
        out_shape=[
            jax.ShapeDtypeStruct((db, N_KV_HEADS, GROUP * TQ_PAD, LANES), F32),
            jax.ShapeDtypeStruct((db, N_KV_HEADS * TQ_PAD, LANES), jnp.int32),
        ],
        compiler_params=pltpu.CompilerParams(
            dimension_semantics=("arbitrary",), vmem_limit_bytes=VMEM_LIMIT),
        name="sample_select",
    )(q_s, cmp_s, cover_s)


def _joint_softmax(s_a, valid_a, s_b, valid_b):
    s_a = jnp.where(valid_a, s_a, NEG)
    s_b = jnp.where(valid_b, s_b, NEG)
    m = jnp.maximum(jnp.max(s_a, -1, keepdims=True), jnp.max(s_b, -1, keepdims=True))
    e_a = jnp.where(valid_a, jnp.exp(s_a - m), 0.0)
    e_b = jnp.where(valid_b, jnp.exp(s_b - m), 0.0)
    inv = 1.0 / jnp.maximum(jnp.sum(e_a, -1, keepdims=True) + jnp.sum(e_b, -1, keepdims=True), 1e-30)
    return e_a * inv, e_b * inv


def _sample_attn_kernel(pt_ref, idx_ref, q_ref, gt_ref, ocmp_ref, ksn_ref, kwn_ref, slc_ref, win_ref,
                        wout_ref, o_ref, kbuf, vbuf, sem, *, past_len, tq, ppb):
    b = pl.program_id(0)
    n_past_blocks = past_len // SLC_BLOCK
    bpp = PAGE_SIZE // SLC_BLOCK
    rows = GROUP * TQ_PAD

    slot = b % 2

    def sel_index(k, t, s, bb=b):
        return idx_ref[((bb * N_KV_HEADS + k) * TQ_PAD + t) * TOP_K + s]

    def block_copies(bb, sl):
        cps = []
        for k in range(N_KV_HEADS):
            for t in range(tq):
                for s in range(TOP_K):
                    blk = jnp.minimum(sel_index(k, t, s, bb), n_past_blocks - 1)
                    page = pt_ref[bb * ppb + blk // bpp]
                    dst = pl.ds(s * PAGE_SIZE, PAGE_SIZE)
                    cps.append(pltpu.make_async_copy(
                        slc_ref.at[page, pl.ds(k * HEAD_DIM, HEAD_DIM), :], kbuf.at[sl, k, t, :, dst],
                        sem.at[sl]))
                    cps.append(pltpu.make_async_copy(
                        slc_ref.at[page, pl.ds((N_KV_HEADS + k) * HEAD_DIM, HEAD_DIM), :],
                        vbuf.at[sl, k, t, :, dst], sem.at[sl]))
        return cps

    @pl.when(b == 0)
    def _():
        for cp in block_copies(b, slot):
            cp.start()

    @pl.when(b + 1 < pl.num_programs(0))
    def _():
        for cp in block_copies(b + 1, 1 - slot):
            cp.start()

    q = q_ref[0].astype(F32)
    gates = gt_ref[0]
    t_gt = lax.broadcasted_iota(jnp.int32, (rows, 1), 0) % TQ_PAD
    g_gt = lax.broadcasted_iota(jnp.int32, (rows, 1), 0) // TQ_PAD
    qpos = past_len + t_gt
    lane_pg = lax.broadcasted_iota(jnp.int32, (1, PAGE_SIZE), 1)
    new_pos = past_len + lax.broadcasted_iota(jnp.int32, (1, TQ_PAD), 1)
    win_len = win_ref.shape[2]
    win_pos = past_len - win_len + lax.broadcasted_iota(jnp.int32, (1, win_len), 1)
    slopes = _slopes()

    for cp in block_copies(b, slot):
        cp.wait()

    acc = jnp.zeros((TQ_PAD, D_MODEL), F32)
    for k in range(N_KV_HEADS):
        qk = _rows_gt(q, k, HEAD_DIM).astype(BF16)
        slope = jnp.zeros((rows, 1), F32)
        for g in range(GROUP):
            slope = jnp.where(g_gt == g, float(slopes[GROUP * k + g]), slope)
        ks_new = ksn_ref[0, :, k * HEAD_DIM:(k + 1) * HEAD_DIM].astype(BF16)
        vs_new = ksn_ref[0, :, (N_KV_HEADS + k) * HEAD_DIM:(N_KV_HEADS + k + 1) * HEAD_DIM].astype(BF16)
        kw_new = kwn_ref[0, :, k * HEAD_DIM:(k + 1) * HEAD_DIM].astype(BF16)
        vw_new = kwn_ref[0, :, (N_KV_HEADS + k) * HEAD_DIM:(N_KV_HEADS + k + 1) * HEAD_DIM].astype(BF16)

        kpos, chosen, owner, s_parts = [], [], [], []
        has_new = jnp.zeros((rows, 1), jnp.int32)
        for t in range(tq):
            n_new = 0
            for s in range(TOP_K):
                blk = sel_index(k, t, s)
                is_past = blk < n_past_blocks
                page_blk = jnp.minimum(blk, n_past_blocks - 1)
                kpos.append((page_blk // bpp) * PAGE_SIZE + lane_pg)
                half = jnp.where(is_past, page_blk % bpp, -1)
                chosen.append(jnp.where((lane_pg // SLC_BLOCK) == half, 1, 0))
                owner.append(jnp.full((1, PAGE_SIZE), t, jnp.int32))
                n_new = n_new + jnp.where(is_past, 0, 1)
            has_new = jnp.where(t_gt == t, n_new, has_new)
            s_parts.append(_dot(qk, kbuf[slot, k, t].astype(BF16)))
        kpos = jnp.concatenate(kpos, axis=1)
        chosen = jnp.concatenate(chosen, axis=1)
        owner = jnp.concatenate(owner, axis=1)
        valid = (t_gt == owner) & (chosen > 0) & (kpos <= qpos)
        valid_new = (new_pos <= qpos) & (has_new > 0) & (t_gt < tq)
        s_past = jnp.concatenate(s_parts, axis=1) - slope * (qpos - kpos).astype(F32)
        s_new = _dot_nt(qk, ks_new) - slope * (qpos - new_pos).astype(F32)
        p_past, p_new = _joint_softmax(s_past, valid, s_new, valid_new)
        p_past = p_past.astype(BF16)
        o_slc = _dot(p_new.astype(BF16), vs_new)
        seg = TOP_K * PAGE_SIZE
        for t in range(tq):
            o_slc = o_slc + _dot_nt(p_past[:, t * seg:(t + 1) * seg], vbuf[slot, k, t].astype(BF16))

        d_w = qpos - win_pos
        d_n = qpos - new_pos
        s_w = _dot(qk, win_ref[0, k * HEAD_DIM:(k + 1) * HEAD_DIM, :].astype(BF16)) - slope * d_w.astype(F32)
        s_n = _dot_nt(qk, kw_new) - slope * d_n.astype(F32)
        p_w, p_n = _joint_softmax(s_w, (d_w >= 0) & (d_w < WINDOW) & (win_pos >= 0),
                                  s_n, (d_n >= 0) & (d_n < WINDOW))
        v_w = win_ref[0, (N_KV_HEADS + k) * HEAD_DIM:(N_KV_HEADS + k + 1) * HEAD_DIM, :].astype(BF16)
        o_win = _dot_nt(p_w.astype(BF16), v_w) + _dot(p_n.astype(BF16), vw_new)

        def gate(branch):
            return jnp.concatenate(
                [gates[:, branch * N_HEADS + GROUP * k + g:branch * N_HEADS + GROUP * k + g + 1]
                 for g in range(GROUP)], axis=0)

        o = (gate(0) * ocmp_ref[0, k][:, 0:HEAD_DIM] + gate(1) * o_slc + gate(2) * o_win).astype(BF16)
        for g in range(GROUP):
            h = GROUP * k + g
            acc = acc + _dot(o[g * TQ_PAD:(g + 1) * TQ_PAD], wout_ref[h * HEAD_DIM:(h + 1) * HEAD_DIM, :])
    o_ref[0] = acc


def _sample_attn(page_table, idx, q_s, gates_s, ocmp, ks_new, kw_new, slc_t, win_t, w_out_attn,
                 past_len, tq):
    db, ppb = page_table.shape
    blk3 = lambda a: pl.BlockSpec((1,) + a.shape[1:], lambda i, pt, ix: (i, 0, 0))
    grid_spec = pltpu.PrefetchScalarGridSpec(
        num_scalar_prefetch=2,
        grid=(db,),
        in_specs=[
            blk3(q_s), blk3(gates_s),
            pl.BlockSpec((1,) + ocmp.shape[1:], lambda i, pt, ix: (i, 0, 0, 0)),
            blk3(ks_new), blk3(kw_new),
            pl.BlockSpec(memory_space=pl.ANY),
            blk3(win_t),
            pl.BlockSpec(w_out_attn.shape, lambda i, pt, ix: (0, 0)),
        ],
        out_specs=pl.BlockSpec((1, TQ_PAD, D_MODEL), lambda i, pt, ix: (i, 0, 0)),
        scratch_shapes=[
            pltpu.VMEM((2, N_KV_HEADS, tq, HEAD_DIM, TOP_K * PAGE_SIZE), F32),
            pltpu.VMEM((2, N_KV_HEADS, tq, HEAD_DIM, TOP_K * PAGE_SIZE), F32),
            pltpu.SemaphoreType.DMA((2,)),
        ],
    )
    return pl.pallas_call(
        functools.partial(_sample_attn_kernel, past_len=past_len, tq=tq, ppb=ppb),
        grid_spec=grid_spec,
        out_shape=jax.ShapeDtypeStruct((db, TQ_PAD, D_MODEL), F32),
        compiler_params=pltpu.CompilerParams(
            dimension_semantics=("arbitrary",), vmem_limit_bytes=VMEM_LIMIT),
        name="sample_attn",
    )(page_table.reshape(-1), idx, q_s, gates_s, ocmp, ks_new, kw_new, slc_t, win_t, w_out_attn)


def _pages_feature_major(cache):
    return jnp.transpose(cache, (0, 2, 3, 4, 1)).reshape(cache.shape[0], KV_COLS, cache.shape[1])


def kernel(x_prompt, x_sample, p_prompt, p_sample, cache_cmp_kv, cache_slc_kv, cache_win_kv, state_conv, page_table, g_attn, w_in, w_cmp1, w_cmp2, pe_cmp, conv_w, conv_b, ln_conv_g, ln_conv_b, w_out, g_mlp, w_up, w_down, g_ple, w_ple, w_ple_gate, g_final):
    b, t, _ = x_prompt.shape
    db, tq, _ = x_sample.shape
    win_buf = cache_win_kv.shape[2]
    kv5 = lambda a, nb, nt: a.reshape(1, nb, nt, 2, N_KV_HEADS, HEAD_DIM)

    w_out_attn = w_out[0][:MIX_ATTN].astype(BF16)
    w_out_conv = w_out[0][MIX_ATTN:].astype(BF16)
    tail_w = (g_mlp[0], w_up[0].astype(BF16), w_down[0].astype(BF16), g_ple[0],
              w_ple[0].astype(BF16), w_ple_gate[0].astype(BF16), g_final)
    g_row = g_attn[0].reshape(1, D_MODEL)

    w_packed, qfill = _proj_weights(w_in[0])
    conv_params = _conv_params(conv_w[0], conv_b[0], ln_conv_g[0], ln_conv_b[0])
    po = _project(x_prompt.reshape(b * t, D_MODEL), g_row, w_packed, qfill, 512, t, prompt=True)
    u_p = po["u"].reshape(b, t, CONV_CH)
    conv_p = _conv_tail(jnp.zeros((b, HALO, CONV_CH), F32), u_p, conv_params, 512)
    w1cat, cw = _compress_weights(w_cmp1[0], w_cmp2[0], pe_cmp[0])
    cmp_p = _compress_paged(None, po["kc_t"], w1cat, *cw, pages=t // PAGE_SIZE)
    n_chunk = t // CMP_STRIDE
    cover = _selection_constants(n_chunk, n_chunk - 1, t // SLC_BLOCK)
    attn_p = _nsa_prompt(po["q_aug"].reshape(b, t, -1), po["gates"].reshape(b, t, LANES), cmp_p,
                         po["ks_aug"].reshape(b, t, -1), po["kw_aug"].reshape(b, t, -1), cover)
    y_prompt = _layer_tail(x_prompt.reshape(b * t, D_MODEL),
                           [attn_p.reshape(b * t, -1), conv_p.reshape(b * t, CONV_CH)],
                           [w_out_attn, w_out_conv], None, p_prompt[0].reshape(b * t, PLE_DIM),
                           *tail_w, tm=512).reshape(b, t, D_MODEL)

    xs = jnp.pad(x_sample, ((0, 0), (0, TQ_PAD - tq), (0, 0))).reshape(db * TQ_PAD, D_MODEL)
    so = _project(xs, g_row, w_packed, qfill, db * TQ_PAD, TQ_PAD, prompt=False)
    rs = lambda a: a.reshape(db, TQ_PAD, -1)
    u_s = rs(so["u"])
    kc_s, ks_s, kw_s = rs(so["kc"])[:, :tq], rs(so["ks"])[:, :tq], rs(so["kw"])[:, :tq]
    state32 = jnp.pad(state_conv[0], ((0, 0), (HALO - (CONV_W - 1), 0), (0, 0)))
    conv_s = _conv_tail(state32, u_s, conv_params, TQ_PAD)

    past_len = page_table.shape[1] * PAGE_SIZE
    assert (past_len + tq) // CMP_STRIDE == past_len // CMP_STRIDE and tq <= TQ_PAD
    n_chunk_s = past_len // CMP_STRIDE
    n_sel_s = past_len // SLC_BLOCK + 1
    cmp_s = _compress_paged(page_table, _pages_feature_major(cache_cmp_kv[0]), w1cat, *cw, pages=64)
    ci = np.arange(n_chunk_s)[:, None] * CMP_STRIDE
    sj = np.arange(SEL_LANES)[None, :] * SLC_BLOCK
    cover_s = jnp.asarray((ci + CMP_BLOCK > sj) & (ci < sj + SLC_BLOCK)
                          & (np.arange(n_chunk_s)[:, None] < n_chunk_s - 1)
                          & (np.arange(SEL_LANES)[None, :] < n_sel_s), BF16)
    q_s = rs(so["q_aug"])
    ocmp_s, idx_s = _sample_select(q_s, cmp_s, cover_s, past_len, n_sel_s)
    pre_s = _sample_attn(page_table, idx_s[:, :, :TOP_K].reshape(-1), q_s, rs(so["gates"]), ocmp_s,
                         rs(so["ks"]), rs(so["kw"]), _pages_feature_major(cache_slc_kv[0]),
                         _pages_feature_major(cache_win_kv[0]), w_out_attn,
                         past_len, tq)
    p_s = jnp.pad(p_sample[0], ((0, 0), (0, TQ_PAD - tq), (0, 0))).reshape(db * TQ_PAD, PLE_DIM)
    y_sample = _layer_tail(xs, [conv_s.reshape(db * TQ_PAD, CONV_CH)], [w_out_conv],
                           pre_s.reshape(db * TQ_PAD, D_MODEL), p_s, *tail_w,
                           tm=db * TQ_PAD).reshape(db, TQ_PAD, D_MODEL)[:, :tq]

    def token_major(a_t):
        nt = a_t.shape[2]
        return jnp.transpose(a_t.reshape(b, 2, N_KV_HEADS, HEAD_DIM, nt), (0, 4, 1, 2, 3))[None]

    new_win = jnp.concatenate([cache_win_kv[:, :, tq:], kv5(kw_s, db, tq)], 2)
    new_conv_s = jnp.concatenate([state_conv[0], u_s[:, :tq]], 1)[:, -(CONV_W - 1):]
    return (y_prompt, y_sample,
            token_major(po["kc_t"]), token_major(po["ks_t"]), token_major(po["kw_t"][:, :, t - win_buf:]),
            u_p[:, -(CONV_W - 1):][None],
            kv5(kc_s, db, tq), kv5(ks_s, db, tq), new_win, new_conv_s[None])
```

```python
import functools

import numpy as np
import jax
import jax.numpy as jnp
from jax import lax
from jax.experimental import pallas as pl
from jax.experimental.pallas import tpu as pltpu

D_MODEL = 1024
N_HEADS = 8
HEAD_DIM = 64
N_KV_HEADS = 2
GROUP = N_HEADS // N_KV_HEADS
MIX_ATTN = N_HEADS * HEAD_DIM
CONV_CH = D_MODEL - MIX_ATTN
CONV_W = 31
CMP_BLOCK = 32
CMP_STRIDE = 16
CMP_HID = 2 * HEAD_DIM
SLC_BLOCK = 64
TOP_K = 16
WINDOW = 512
PAGE_SIZE = 128
D_FF = 4 * D_MODEL
PLE_DIM = 256
Q_COLS = N_HEADS * HEAD_DIM
KV_COLS = 2 * N_KV_HEADS * HEAD_DIM
GATE_COLS = 3 * N_HEADS
EPS = 1e-6
NEG = -1e30
FORCE_BONUS = 1e4

LANES = 128
Q_TILE = 256
VMEM_LIMIT = 56 * 1024 * 1024

F32 = jnp.float32
BF16 = jnp.bfloat16


def _dot(a, b):
    return jnp.dot(a, b, preferred_element_type=F32)


def _dot_nt(a, b):
    return lax.dot_general(a, b, (((1,), (1,)), ((), ())), preferred_element_type=F32)


def _sigmoid(x):
    return 1.0 / (1.0 + jnp.exp(-x))


def _rms(x, g):
    return x * lax.rsqrt(jnp.mean(x * x, -1, keepdims=True) + EPS) * g


def _slopes():
    return 2.0 ** (-8.0 * np.arange(1, N_HEADS + 1) / N_HEADS)


def _expand_heads(y, fill):
    low = lax.broadcasted_iota(jnp.int32, (y.shape[0], LANES), 1) < HEAD_DIM
    tiles = []
    for i in range(y.shape[1] // HEAD_DIM):
        pair = y[:, (i // 2) * LANES:(i // 2 + 1) * LANES]
        data = pair if i % 2 == 0 else pltpu.roll(pair, HEAD_DIM, axis=1)
        tiles.append(jnp.where(low, data, fill[i]))
    return tiles


def _store_tiles(o_ref, tiles):
    for i, t in enumerate(tiles):
        o_ref[:, i * LANES:(i + 1) * LANES] = t.astype(o_ref.dtype)


PROJ_OUTPUTS_PROMPT = ("q_aug", "kc_t", "ks_t", "ks_aug", "kw_t", "kw_aug", "gates", "u")
PROJ_OUTPUTS_SAMPLE = ("q_aug", "kc", "ks", "kw", "gates", "u")


def _proj_kernel(x_ref, g_ref, w_ref, qfill_ref, *out_refs, seq_len, prompt):
    x = x_ref[...]
    xn = _rms(x, g_ref[...]).astype(BF16)
    tm = x.shape[0]
    outs = dict(zip(PROJ_OUTPUTS_PROMPT if prompt else PROJ_OUTPUTS_SAMPLE, out_refs))
    col = [0]

    def matmul(width):
        y = _dot(xn, w_ref[:, col[0]:col[0] + width])
        col[0] += width
        return y

    _store_tiles(outs["q_aug"], _expand_heads(
        matmul(Q_COLS), [qfill_ref[:, h * LANES:(h + 1) * LANES] for h in range(N_HEADS)]))
    if prompt:
        pos = (pl.program_id(0) * tm + lax.broadcasted_iota(jnp.int32, (tm, LANES), 0)) % seq_len
        lane = lax.broadcasted_iota(jnp.int32, (tm, LANES), 1)
        blk_lane = SEL_LANE0 + (pos // SLC_BLOCK) % (SLC_CHUNK // SLC_BLOCK)
        k_fill = (jnp.where(lane == HEAD_DIM, (pos // SLC_BLOCK).astype(F32), 0.0)
                  + jnp.where(lane == HEAD_DIM + 1, (pos % SLC_BLOCK).astype(F32), 0.0)
                  + jnp.where(lane == blk_lane, 1.0, 0.0))
        v_fill = jnp.where(lane == HEAD_DIM, 1.0, 0.0)
        outs["kc_t"][0] = matmul(KV_COLS).T
        for name in ("ks", "kw"):
            y = matmul(KV_COLS)
            outs[name + "_t"][0] = y.T
            _store_tiles(outs[name + "_aug"], _expand_heads(y, [k_fill, k_fill, v_fill, v_fill]))
    else:
        for name in ("kc", "ks", "kw"):
            outs[name][...] = matmul(KV_COLS)
    outs["gates"][...] = _sigmoid(matmul(LANES))
    a = matmul(CONV_CH)
    outs["u"][...] = a * _sigmoid(matmul(CONV_CH))


def _project(x2d, g, w_packed, qfill, tm, seq_len, prompt):
    rows = x2d.shape[0]
    nw = w_packed.shape[1]
    tiles = seq_len // tm if prompt else 1
    widths = {"q_aug": (N_HEADS * LANES, BF16), "kc": (KV_COLS, F32), "ks": (KV_COLS, F32),
              "kw": (KV_COLS, F32), "ks_aug": (4 * LANES, BF16), "kw_aug": (4 * LANES, BF16),
              "gates": (LANES, F32), "u": (CONV_CH, F32)}
    names = PROJ_OUTPUTS_PROMPT if prompt else PROJ_OUTPUTS_SAMPLE
    out_shape, out_specs = [], []
    for name in names:
        if name.endswith("_t"):
            out_shape.append(jax.ShapeDtypeStruct((rows // seq_len, KV_COLS, seq_len), F32))
            out_specs.append(pl.BlockSpec((1, KV_COLS, tm), lambda i: (i // tiles, 0, i % tiles)))
        else:
            width, dt = widths[name]
            out_shape.append(jax.ShapeDtypeStruct((rows, width), dt))
            out_specs.append(pl.BlockSpec((tm, width), lambda i: (i, 0)))
    outs = pl.pallas_call(
        functools.partial(_proj_kernel, seq_len=seq_len, prompt=prompt),
        grid=(rows // tm,),
        in_specs=[
            pl.BlockSpec((tm, D_MODEL), lambda i: (i, 0)),
            pl.BlockSpec((1, D_MODEL), lambda i: (0, 0)),
            pl.BlockSpec((D_MODEL, nw), lambda i: (0, 0), pipeline_mode=pl.Buffered(1)),
            pl.BlockSpec(qfill.shape, lambda i: (0, 0)),
        ],
        out_specs=out_specs,
        out_shape=out_shape,
        compiler_params=pltpu.CompilerParams(
            dimension_semantics=("arbitrary",), vmem_limit_bytes=VMEM_LIMIT),
        name="proj",
    )(x2d, g, w_packed, qfill)
    return dict(zip(names, outs))


HALO = 32


def _conv_ln_swish(u, w_ref, cb_ref, lg_ref, lb_ref, ext_ref, sh_ref):
    tt = u.shape[0]
    ext_ref[HALO:HALO + tt, :] = u
    for r in range(1, 8):
        sh_ref[r] = ext_ref[pl.ds(r, tt + HALO - 8), :]
    w = w_ref[...]
    y = jnp.zeros((tt, CONV_CH), F32) + cb_ref[...]
    for r in range(8):
        for q in range(5 if r == 0 else 4):
            k = 8 * q + r - 2
            if 0 <= k < CONV_W:
                src = ext_ref[8 * q:8 * q + tt, :] if r == 0 else sh_ref[r, 8 * q:8 * q + tt, :]
                y = y + w[k:k + 1, :] * src
    mu = jnp.mean(y, -1, keepdims=True)
    yc = y - mu
    yn = yc * lax.rsqrt(jnp.mean(yc * yc, -1, keepdims=True) + EPS) * lg_ref[...] + lb_ref[...]
    ext_ref[0:HALO, :] = ext_ref[tt:tt + HALO, :]
    return yn * _sigmoid(yn)


def _conv_kernel(st_ref, u_ref, w_ref, cb_ref, lg_ref, lb_ref, o_ref, ext_ref, sh_ref):
    n_seq = u_ref.shape[0]
    for sq in range(n_seq):
        if n_seq == 1:
            @pl.when(pl.program_id(1) == 0)
            def _():
                ext_ref[0:HALO, :] = st_ref[0]
        else:
            ext_ref[0:HALO, :] = st_ref[sq]
        o_ref[sq] = _conv_ln_swish(u_ref[sq], w_ref, cb_ref, lg_ref, lb_ref, ext_ref, sh_ref).astype(o_ref.dtype)


def _conv_params(conv_w, conv_b, ln_g, ln_b):
    row = lambda a: a.reshape(1, CONV_CH)
    return jnp.pad(conv_w, ((0, 1), (0, 0))), row(conv_b), row(ln_g), row(ln_b)


def _conv_scratch(tt):
    return [pltpu.VMEM((HALO + tt, CONV_CH), F32), pltpu.VMEM((8, HALO + tt - 8, CONV_CH), F32)]


def _conv_tail(state32, u, conv_params, tt, seqs=1):
    b, t, _ = u.shape
    assert seqs == 1 or t == tt
    return pl.pallas_call(
        _conv_kernel,
        grid=(b // seqs, t // tt),
        in_specs=[
            pl.BlockSpec((seqs, HALO, CONV_CH), lambda i, j: (i, 0, 0)),
            pl.BlockSpec((seqs, tt, CONV_CH), lambda i, j: (i, j, 0)),
            pl.BlockSpec((HALO, CONV_CH), lambda i, j: (0, 0)),
            pl.BlockSpec((1, CONV_CH), lambda i, j: (0, 0)),
            pl.BlockSpec((1, CONV_CH), lambda i, j: (0, 0)),
            pl.BlockSpec((1, CONV_CH), lambda i, j: (0, 0)),
        ],
        out_specs=pl.BlockSpec((seqs, tt, CONV_CH), lambda i, j: (i, j, 0)),
        out_shape=jax.ShapeDtypeStruct((b, t, CONV_CH), BF16),
        scratch_shapes=_conv_scratch(tt),
        compiler_params=pltpu.CompilerParams(
            dimension_semantics=("arbitrary", "arbitrary"), vmem_limit_bytes=VMEM_LIMIT),
        name="conv_tail",
    )(state32, u, *conv_params)


def _gelu_tanh(x):
    return 0.5 * x * (1.0 + jnp.tanh(np.sqrt(2.0 / np.pi) * (x + 0.044715 * (x * x * x))))


def _pe_term(pe_ref, w1f_ref, pet):
    for c in range(2):
        t = lax.dot_general(pe_ref[c], w1f_ref[c], (((1,), (0,)), ((), ())),
                            precision=lax.Precision.HIGHEST, preferred_element_type=F32)
        pet[c] = jnp.concatenate([t, t], axis=1)


def _cmp_key_columns(n0, n):
    nidx = n0 + lax.broadcasted_iota(jnp.int32, (n, 2 * LANES), 0)
    cend = nidx * CMP_STRIDE + (CMP_BLOCK - 1)
    lane = lax.broadcasted_iota(jnp.int32, (n, 2 * LANES), 1) % LANES
    return (jnp.where(lane == HEAD_DIM, (cend // SLC_BLOCK).astype(F32), 0.0)
            + jnp.where(lane == HEAD_DIM + 1, (cend % SLC_BLOCK).astype(F32), 0.0))


def _compress_paged_kernel(pt_ref, cache_ref, w1_ref, pe_ref, w1f_ref, w2k_ref, w2v_ref, o_ref,
                           stg, xj, pet, sem, tsem, *, pages, n_groups, ppb, n_steps, paged):
    step = pl.program_id(0) * n_groups + pl.program_id(1)
    chunks_per_page = PAGE_SIZE // CMP_STRIDE
    n = pages * chunks_per_page
    pitch = n + 8
    hid = 2 * CMP_HID

    def page_src(seq, page):
        if paged:
            return cache_ref.at[pt_ref[seq * ppb + page]]
        return cache_ref.at[seq, :, pl.ds(pl.multiple_of(page * PAGE_SIZE, PAGE_SIZE), PAGE_SIZE)]

    def page_copies(s, sl):
        return [pltpu.make_async_copy(page_src(s // n_groups, (s % n_groups) * pages + p), stg.at[sl, p],
                                      sem.at[sl]) for p in range(pages)]

    def tail_copy(s, sl):
        return pltpu.make_async_copy(page_src(s // n_groups, (s % n_groups + 1) * pages), stg.at[sl, pages],
                                     tsem.at[sl])

    def has_lookahead(s):
        return s % n_groups + 1 < n_groups

    def start_fetch(s, sl):
        for cp in page_copies(s, sl):
            cp.start()

        @pl.when(has_lookahead(s))
        def _():
            tail_copy(s, sl).start()

    def wait_fetch(s, sl):
        for cp in page_copies(s, sl):
            cp.wait()

        @pl.when(has_lookahead(s))
        def _():
            tail_copy(s, sl).wait()

    def relayout(s, sl):
        def scatter_tokens(x_t, chunk0, n_tok, keep=None):
            for c in range(2):
                xt = x_t[c * LANES:(c + 1) * LANES, :].T
                for i in range(n_tok // 8):
                    rows8 = xt[8 * i:8 * i + 8, :]
                    if keep is not None:
                        rows8 = jnp.where(keep, rows8, 0.0)
                    first = (8 * (i % 2)) * pitch + chunk0 + i // 2
                    xj[sl, c, pl.ds(first, 8, stride=pitch), :] = rows8

        for c in range(2):
            for j in range(CMP_STRIDE):
                xj[sl, c, j * pitch + n:(j + 1) * pitch, :] = jnp.zeros((pitch - n, LANES), F32)
        for p in range(pages):
            scatter_tokens(stg[sl, p], p * chunks_per_page, PAGE_SIZE)
        keep = (lax.broadcasted_iota(jnp.int32, (8, LANES), 0) * 0 + jnp.where(has_lookahead(s), 1, 0)) > 0
        scatter_tokens(stg[sl, pages], n, CMP_STRIDE, keep)

    def mlp(sl):
        aug = _cmp_key_columns(pl.program_id(1) * n, n)
        for c, w2_ref in ((0, w2k_ref), (1, w2v_ref)):
            lhs = jnp.concatenate(
                [xj[sl, c, j * pitch:(j + 1) * pitch, :].astype(BF16) for j in range(CMP_STRIDE)], axis=1)
            res = _dot(lhs, w1_ref[c])
            second_half_next = pltpu.roll(res[:, hid:], pitch - 1, axis=0)
            pre = (res[:, :hid] + second_half_next)[0:n] + pet[c][0:1, :]
            h = _gelu_tanh(pre).astype(BF16)
            out = _dot(h, w2_ref[...])
            if c == 0:
                o_ref[0, :, 0:2 * LANES] = (out + aug).astype(o_ref.dtype)
            else:
                o_ref[0, :, 2 * LANES:] = out.astype(o_ref.dtype)

    @pl.when(step == 0)
    def _():
        _pe_term(pe_ref, w1f_ref, pet)
        for sl in range(2):
            stg[sl, pages] = jnp.zeros((KV_COLS, PAGE_SIZE), F32)
        start_fetch(0, 0)
        if n_steps > 1:
            start_fetch(1, 1)
        wait_fetch(0, 0)
        relayout(0, 0)

    for par in range(2):
        @pl.when(step % 2 == par)
        def _(par=par):
            @pl.when(step + 1 < n_steps)
            def _():
                wait_fetch(step + 1, 1 - par)

            @pl.when(step + 2 < n_steps)
            def _():
                start_fetch(step + 2, par)

            relayout(step + 1, 1 - par)
            mlp(par)


def _compress_paged(page_table, cache_t, w1cat, pe8, w1f, w2k, w2v, pages):
    paged = page_table is not None
    if paged:
        nb, ppb = page_table.shape
    else:
        nb, ppb = cache_t.shape[0], cache_t.shape[2] // PAGE_SIZE
        page_table = jnp.zeros((1, 1), jnp.int32)
    n_groups = ppb // pages
    n = pages * PAGE_SIZE // CMP_STRIDE
    ncols = 2 * LANES + w2v.shape[1]
    once = dict(pipeline_mode=pl.Buffered(1))
    grid_spec = pltpu.PrefetchScalarGridSpec(
        num_scalar_prefetch=1,
        grid=(nb, n_groups),
        in_specs=[
            pl.BlockSpec(memory_space=pl.ANY),
            pl.BlockSpec(w1cat.shape, lambda i, j, pt: (0, 0, 0), **once),
            pl.BlockSpec(pe8.shape, lambda i, j, pt: (0, 0, 0), **once),
            pl.BlockSpec(w1f.shape, lambda i, j, pt: (0, 0, 0), **once),
            pl.BlockSpec(w2k.shape, lambda i, j, pt: (0, 0), **once),
            pl.BlockSpec(w2v.shape, lambda i, j, pt: (0, 0), **once),
        ],
        out_specs=pl.BlockSpec((1, n, ncols), lambda i, j, pt: (i, j, 0)),
        scratch_shapes=[
            pltpu.VMEM((2, pages + 1, KV_COLS, PAGE_SIZE), F32),
            pltpu.VMEM((2, 2, CMP_STRIDE * (n + 8), LANES), F32),
            pltpu.VMEM((2, 8, 2 * CMP_HID), F32),
            pltpu.SemaphoreType.DMA((2,)),
            pltpu.SemaphoreType.DMA((2,)),
        ],
    )
    return pl.pallas_call(
        functools.partial(_compress_paged_kernel, pages=pages, n_groups=n_groups, ppb=ppb,
                          n_steps=nb * n_groups, paged=paged),
        grid_spec=grid_spec,
        out_shape=jax.ShapeDtypeStruct((nb, n_groups * n, ncols), BF16),
        compiler_params=pltpu.CompilerParams(
            dimension_semantics=("arbitrary", "arbitrary"), vmem_limit_bytes=VMEM_LIMIT),
        name="compress_paged",
    )(page_table.reshape(-1), cache_t, w1cat, pe8, w1f, w2k, w2v)


def _compress_weights(w_cmp1, w_cmp2, pe_cmp):
    w1 = w_cmp1
    z = jnp.zeros_like(w1)
    w1big = jnp.stack([jnp.concatenate([w1, z], -1), jnp.concatenate([z, w1], -1)], axis=2)
    w1cat = jnp.concatenate([w1big[:, :CMP_STRIDE], w1big[:, CMP_STRIDE:]], axis=-1)
    w1cat = w1cat.reshape(2, CMP_STRIDE * 2 * HEAD_DIM, 4 * CMP_HID).astype(BF16)
    pe8 = jnp.broadcast_to(pe_cmp.reshape(2, 1, CMP_BLOCK * HEAD_DIM), (2, 8, CMP_BLOCK * HEAD_DIM))
    w1f = w1.reshape(2, CMP_BLOCK * HEAD_DIM, CMP_HID)

    def w2_layout(w2):
        out = jnp.zeros((2, CMP_HID, 2, LANES), F32)
        for h in range(2):
            out = out.at[h, :, h, 0:HEAD_DIM].set(w2)
        return out.reshape(2 * CMP_HID, 2 * LANES).astype(BF16)

    return w1cat, (pe8, w1f, w2_layout(w_cmp2[0]), w2_layout(w_cmp2[1]))


SLC_CHUNK = 1024
SEL_LANE0 = HEAD_DIM + 2
WIN_SPAN = WINDOW + Q_TILE


def _nsa_prompt_kernel(q_ref, gt_ref, cmp_ref, ks_ref, kw_ref, cov_ref, o_ref):
    i = pl.program_id(1)
    qs = i * Q_TILE
    q = q_ref[0]
    gates = gt_ref[0]
    qpos_t = qs + lax.broadcasted_iota(jnp.int32, (Q_TILE, 1), 0)
    qpos_gt = jnp.concatenate([qpos_t] * GROUP, axis=0)
    n_cmp_pad = cmp_ref.shape[1]
    cend = lax.broadcasted_iota(jnp.int32, (1, n_cmp_pad), 1) * CMP_STRIDE + (CMP_BLOCK - 1)
    jl = lax.broadcasted_iota(jnp.int32, (1, LANES), 1)
    cur = qpos_t // SLC_BLOCK
    forced = (jl == 0) | (jl == cur) | (jl == cur - 1)
    n_sel = ks_ref.shape[1] // SLC_BLOCK
    jrow = lax.broadcasted_iota(jnp.int32, (n_sel, Q_TILE), 0)

    kend = qs + Q_TILE
    woff = pl.multiple_of(jnp.maximum(kend - WIN_SPAN, 0), Q_TILE)
    d_w = qpos_t - (woff + lax.broadcasted_iota(jnp.int32, (1, WIN_SPAN), 1))
    bias_w = jnp.tile(jnp.where((d_w >= 0) & (d_w < WINDOW), 0.0, NEG), (GROUP, 1))

    qks, o_cmps, sel_negs, a_wins = [], [], [], []
    for k in range(N_KV_HEADS):
        qk = jnp.concatenate(
            [q[:, (GROUP * k + g) * LANES:(GROUP * k + g + 1) * LANES] for g in range(GROUP)], axis=0)

        s = _dot_nt(qk, kw_ref[0, pl.ds(woff, WIN_SPAN), k * LANES:(k + 1) * LANES]) + bias_w
        p = jnp.exp(s - jnp.max(s, -1, keepdims=True)).astype(BF16)
        a_wins.append(_dot(p, kw_ref[0, pl.ds(woff, WIN_SPAN), (2 + k) * LANES:(3 + k) * LANES]))

        kc = cmp_ref[0, :, k * LANES:(k + 1) * LANES]
        vc = cmp_ref[0, :, (2 + k) * LANES:(3 + k) * LANES]
        valid = qpos_gt >= cend
        s = jnp.where(valid, _dot_nt(qk, kc), NEG)
        e = jnp.where(valid, jnp.exp(s - jnp.max(s, -1, keepdims=True)), 0.0)
        p = e / jnp.maximum(jnp.sum(e, -1, keepdims=True), 1e-30)
        o_cmp = _dot(p.astype(BF16), vc)

        psum = p[0:Q_TILE]
        for g in range(1, GROUP):
            psum = psum + p[g * Q_TILE:(g + 1) * Q_TILE]
        p_hi = psum.astype(BF16)
        p_lo = (psum - p_hi.astype(F32)).astype(BF16)
        imp = _dot(p_hi, cov_ref[...]) + _dot(p_lo, cov_ref[...])
        score = (jnp.where(jl * SLC_BLOCK <= qpos_t, imp, -1.0)
                 + jnp.where(forced, FORCE_BONUS, 0.0))
        sc_t = score.T[0:n_sel]
        cnt = jnp.zeros((n_sel, Q_TILE), jnp.int32)
        for jp in range(n_sel):
            row = sc_t[jp:jp + 1, :]
            tie = jnp.where(jrow > jp, 1, 0)
            cnt = cnt + jnp.where(row > sc_t, 1, jnp.where(row == sc_t, tie, 0))
        sel_t = jnp.where(cnt < TOP_K, 1.0, 0.0)
        sel_t = jnp.concatenate([sel_t, jnp.zeros((LANES - n_sel, Q_TILE), F32)], axis=0)
        qks.append(qk)
        o_cmps.append(o_cmp)
        sel_negs.append(jnp.where(sel_t.T > 0.5, 0.0, NEG))

    rows = GROUP * Q_TILE

    blocks_per_chunk = SLC_CHUNK // SLC_BLOCK
    in_sel_lanes = (jl >= SEL_LANE0) & (jl < SEL_LANE0 + blocks_per_chunk)
    q_f32 = [[q[:, (GROUP * k + g) * LANES:(GROUP * k + g + 1) * LANES].astype(F32) for g in range(GROUP)]
             for k in range(N_KV_HEADS)]

    def slc_chunk(c, carry, bias):
        off = pl.multiple_of(c * SLC_CHUNK, SLC_CHUNK)
        out = []
        for k in range(N_KV_HEADS):
            m, acc = carry[k]
            mask_cols = jnp.where(
                in_sel_lanes, pltpu.roll(sel_negs[k], SEL_LANE0 - c * blocks_per_chunk, axis=1), 0.0)
            qx = jnp.concatenate([(q_f32[k][g] + mask_cols).astype(BF16) for g in range(GROUP)], axis=0)
            s = _dot_nt(qx, ks_ref[0, pl.ds(off, SLC_CHUNK), k * LANES:(k + 1) * LANES])
            if bias is not None:
                s = s + bias
            m_new = jnp.maximum(m, jnp.max(s, -1, keepdims=True))
            p = jnp.exp(s - m_new).astype(BF16)
            acc = jnp.exp(m - m_new) * acc + _dot(
                p, ks_ref[0, pl.ds(off, SLC_CHUNK), (2 + k) * LANES:(3 + k) * LANES])
            out.append((m_new, acc))
        return tuple(out)

    init = tuple((jnp.full((rows, 1), 0.1 * NEG, F32), jnp.zeros((rows, LANES), F32))
                 for _ in range(N_KV_HEADS))
    last = (kend - 1) // SLC_CHUNK
    slc = lax.fori_loop(0, last, lambda c, carry: slc_chunk(c, carry, None), init)
    kpos_last = last * SLC_CHUNK + lax.broadcasted_iota(jnp.int32, (1, SLC_CHUNK), 1)
    slc = slc_chunk(last, slc, jnp.tile(jnp.where(kpos_last <= qpos_t, 0.0, NEG), (GROUP, 1)))

    for k in range(N_KV_HEADS):
        a_win = a_wins[k]
        a_slc = slc[k][1]
        o_cmp = o_cmps[k]
        heads = []
        for g in range(GROUP):
            h = GROUP * k + g
            rs = slice(g * Q_TILE, (g + 1) * Q_TILE)
            g_slc = gates[:, N_HEADS + h:N_HEADS + h + 1] / jnp.maximum(
                a_slc[rs][:, HEAD_DIM:HEAD_DIM + 1], 1e-30)
            g_win = gates[:, 2 * N_HEADS + h:2 * N_HEADS + h + 1] / jnp.maximum(
                a_win[rs][:, HEAD_DIM:HEAD_DIM + 1], 1e-30)
            heads.append(gates[:, h:h + 1] * o_cmp[rs] + g_slc * a_slc[rs] + g_win * a_win[rs])
        low = lax.broadcasted_iota(jnp.int32, (Q_TILE, LANES), 1) < HEAD_DIM
        for pair in range(GROUP // 2):
            both = jnp.where(low, heads[2 * pair], pltpu.roll(heads[2 * pair + 1], HEAD_DIM, axis=1))
            col = (GROUP // 2 * k + pair) * LANES
            o_ref[0, :, col:col + LANES] = both.astype(o_ref.dtype)


def _nsa_prompt(q_aug, gates, cmp, ks_aug, kw_aug, cover):
    b, t, _ = q_aug.shape
    return pl.pallas_call(
        _nsa_prompt_kernel,
        grid=(b, t // Q_TILE),
        in_specs=[
            pl.BlockSpec((1, Q_TILE, N_HEADS * LANES), lambda i, j: (i, j, 0)),
            pl.BlockSpec((1, Q_TILE, LANES), lambda i, j: (i, j, 0)),
            pl.BlockSpec((1,) + cmp.shape[1:], lambda i, j: (i, 0, 0)),
            pl.BlockSpec((1,) + ks_aug.shape[1:], lambda i, j: (i, 0, 0)),
            pl.BlockSpec((1,) + kw_aug.shape[1:], lambda i, j: (i, 0, 0)),
            pl.BlockSpec(cover.shape, lambda i, j: (0, 0)),
        ],
        out_specs=pl.BlockSpec((1, Q_TILE, MIX_ATTN), lambda i, j: (i, j, 0)),
        out_shape=jax.ShapeDtypeStruct((b, t, MIX_ATTN), BF16),
        compiler_params=pltpu.CompilerParams(
            dimension_semantics=("arbitrary", "arbitrary"), vmem_limit_bytes=VMEM_LIMIT),
        name="nsa_prompt",
    )(q_aug, gates, cmp, ks_aug, kw_aug, cover)


def _selection_constants(n_cmp_pad, n_cmp, n_sel):
    ci = np.arange(n_cmp_pad)[:, None] * CMP_STRIDE
    sj = np.arange(LANES)[None, :] * SLC_BLOCK
    cover = ((ci + CMP_BLOCK > sj) & (ci < sj + SLC_BLOCK)
             & (np.arange(n_cmp_pad)[:, None] < n_cmp) & (np.arange(LANES)[None, :] < n_sel))
    return jnp.asarray(cover, BF16)


FF_CHUNK = 1024


def _tail_kernel(*refs, n_mix, has_pre):
    x_ref = refs[0]
    mix_refs = refs[1:1 + n_mix]
    pos = 1 + n_mix
    pre_ref = refs[pos] if has_pre else None
    pos += int(has_pre)
    p_ref = refs[pos]
    wmix_refs = refs[pos + 1:pos + 1 + n_mix]
    (gmlp_ref, wup_ref, wdown_ref, gple_ref, wple_ref, wpg_ref, gfin_ref,
     y_ref) = refs[pos + 1 + n_mix:]
    x = x_ref[...]
    for m_ref, w_ref in zip(mix_refs, wmix_refs):
        x = x + _dot(m_ref[...], w_ref[...])
    if has_pre:
        x = x + pre_ref[...]
    xn = _rms(x, gmlp_ref[...]).astype(BF16)
    acc = jnp.zeros_like(x)
    for c in range(D_FF // FF_CHUNK):
        h = _dot(xn, wup_ref[:, c * FF_CHUNK:(c + 1) * FF_CHUNK])
        h = jnp.square(jnp.maximum(h, 0.0)).astype(BF16)
        acc = acc + _dot(h, wdown_ref[c * FF_CHUNK:(c + 1) * FF_CHUNK, :])
    x = x + acc
    gate = _sigmoid(_dot(_rms(x, gple_ref[...]).astype(BF16), wpg_ref[...]))
    x = x + _dot(p_ref[...].astype(BF16), wple_ref[...]) * gate
    y_ref[...] = _rms(x, gfin_ref[...])


def _layer_tail(x2d, mixes, wmixes, pre, p2d, g_mlp, w_up, w_down, g_ple, w_ple, w_pg, g_final, tm):
    rows = x2d.shape[0]
    row_spec = lambda a: pl.BlockSpec((tm, a.shape[1]), lambda i: (i, 0))
    const = lambda a: pl.BlockSpec(a.shape, lambda i: (0, 0), pipeline_mode=pl.Buffered(1))
    vec = lambda a: a.reshape(1, -1)
    has_pre = pre is not None
    args = [x2d, *mixes] + ([pre] if has_pre else []) + [p2d]
    specs = [row_spec(a) for a in args]
    consts = [*wmixes, vec(g_mlp), w_up, w_down, vec(g_ple), w_ple, w_pg, vec(g_final)]
    return pl.pallas_call(
        functools.partial(_tail_kernel, n_mix=len(mixes), has_pre=has_pre),
        grid=(rows // tm,),
        in_specs=specs + [const(a) for a in consts],
        out_specs=pl.BlockSpec((tm, D_MODEL), lambda i: (i, 0)),
        out_shape=jax.ShapeDtypeStruct((rows, D_MODEL), F32),
        compiler_params=pltpu.CompilerParams(
            dimension_semantics=("arbitrary",), vmem_limit_bytes=VMEM_LIMIT),
        name="layer_tail",
    )(*args, *consts)


def _proj_weights(w_in):
    c_q = Q_COLS
    c_gt = Q_COLS + 3 * KV_COLS
    w_packed = jnp.concatenate([
        w_in[:, :c_q] * (HEAD_DIM ** -0.5), w_in[:, c_q:c_gt],
        jnp.pad(w_in[:, c_gt:c_gt + GATE_COLS], ((0, 0), (0, LANES - GATE_COLS))),
        w_in[:, c_gt + GATE_COLS:]], axis=1).astype(BF16)
    qfill = np.zeros((1, N_HEADS * LANES), np.float32)
    for h, slope in enumerate(_slopes()):
        qfill[0, h * LANES + HEAD_DIM] = SLC_BLOCK * slope
        qfill[0, h * LANES + HEAD_DIM + 1] = slope
    return w_packed, jnp.asarray(qfill)


TQ_PAD = 8
SEL_LANES = 384
BIG = 1e9


def _masked_softmax(s, valid):
    s = jnp.where(valid, s, NEG)
    e = jnp.where(valid, jnp.exp(s - jnp.max(s, -1, keepdims=True)), 0.0)
    return e / jnp.maximum(jnp.sum(e, -1, keepdims=True), 1e-30)


def _rows_gt(x, k, width):
    return jnp.concatenate(
        [x[:, (GROUP * k + g) * LANES:(GROUP * k + g) * LANES + width] for g in range(GROUP)], axis=0)


def _sample_select_kernel(q_ref, cmp_ref, cov_ref, ocmp_ref, idx_ref, *, past_len, n_sel):
    rows = GROUP * TQ_PAD
    t_gt = lax.broadcasted_iota(jnp.int32, (rows, 1), 0) % TQ_PAD
    n_cmp_pad = cmp_ref.shape[1]
    cend = lax.broadcasted_iota(jnp.int32, (1, n_cmp_pad), 1) * CMP_STRIDE + (CMP_BLOCK - 1)
    qpos_t = past_len + lax.broadcasted_iota(jnp.int32, (TQ_PAD, 1), 0)
    jl = lax.broadcasted_iota(jnp.int32, (1, SEL_LANES), 1)
    cur = qpos_t // SLC_BLOCK
    forced = (jl == 0) | (jl == cur) | (jl == cur - 1)
    scores = []
    n_seq = q_ref.shape[0]
    for bb in range(n_seq):
        q = q_ref[bb].astype(F32)
        for k in range(N_KV_HEADS):
            qk = _rows_gt(q, k, LANES).astype(BF16)
            kc = cmp_ref[bb, :, k * LANES:(k + 1) * LANES]
            vc = cmp_ref[bb, :, (2 + k) * LANES:(3 + k) * LANES]
            p = _masked_softmax(_dot_nt(qk, kc), (past_len + t_gt) >= cend)
            ocmp_ref[bb, k] = _dot(p.astype(BF16), vc)
            psum = p[0:TQ_PAD]
            for g in range(1, GROUP):
                psum = psum + p[g * TQ_PAD:(g + 1) * TQ_PAD]
            p_hi = psum.astype(BF16)
            p_lo = (psum - p_hi.astype(F32)).astype(BF16)
            imp = _dot(p_hi, cov_ref[...]) + _dot(p_lo, cov_ref[...])
            score = (jnp.where(jl * SLC_BLOCK <= qpos_t, imp, -1.0)
                     + jnp.where(forced, FORCE_BONUS, 0.0))
            scores.append(jnp.where(jl < n_sel, score, -BIG))
    sc = jnp.concatenate(scores, axis=0)
    lane = lax.broadcasted_iota(jnp.int32, sc.shape, 1).astype(F32)
    out_lane = lax.broadcasted_iota(jnp.int32, (sc.shape[0], LANES), 1)
    picked = jnp.zeros((sc.shape[0], LANES), F32)
    for s in range(TOP_K):
        m = jnp.max(sc, -1, keepdims=True)
        am = jnp.min(jnp.where(sc == m, lane, BIG), -1, keepdims=True)
        picked = jnp.where(out_lane == s, am, picked)
        sc = jnp.where(lane == am, -2.0 * BIG, sc)
    per_seq = N_KV_HEADS * TQ_PAD
    for bb in range(n_seq):
        idx_ref[bb] = picked[bb * per_seq:(bb + 1) * per_seq].astype(jnp.int32)


SELECT_SEQS = 8


def _sample_select(q_s, cmp_s, cover_s, past_len, n_sel):
    db = q_s.shape[0]
    ns = SELECT_SEQS
    return pl.pallas_call(
        functools.partial(_sample_select_kernel, past_len=past_len, n_sel=n_sel),
        grid=(db // ns,),
        in_specs=[
            pl.BlockSpec((ns,) + q_s.shape[1:], lambda i: (i, 0, 0)),
            pl.BlockSpec((ns,) + cmp_s.shape[1:], lambda i: (i, 0, 0)),
            pl.BlockSpec(cover_s.shape, lambda i: (0, 0)),
        ],
        out_specs=[
            pl.BlockSpec((ns, N_KV_HEADS, GROUP * TQ_PAD, LANES), lambda i: (i, 0, 0, 0)),
            pl.BlockSpec((ns, N_KV_HEADS * TQ_PAD, LANES), lambda i: (i, 0, 0)),
        ],
        out_shape=[
            jax.ShapeDtypeStruct((db, N_KV_HEADS, GROUP * TQ_PAD, LANES), F32),
            jax.ShapeDtypeStruct((db, N_KV_HEADS * TQ_PAD, LANES), jnp.int32),
        ],
        compiler_params=pltpu.CompilerParams(
            dimension_semantics=("arbitrary",), vmem_limit_bytes=VMEM_LIMIT),
        name="sample_select",
    )(q_s, cmp_s, cover_s)


def _joint_softmax(s_a, valid_a, s_b, valid_b):
    s_a = jnp.where(valid_a, s_a, NEG)
    s_b = jnp.where(valid_b, s_b, NEG)
    m = jnp.maximum(jnp.max(s_a, -1, keepdims=True), jnp.max(s_b, -1, keepdims=True))
    e_a = jnp.where(valid_a, jnp.exp(s_a - m), 0.0)
    e_b = jnp.where(valid_b, jnp.exp(s_b - m), 0.0)
    inv = 1.0 / jnp.maximum(jnp.sum(e_a, -1, keepdims=True) + jnp.sum(e_b, -1, keepdims=True), 1e-30)
    return e_a * inv, e_b * inv


def _sample_attn_kernel(pt_ref, idx_ref, q_ref, gt_ref, ocmp_ref, ksn_ref, kwn_ref, slc_ref, win_ref,
                        wout_ref, o_ref, kbuf, vbuf, sem, *, past_len, tq, ppb):
    b = pl.program_id(0)
    n_past_blocks = past_len // SLC_BLOCK
    bpp = PAGE_SIZE // SLC_BLOCK
    rows = GROUP * TQ_PAD

    slot = b % 2

    def sel_index(k, t, s, bb=b):
        return idx_ref[((bb * N_KV_HEADS + k) * TQ_PAD + t) * TOP_K + s]

    def block_copies(bb, sl):
        cps = []
        for k in range(N_KV_HEADS):
            for t in range(tq):
                for s in range(TOP_K):
                    blk = jnp.minimum(sel_index(k, t, s, bb), n_past_blocks - 1)
                    page = pt_ref[bb * ppb + blk // bpp]
                    dst = pl.ds(s * PAGE_SIZE, PAGE_SIZE)
                    cps.append(pltpu.make_async_copy(
                        slc_ref.at[page, pl.ds(k * HEAD_DIM, HEAD_DIM), :], kbuf.at[sl, k, t, :, dst],
                        sem.at[sl]))
                    cps.append(pltpu.make_async_copy(
                        slc_ref.at[page, pl.ds((N_KV_HEADS + k) * HEAD_DIM, HEAD_DIM), :],
                        vbuf.at[sl, k, t, :, dst], sem.at[sl]))
        return cps

    @pl.when(b == 0)
    def _():
        for cp in block_copies(b, slot):
            cp.start()

    @pl.when(b + 1 < pl.num_programs(0))
    def _():
        for cp in block_copies(b + 1, 1 - slot):
            cp.start()

    q = q_ref[0].astype(F32)
    gates = gt_ref[0]
    t_gt = lax.broadcasted_iota(jnp.int32, (rows, 1), 0) % TQ_PAD
    g_gt = lax.broadcasted_iota(jnp.int32, (rows, 1), 0) // TQ_PAD
    qpos = past_len + t_gt
    lane_pg = lax.broadcasted_iota(jnp.int32, (1, PAGE_SIZE), 1)
    new_pos = past_len + lax.broadcasted_iota(jnp.int32, (1, TQ_PAD), 1)
    win_len = win_ref.shape[2]
    win_pos = past_len - win_len + lax.broadcasted_iota(jnp.int32, (1, win_len), 1)
    slopes = _slopes()

    for cp in block_copies(b, slot):
        cp.wait()

    acc = jnp.zeros((TQ_PAD, D_MODEL), F32)
    for k in range(N_KV_HEADS):
        qk = _rows_gt(q, k, HEAD_DIM).astype(BF16)
        slope = jnp.zeros((rows, 1), F32)
        for g in range(GROUP):
            slope = jnp.where(g_gt == g, float(slopes[GROUP * k + g]), slope)
        ks_new = ksn_ref[0, :, k * HEAD_DIM:(k + 1) * HEAD_DIM].astype(BF16)
        vs_new = ksn_ref[0, :, (N_KV_HEADS + k) * HEAD_DIM:(N_KV_HEADS + k + 1) * HEAD_DIM].astype(BF16)
        kw_new = kwn_ref[0, :, k * HEAD_DIM:(k + 1) * HEAD_DIM].astype(BF16)
        vw_new = kwn_ref[0, :, (N_KV_HEADS + k) * HEAD_DIM:(N_KV_HEADS + k + 1) * HEAD_DIM].astype(BF16)

        kpos, chosen, owner, s_parts = [], [], [], []
        has_new = jnp.zeros((rows, 1), jnp.int32)
        for t in range(tq):
            n_new = 0
            for s in range(TOP_K):
                blk = sel_index(k, t, s)
                is_past = blk < n_past_blocks
                page_blk = jnp.minimum(blk, n_past_blocks - 1)
                kpos.append((page_blk // bpp) * PAGE_SIZE + lane_pg)
                half = jnp.where(is_past, page_blk % bpp, -1)
                chosen.append(jnp.where((lane_pg // SLC_BLOCK) == half, 1, 0))
                owner.append(jnp.full((1, PAGE_SIZE), t, jnp.int32))
                n_new = n_new + jnp.where(is_past, 0, 1)
            has_new = jnp.where(t_gt == t, n_new, has_new)
            s_parts.append(_dot(qk, kbuf[slot, k, t].astype(BF16)))
        kpos = jnp.concatenate(kpos, axis=1)
        chosen = jnp.concatenate(chosen, axis=1)
        owner = jnp.concatenate(owner, axis=1)
        valid = (t_gt == owner) & (chosen > 0) & (kpos <= qpos)
        valid_new = (new_pos <= qpos) & (has_new > 0) & (t_gt < tq)
        s_past = jnp.concatenate(s_parts, axis=1) - slope * (qpos - kpos).astype(F32)
        s_new = _dot_nt(qk, ks_new) - slope * (qpos - new_pos).astype(F32)
        p_past, p_new = _joint_softmax(s_past, valid, s_new, valid_new)
        p_past = p_past.astype(BF16)
        o_slc = _dot(p_new.astype(BF16), vs_new)
        seg = TOP_K * PAGE_SIZE
        for t in range(tq):
            o_slc = o_slc + _dot_nt(p_past[:, t * seg:(t + 1) * seg], vbuf[slot, k, t].astype(BF16))

        d_w = qpos - win_pos
        d_n = qpos - new_pos
        s_w = _dot(qk, win_ref[0, k * HEAD_DIM:(k + 1) * HEAD_DIM, :].astype(BF16)) - slope * d_w.astype(F32)
        s_n = _dot_nt(qk, kw_new) - slope * d_n.astype(F32)
        p_w, p_n = _joint_softmax(s_w, (d_w >= 0) & (d_w < WINDOW) & (win_pos >= 0),
                                  s_n, (d_n >= 0) & (d_n < WINDOW))
        v_w = win_ref[0, (N_KV_HEADS + k) * HEAD_DIM:(N_KV_HEADS + k + 1) * HEAD_DIM, :].astype(BF16)
        o_win = _dot_nt(p_w.astype(BF16), v_w) + _dot(p_n.astype(BF16), vw_new)

        def gate(branch):
            return jnp.concatenate(
                [gates[:, branch * N_HEADS + GROUP * k + g:branch * N_HEADS + GROUP * k + g + 1]
                 for g in range(GROUP)], axis=0)

        o = (gate(0) * ocmp_ref[0, k][:, 0:HEAD_DIM] + gate(1) * o_slc + gate(2) * o_win).astype(BF16)
        for g in range(GROUP):
            h = GROUP * k + g
            acc = acc + _dot(o[g * TQ_PAD:(g + 1) * TQ_PAD], wout_ref[h * HEAD_DIM:(h + 1) * HEAD_DIM, :])
    o_ref[0] = acc


def _sample_attn(page_table, idx, q_s, gates_s, ocmp, ks_new, kw_new, slc_t, win_t, w_out_attn,
                 past_len, tq):
    db, ppb = page_table.shape
    blk3 = lambda a: pl.BlockSpec((1,) + a.shape[1:], lambda i, pt, ix: (i, 0, 0))
    grid_spec = pltpu.PrefetchScalarGridSpec(
        num_scalar_prefetch=2,
        grid=(db,),
        in_specs=[
            blk3(q_s), blk3(gates_s),
            pl.BlockSpec((1,) + ocmp.shape[1:], lambda i, pt, ix: (i, 0, 0, 0)),
            blk3(ks_new), blk3(kw_new),
            pl.BlockSpec(memory_space=pl.ANY),
            blk3(win_t),
            pl.BlockSpec(w_out_attn.shape, lambda i, pt, ix: (0, 0)),
        ],
        out_specs=pl.BlockSpec((1, TQ_PAD, D_MODEL), lambda i, pt, ix: (i, 0, 0)),
        scratch_shapes=[
            pltpu.VMEM((2, N_KV_HEADS, tq, HEAD_DIM, TOP_K * PAGE_SIZE), F32),
            pltpu.VMEM((2, N_KV_HEADS, tq, HEAD_DIM, TOP_K * PAGE_SIZE), F32),
            pltpu.SemaphoreType.DMA((2,)),
        ],
    )
    return pl.pallas_call(
        functools.partial(_sample_attn_kernel, past_len=past_len, tq=tq, ppb=ppb),
        grid_spec=grid_spec,
        out_shape=jax.ShapeDtypeStruct((db, TQ_PAD, D_MODEL), F32),
        compiler_params=pltpu.CompilerParams(
            dimension_semantics=("arbitrary",), vmem_limit_bytes=VMEM_LIMIT),
        name="sample_attn",
    )(page_table.reshape(-1), idx, q_s, gates_s, ocmp, ks_new, kw_new, slc_t, win_t, w_out_attn)


def _pages_feature_major(cache):
    return jnp.transpose(cache, (0, 2, 3, 4, 1)).reshape(cache.shape[0], KV_COLS, cache.shape[1])


def kernel(x_prompt, x_sample, p_prompt, p_sample, cache_cmp_kv, cache_slc_kv, cache_win_kv, state_conv, page_table, g_attn, w_in, w_cmp1, w_cmp2, pe_cmp, conv_w, conv_b, ln_conv_g, ln_conv_b, w_out, g_mlp, w_up, w_down, g_ple, w_ple, w_ple_gate, g_final):
    b, t, _ = x_prompt.shape
    db, tq, _ = x_sample.shape
    win_buf = cache_win_kv.shape[2]
    kv5 = lambda a, nb, nt: a.reshape(1, nb, nt, 2, N_KV_HEADS, HEAD_DIM)

    w_out_attn = w_out[0][:MIX_ATTN].astype(BF16)
    w_out_conv = w_out[0][MIX_ATTN:].astype(BF16)
    tail_w = (g_mlp[0], w_up[0].astype(BF16), w_down[0].astype(BF16), g_ple[0],
              w_ple[0].astype(BF16), w_ple_gate[0].astype(BF16), g_final)
    g_row = g_attn[0].reshape(1, D_MODEL)

    w_packed, qfill = _proj_weights(w_in[0])
    conv_params = _conv_params(conv_w[0], conv_b[0], ln_conv_g[0], ln_conv_b[0])
    po = _project(x_prompt.reshape(b * t, D_MODEL), g_row, w_packed, qfill, 512, t, prompt=True)
    u_p = po["u"].reshape(b, t, CONV_CH)
    conv_p = _conv_tail(jnp.zeros((b, HALO, CONV_CH), F32), u_p, conv_params, 512)
    w1cat, cw = _compress_weights(w_cmp1[0], w_cmp2[0], pe_cmp[0])
    cmp_p = _compress_paged(None, po["kc_t"], w1cat, *cw, pages=t // PAGE_SIZE)
    n_chunk = t // CMP_STRIDE
    cover = _selection_constants(n_chunk, n_chunk - 1, t // SLC_BLOCK)
    attn_p = _nsa_prompt(po["q_aug"].reshape(b, t, -1), po["gates"].reshape(b, t, LANES), cmp_p,
                         po["ks_aug"].reshape(b, t, -1), po["kw_aug"].reshape(b, t, -1), cover)
    y_prompt = _layer_tail(x_prompt.reshape(b * t, D_MODEL),
                           [attn_p.reshape(b * t, -1), conv_p.reshape(b * t, CONV_CH)],
                           [w_out_attn, w_out_conv], None, p_prompt[0].reshape(b * t, PLE_DIM),
                           *tail_w, tm=512).reshape(b, t, D_MODEL)

    xs = jnp.pad(x_sample, ((0, 0), (0, TQ_PAD - tq), (0, 0))).reshape(db * TQ_PAD, D_MODEL)
    so = _project(xs, g_row, w_packed, qfill, db * TQ_PAD, TQ_PAD, prompt=False)
    rs = lambda a: a.reshape(db, TQ_PAD, -1)
    u_s = rs(so["u"])
    kc_s, ks_s, kw_s = rs(so["kc"])[:, :tq], rs(so["ks"])[:, :tq], rs(so["kw"])[:, :tq]
    state32 = jnp.pad(state_conv[0], ((0, 0), (HALO - (CONV_W - 1), 0), (0, 0)))
    conv_s = _conv_tail(state32, u_s, conv_params, TQ_PAD, seqs=8)

    past_len = page_table.shape[1] * PAGE_SIZE
    assert (past_len + tq) // CMP_STRIDE == past_len // CMP_STRIDE and tq <= TQ_PAD
    n_chunk_s = past_len // CMP_STRIDE
    n_sel_s = past_len // SLC_BLOCK + 1
    cmp_s = _compress_paged(page_table, _pages_feature_major(cache_cmp_kv[0]), w1cat, *cw, pages=64)
    ci = np.arange(n_chunk_s)[:, None] * CMP_STRIDE
    sj = np.arange(SEL_LANES)[None, :] * SLC_BLOCK
    cover_s = jnp.asarray((ci + CMP_BLOCK > sj) & (ci < sj + SLC_BLOCK)
                          & (np.arange(n_chunk_s)[:, None] < n_chunk_s - 1)
                          & (np.arange(SEL_LANES)[None, :] < n_sel_s), BF16)
    q_s = rs(so["q_aug"])
    ocmp_s, idx_s = _sample_select(q_s, cmp_s, cover_s, past_len, n_sel_s)
    pre_s = _sample_attn(page_table, idx_s[:, :, :TOP_K].reshape(-1), q_s, rs(so["gates"]), ocmp_s,
                         rs(so["ks"]), rs(so["kw"]), _pages_feature_major(cache_slc_kv[0]),
                         _pages_feature_major(cache_win_kv[0]), w_out_attn,
                         past_len, tq)
    p_s = jnp.pad(p_sample[0], ((0, 0), (0, TQ_PAD - tq), (0, 0))).reshape(db * TQ_PAD, PLE_DIM)
    y_sample = _layer_tail(xs, [conv_s.reshape(db * TQ_PAD, CONV_CH)], [w_out_conv],
                           pre_s.reshape(db * TQ_PAD, D_MODEL), p_s, *tail_w,
                           tm=db * TQ_PAD).reshape(db, TQ_PAD, D_MODEL)[:, :tq]

    def token_major(a_t):
        nt = a_t.shape[2]
        return jnp.transpose(a_t.reshape(b, 2, N_KV_HEADS, HEAD_DIM, nt), (0, 4, 1, 2, 3))[None]

    new_win = jnp.concatenate([cache_win_kv[:, :, tq:], kv5(kw_s, db, tq)], 2)
    new_conv_s = jnp.concatenate([state_conv[0], u_s[:, :tq]], 1)[:, -(CONV_W - 1):]
    return (y_prompt, y_sample,
            token_major(po["kc_t"]), token_major(po["ks_t"]), token_major(po["kw_t"][:, :, t - win_buf:]),
            u_p[:, -(CONV_W - 1):][None],
            kv5(kc_s, db, tq), kv5(ks_s, db, tq), new_win, new_conv_s[None])
```
